```python
import math
import jax, jax.numpy as jnp
from jax import lax
import numpy as np

D_MODEL = 1024
BATCH = 2
SEQ = 8192
DEPTH = 1

D_MIX = 2 * D_MODEL
D_SSD = D_MIX // 2
D_ATTN = D_MIX - D_SSD
SSD_HEAD_DIM = 64
SSD_HEADS = D_SSD // SSD_HEAD_DIM
SSD_GROUPS = 2
SSD_STATE = 128
CONV_WIDTH = 4
CHUNK = 128
D_XBC = D_SSD + 2 * SSD_GROUPS * SSD_STATE
ATTN_QK_DIM = 64
ATTN_HEADS = D_ATTN // (2 * ATTN_QK_DIM)
ATTN_V_DIM = 2 * ATTN_QK_DIM
D_QK = ATTN_HEADS * 2 * ATTN_QK_DIM
Q_BLOCK = 128
OFF_XBC = D_SSD
OFF_DT = OFF_XBC + D_XBC
OFF_Q = OFF_DT + SSD_HEADS
OFF_K = OFF_Q + D_QK
OFF_V = OFF_K + D_QK
OFF_ZA = OFF_V + ATTN_HEADS * ATTN_V_DIM
D_PROJ = OFF_ZA + D_ATTN
ALPHA = (2.0 * DEPTH) ** 0.25
BETA = (8.0 * DEPTH) ** -0.25
EPS = 1e-5

kernel_name = "hymba_ssd_diffattn_deepnorm_adaln"


def layer_norm(x, g, b):
    xf = x.astype(jnp.float32)
    mu = jnp.mean(xf, axis=-1, keepdims=True)
    var = jnp.mean(jnp.square(xf - mu), axis=-1, keepdims=True)
    y = (xf - mu) * lax.rsqrt(var + EPS)
    return (y * g + b).astype(x.dtype)


def rms_norm(x, w):
    xf = x.astype(jnp.float32)
    y = xf * lax.rsqrt(jnp.mean(jnp.square(xf), axis=-1, keepdims=True) + EPS)
    return (y * w).astype(x.dtype)


def group_rms_norm(y, w, groups):
    shp = y.shape
    yf = y.astype(jnp.float32).reshape(*shp[:-1], groups, shp[-1] // groups)
    yf = yf * lax.rsqrt(jnp.mean(jnp.square(yf), axis=-1, keepdims=True) + EPS)
    return yf.reshape(shp) * w


def causal_depthwise_conv(u, w, b):
    ch = u.shape[-1]
    out = lax.conv_general_dilated(
        u, w[:, None, :].astype(u.dtype), window_strides=(1,),
        padding=[(CONV_WIDTH - 1, 0)],
        dimension_numbers=('NWC', 'WIO', 'NWC'),
        feature_group_count=ch)
    return out + b


def decay_matrix(a_cum):
    t = a_cum.shape[-1]
    diff = a_cum[..., :, None] - a_cum[..., None, :]
    mask = jnp.tril(jnp.ones((t, t), dtype=bool))
    return jnp.exp(jnp.where(mask, diff, -jnp.inf))


def ssd_chunked(x, dt, a, bmat, cmat):
    bsz, seq, nh, hp = x.shape
    g, n = bmat.shape[2], bmat.shape[3]
    e = nh // g
    nc = seq // CHUNK
    xc = (x * dt[..., None]).reshape(bsz, nc, CHUNK, g, e, hp)
    a_dt = (dt * a).reshape(bsz, nc, CHUNK, g, e).transpose(0, 3, 4, 1, 2)
    bc = bmat.reshape(bsz, nc, CHUNK, g, n)
    cc = cmat.reshape(bsz, nc, CHUNK, g, n)
    a_cum = jnp.cumsum(a_dt, axis=-1)
    lmat = decay_matrix(a_cum)
    y_diag = jnp.einsum('bclgn,bcsgn,bgecls,bcsgep->bclgep', cc, bc, lmat, xc)
    decay_states = jnp.exp(a_cum[..., -1:] - a_cum)
    states = jnp.einsum('bclgn,bgecl,bclgep->bcgepn', bc, decay_states, xc)
    chunk_decay = jnp.exp(a_cum[..., -1])

    def step(hstate, inp):
        s_c, d_c = inp
        return hstate * d_c[..., None, None] + s_c, hstate

    init = jnp.zeros((bsz, g, e, hp, n), jnp.float32)
    _, prev = lax.scan(step, init, (states.transpose(1, 0, 2, 3, 4, 5),
                                    chunk_decay.transpose(3, 0, 1, 2)))
    y_off = jnp.einsum('bclgn,cbgepn,bgecl->bclgep', cc, prev, jnp.exp(a_cum))
    return (y_diag + y_off).reshape(bsz, seq, nh, hp)


def diff_attention(q, k, v, lam):
    bsz, seq, nh, _, dk = q.shape
    nb = seq // Q_BLOCK
    qb = (q * (dk ** -0.5)).reshape(bsz, nb, Q_BLOCK, nh, 2, dk).transpose(1, 0, 2, 3, 4, 5)
    k_pos = jnp.arange(seq)

    def block(args):
        q_blk, i = args
        s = jnp.einsum('bqhjd,bkhjd->bhjqk', q_blk, k).astype(jnp.float32)
        q_pos = i * Q_BLOCK + jnp.arange(Q_BLOCK)
        mask = k_pos[None, :] <= q_pos[:, None]
        p = jax.nn.softmax(jnp.where(mask, s, -jnp.inf), axis=-1)
        w = p[:, :, 0] - lam * p[:, :, 1]
        return jnp.einsum('bhqk,bkhd->bqhd', w.astype(v.dtype), v)

    out = lax.map(block, (qb, jnp.arange(nb)))
    return out.transpose(1, 0, 2, 3, 4).reshape(bsz, seq, nh, -1)


def hybrid_layer(x, c, w_ada, b_ada, w_in, conv_w, conv_b, dt_bias, a_log, d_skip,
                 ssd_norm_w, lambda_q1, lambda_k1, lambda_q2, lambda_k2, attn_norm_w,
                 w_out, ln_g, ln_b, layer_idx):
    bsz, seq, _ = x.shape
    mod = c @ w_ada + b_ada
    shift, scale, gate = jnp.split(mod, 3, axis=-1)
    h = x * (1.0 + scale[:, None, :]) + shift[:, None, :]
    proj = h @ w_in
    z_ssd, xbc, dt_raw, q, k, v, z_attn = jnp.split(
        proj, [OFF_XBC, OFF_DT, OFF_Q, OFF_K, OFF_V, OFF_ZA], axis=-1)

    xbc = jax.nn.silu(causal_depthwise_conv(xbc, conv_w, conv_b))
    xs, bm, cm = jnp.split(xbc, [D_SSD, D_SSD + SSD_GROUPS * SSD_STATE], axis=-1)
    dt = jax.nn.softplus(dt_raw.astype(jnp.float32) + dt_bias.astype(jnp.float32))
    a = -jnp.exp(a_log.astype(jnp.float32))
    xs_h = xs.astype(jnp.float32).reshape(bsz, seq, SSD_HEADS, SSD_HEAD_DIM)
    y = ssd_chunked(xs_h, dt, a,
                    bm.astype(jnp.float32).reshape(bsz, seq, SSD_GROUPS, SSD_STATE),
                    cm.astype(jnp.float32).reshape(bsz, seq, SSD_GROUPS, SSD_STATE))
    y = y + d_skip.astype(jnp.float32)[:, None] * xs_h
    y = y.reshape(bsz, seq, D_SSD) * jax.nn.silu(z_ssd.astype(jnp.float32))
    y_ssd = group_rms_norm(y, ssd_norm_w.astype(jnp.float32), SSD_GROUPS).astype(x.dtype)

    lambda_init = 0.8 - 0.6 * math.exp(-0.3 * layer_idx)
    lam = (jnp.exp(jnp.sum(lambda_q1.astype(jnp.float32) * lambda_k1.astype(jnp.float32)))
           - jnp.exp(jnp.sum(lambda_q2.astype(jnp.float32) * lambda_k2.astype(jnp.float32)))
           + lambda_init)
    o = diff_attention(q.reshape(bsz, seq, ATTN_HEADS, 2, ATTN_QK_DIM),
                       k.reshape(bsz, seq, ATTN_HEADS, 2, ATTN_QK_DIM),
                       v.reshape(bsz, seq, ATTN_HEADS, ATTN_V_DIM), lam)
    o = rms_norm(o, attn_norm_w) * (1.0 - lambda_init)
    y_attn = o.reshape(bsz, seq, D_ATTN) * jax.nn.silu(z_attn)

    mixed = jnp.concatenate([y_ssd, y_attn], axis=-1) @ w_out
    return layer_norm(ALPHA * x + gate[:, None, :] * mixed, ln_g, ln_b)


def setup_inputs(seed: int = 0) -> dict:
    key = jax.random.key(seed)
    ks = jax.random.split(key, 20)
    f32 = jnp.float32
    x = jax.random.normal(ks[0], (BATCH, SEQ, D_MODEL), f32)
    c = jax.random.normal(ks[1], (BATCH, D_MODEL), f32)
    w_ada = jax.random.normal(ks[2], (DEPTH, D_MODEL, 3 * D_MODEL), f32) * (0.5 * D_MODEL ** -0.5)
    b_ada = jax.random.normal(ks[3], (DEPTH, 3 * D_MODEL), f32) * 0.01
    w_in = jax.random.normal(ks[4], (DEPTH, D_MODEL, D_PROJ), f32) * D_MODEL ** -0.5
    conv_w = jax.random.normal(ks[5], (DEPTH, CONV_WIDTH, D_XBC), f32) * CONV_WIDTH ** -0.5
    conv_b = jax.random.normal(ks[6], (DEPTH, D_XBC), f32) * 0.01
    dt0 = jnp.exp(jax.random.uniform(ks[7], (DEPTH, SSD_HEADS), f32,
                                     math.log(1e-3), math.log(1e-1)))
    dt_bias = dt0 + jnp.log(-jnp.expm1(-dt0))
    a_log = jnp.log(jax.random.uniform(ks[8], (DEPTH, SSD_HEADS), f32, 1.0, 16.0))
    d_skip = 1.0 + 0.1 * jax.random.normal(ks[9], (DEPTH, SSD_HEADS), f32)
    ssd_norm_w = 1.0 + 0.01 * jax.random.normal(ks[10], (DEPTH, D_SSD), f32)
    lambda_q1 = 0.1 * jax.random.normal(ks[11], (DEPTH, ATTN_QK_DIM), f32)
    lambda_k1 = 0.1 * jax.random.normal(ks[12], (DEPTH, ATTN_QK_DIM), f32)
    lambda_q2 = 0.1 * jax.random.normal(ks[13], (DEPTH, ATTN_QK_DIM), f32)
    lambda_k2 = 0.1 * jax.random.normal(ks[14], (DEPTH, ATTN_QK_DIM), f32)
    attn_norm_w = 1.0 + 0.01 * jax.random.normal(ks[15], (DEPTH, ATTN_V_DIM), f32)
    w_out = jax.random.normal(ks[16], (DEPTH, D_MIX, D_MODEL), f32) * (D_MIX ** -0.5 * BETA)
    ln_g = 1.0 + 0.01 * jax.random.normal(ks[17], (DEPTH, D_MODEL), f32)
    ln_b = 0.01 * jax.random.normal(ks[18], (DEPTH, D_MODEL), f32)
    return {"x": x, "c": c, "w_ada": w_ada, "b_ada": b_ada, "w_in": w_in,
            "conv_w": conv_w, "conv_b": conv_b, "dt_bias": dt_bias, "a_log": a_log,
            "d_skip": d_skip, "ssd_norm_w": ssd_norm_w, "lambda_q1": lambda_q1,
            "lambda_k1": lambda_k1, "lambda_q2": lambda_q2, "lambda_k2": lambda_k2,
            "attn_norm_w": attn_norm_w, "w_out": w_out, "ln_g": ln_g, "ln_b": ln_b}


def reference(x, c, w_ada, b_ada, w_in, conv_w, conv_b, dt_bias, a_log, d_skip,
              ssd_norm_w, lambda_q1, lambda_k1, lambda_q2, lambda_k2, attn_norm_w,
              w_out, ln_g, ln_b):
    for l in range(DEPTH):
        x = hybrid_layer(x, c, w_ada[l], b_ada[l], w_in[l], conv_w[l], conv_b[l],
                         dt_bias[l], a_log[l], d_skip[l], ssd_norm_w[l],
                         lambda_q1[l], lambda_k1[l], lambda_q2[l], lambda_k2[l],
                         attn_norm_w[l], w_out[l], ln_g[l], ln_b[l], l)
    return x
```

```python
import functools
import math

import jax
import jax.numpy as jnp
from jax import lax
from jax.experimental import pallas as pl
from jax.experimental.pallas import tpu as pltpu

F32 = jnp.float32
BF16 = jnp.bfloat16

SSD_HEAD_DIM = 64
SSD_GROUPS = 2
SSD_STATE = 128
CONV_WIDTH = 4
CHUNK = 128
ATTN_QK_DIM = 64
ATTN_V_DIM = 2 * ATTN_QK_DIM
EPS = 1e-5
LAMBDA_INIT_L0 = 0.8 - 0.6 * math.exp(-0.3 * 0)

V7X_LANES = 128
V7X_SUBLANES = 8
V7X_VMEM_BYTES = 64 * 1024 * 1024
V7X_VMEM_USABLE_BYTES = 56 * 1024 * 1024

TM_INPROJ = 512
TQ_ATTN = 512
TM_OUT = 512


def _vmem_limit(estimate_bytes):
    return int(min(V7X_VMEM_USABLE_BYTES, max(32 * 1024 * 1024, estimate_bytes)))


def _silu(x):
    return x * (1.0 / (1.0 + jnp.exp(-x)))


def _softplus(x):
    return jnp.maximum(x, 0.0) + jnp.log(1.0 + jnp.exp(-jnp.abs(x)))


def _split_bf16(a, pieces):
    out = []
    r = a
    for i in range(pieces):
        p = r.astype(BF16)
        out.append(p)
        if i + 1 < pieces:
            r = r - p.astype(F32)
    return out


def _dot(a, b):
    return jnp.dot(a, b, preferred_element_type=F32)


def _dot_nt(a, b):
    return lax.dot_general(a, b, (((1,), (1,)), ((), ())), preferred_element_type=F32)


def _dot_tn(a, b):
    return lax.dot_general(a, b, (((0,), (0,)), ((), ())), preferred_element_type=F32)


def _mod_kernel(c_ref, w_ref, b_ref, o_ref):
    c_hi, c_lo = _split_bf16(c_ref[...], 2)
    w_hi, w_lo = _split_bf16(w_ref[...], 2)
    o_ref[...] = _dot(c_hi, w_hi) + _dot(c_hi, w_lo) + _dot(c_lo, w_hi) + b_ref[...]


def _modulation(c, w_ada, b_ada):
    bsz, d = c.shape
    n = w_ada.shape[1]
    rows = V7X_SUBLANES
    c_pad = jnp.zeros((rows, d), F32).at[:bsz].set(c)
    bn = d
    out = pl.pallas_call(
        _mod_kernel,
        grid=(n // bn,),
        in_specs=[pl.BlockSpec((rows, d), lambda j: (0, 0)),
                  pl.BlockSpec((d, bn), lambda j: (0, j)),
                  pl.BlockSpec((1, bn), lambda j: (0, j))],
        out_specs=pl.BlockSpec((rows, bn), lambda j: (0, j)),
        out_shape=jax.ShapeDtypeStruct((rows, n), F32),
        name="modulation",
    )(c_pad, w_ada, b_ada.reshape(1, n))
    return out[:bsz]


def _inproj_kernel(x_ref, scale_ref, shift_ref, w_ref, wdt_ref, wdtT_ref, convw_ref, convb_ref,
                   dtb_ref, dtbT_ref,
                   zs_ref, xs_ref, bm_ref, cm_ref, q_ref, k_ref, v_ref, za_ref, dt_ref, dtT_ref,
                   carry_ref, ext_ref, *, tm, d_ssd, d_xbc, d_attn, n_bc):
    @pl.when(pl.program_id(1) == 0)
    def _():
        carry_ref[...] = jnp.zeros_like(carry_ref)

    h32 = x_ref[0] * (1.0 + scale_ref[0]) + shift_ref[0]
    h = h32.astype(BF16)

    def proj(lo, width):
        return _dot(h, w_ref[:, lo:lo + width])

    off = 0
    zs_ref[0] = _silu(proj(off, d_ssd)).astype(BF16)
    off += d_ssd

    u = proj(off, d_xbc)
    off += d_xbc
    pad = V7X_SUBLANES
    ext_ref[0:pad, :] = carry_ref[...]
    ext_ref[pad:pad + tm, :] = u
    carry_ref[...] = u[tm - pad:tm, :]
    cw = convw_ref[...]
    acc = convb_ref[...] + cw[CONV_WIDTH - 1:CONV_WIDTH, :] * u
    for tap in range(CONV_WIDTH - 1):
        shift = CONV_WIDTH - 1 - tap
        acc = acc + cw[tap:tap + 1, :] * ext_ref[pad - shift:pad - shift + tm, :]
    xbc = _silu(acc)
    xs_ref[0] = xbc[:, :d_ssd].astype(BF16)
    bm_ref[0] = xbc[:, d_ssd:d_ssd + n_bc].astype(BF16)
    cm_ref[0] = xbc[:, d_ssd + n_bc:d_ssd + 2 * n_bc].astype(BF16)

    q_ref[0] = (proj(off, d_attn) * (ATTN_QK_DIM ** -0.5)).astype(BF16)
    off += d_attn
    k_ref[0] = proj(off, d_attn).astype(BF16)
    off += d_attn
    v_ref[0] = proj(off, d_attn).astype(BF16)
    off += d_attn
    za_ref[0] = _silu(proj(off, d_attn)).astype(BF16)

    h_lo = (h32 - h.astype(F32)).astype(BF16)
    w_hi, w_lo = _split_bf16(wdt_ref[...], 2)
    dt_raw = _dot(h, w_hi) + _dot(h, w_lo) + _dot(h_lo, w_hi)
    dt_ref[0] = _softplus(dt_raw + dtb_ref[...])
    wT_hi, wT_lo = _split_bf16(wdtT_ref[...], 2)
    dtT_raw = _dot_nt(wT_hi, h) + _dot_nt(wT_lo, h) + _dot_nt(wT_hi, h_lo)
    dtT_ref[0] = _softplus(dtT_raw + dtbT_ref[...])


def _in_projection(x, scale, shift, w_main, w_dt, conv_w, conv_b, dt_bias, *, d_ssd, d_xbc, d_attn):
    bsz, seq, d = x.shape
    tm = TM_INPROJ
    nh = w_dt.shape[1]
    n_bc = (d_xbc - d_ssd) // 2
    n_main = w_main.shape[1]
    grid = (bsz, seq // tm)

    row = lambda b, l: (b, l, 0)
    const2 = lambda b, l: (0, 0)
    per_b = lambda b, l: (b, 0, 0)
    single = pl.Buffered(1)
    in_specs = [
        pl.BlockSpec((1, tm, d), row),
        pl.BlockSpec((1, 1, d), per_b),
        pl.BlockSpec((1, 1, d), per_b),
        pl.BlockSpec((d, n_main), const2, pipeline_mode=single),
        pl.BlockSpec((d, nh), const2),
        pl.BlockSpec((nh, d), const2),
        pl.BlockSpec((CONV_WIDTH, d_xbc), const2),
        pl.BlockSpec((1, d_xbc), const2),
        pl.BlockSpec((1, nh), const2),
        pl.BlockSpec((nh, 1), const2),
    ]
    out_specs = [
        pl.BlockSpec((1, tm, d_ssd), row),
        pl.BlockSpec((1, tm, d_ssd), row),
        pl.BlockSpec((1, tm, n_bc), row),
        pl.BlockSpec((1, tm, n_bc), row),
        pl.BlockSpec((1, tm, d_attn), row),
        pl.BlockSpec((1, tm, d_attn), row),
        pl.BlockSpec((1, tm, d_attn), row),
        pl.BlockSpec((1, tm, d_attn), row),
        pl.BlockSpec((1, tm, nh), row),
        pl.BlockSpec((1, nh, tm), lambda b, l: (b, 0, l)),
    ]
    out_shape = [
        jax.ShapeDtypeStruct((bsz, seq, d_ssd), BF16),
        jax.ShapeDtypeStruct((bsz, seq, d_ssd), BF16),
        jax.ShapeDtypeStruct((bsz, seq, n_bc), BF16),
        jax.ShapeDtypeStruct((bsz, seq, n_bc), BF16),
        jax.ShapeDtypeStruct((bsz, seq, d_attn), BF16),
        jax.ShapeDtypeStruct((bsz, seq, d_attn), BF16),
        jax.ShapeDtypeStruct((bsz, seq, d_attn), BF16),
        jax.ShapeDtypeStruct((bsz, seq, d_attn), BF16),
        jax.ShapeDtypeStruct((bsz, seq, nh), F32),
        jax.ShapeDtypeStruct((bsz, nh, seq), F32),
    ]
    pad = V7X_SUBLANES
    vmem = (d * n_main * 2 + 2 * tm * d * 4 + 2 * tm * (2 * d_ssd + 2 * n_bc + 4 * d_attn) * 2
            + (tm + 2 * pad) * d_xbc * 4 + 6 * tm * d_xbc * 4)
    kern = functools.partial(_inproj_kernel, tm=tm, d_ssd=d_ssd, d_xbc=d_xbc, d_attn=d_attn, n_bc=n_bc)
    return pl.pallas_call(
        kern, grid=grid, in_specs=in_specs, out_specs=out_specs, out_shape=out_shape,
        scratch_shapes=[pltpu.VMEM((pad, d_xbc), F32), pltpu.VMEM((tm + pad, d_xbc), F32)],
        compiler_params=pltpu.CompilerParams(
            dimension_semantics=("arbitrary", "arbitrary"), vmem_limit_bytes=_vmem_limit(vmem)),
        name="in_projection",
    )(x, scale, shift, w_main, w_dt, w_dt.T, conv_w, conv_b.reshape(1, -1),
      dt_bias.reshape(1, -1), dt_bias.reshape(-1, 1))


def _ssd_kernel(xs_ref, bm_ref, cm_ref, zs_ref, dt_ref, dtT_ref, alog_ref, alogT_ref, dskip_ref,
                normw_ref, y_ref, state_ref, ybuf_ref, *, nheads, d_ssd):
    t = CHUNK
    n = SSD_STATE
    hp = SSD_HEAD_DIM
    gw = d_ssd // SSD_GROUPS
    heads_per_group = nheads // SSD_GROUPS

    @pl.when(pl.program_id(1) == 0)
    def _():
        state_ref[...] = jnp.zeros_like(state_ref)

    row = lax.broadcasted_iota(jnp.int32, (t, t), 0)
    col = lax.broadcasted_iota(jnp.int32, (t, t), 1)
    causal = col <= row
    tri = jnp.where(causal, 1.0, 0.0).astype(BF16)
    tri_t = jnp.where(row <= col, 1.0, 0.0).astype(BF16)

    a = -jnp.exp(alog_ref[...])
    a_t = -jnp.exp(alogT_ref[...])
    dt = dt_ref[0]
    dt_t = dtT_ref[0]
    acum = sum(_dot(tri, p) for p in _split_bf16(dt * a, 3))
    acum_t = sum(_dot(p, tri_t) for p in _split_bf16(dt_t * a_t, 3))

    eh = lax.broadcasted_iota(jnp.int32, (nheads, d_ssd), 0)
    ej = lax.broadcasted_iota(jnp.int32, (nheads, d_ssd), 1)
    expand_m = jnp.where(ej // hp == eh, 1.0, 0.0).astype(BF16)

    def expand(v):
        return sum(_dot(p, expand_m) for p in _split_bf16(v, 2))

    last = acum[t - 1:t, :]
    dt_x = expand(dt)
    e_x = expand(jnp.exp(acum))
    w_x = expand(jnp.exp(last - acum))

    xs = xs_ref[0].astype(F32)
    xdt = xs * dt_x
    xdt_b = xdt.astype(BF16)
    xw_b = (xdt * w_x).astype(BF16)
    lane = lax.broadcasted_iota(jnp.int32, (1, 2 * hp), 1)
    neg_inf = jnp.float32(-jnp.inf)

    for g in range(SSD_GROUPS):
        b_g = bm_ref[0][:, g * n:(g + 1) * n]
        c_g = cm_ref[0][:, g * n:(g + 1) * n]
        gsl = slice(g * gw, (g + 1) * gw)
        cb = _dot_nt(c_g, b_g)
        prev = state_ref[g]
        y_off = _dot(c_g, prev.astype(BF16)) * e_x[:, gsl]
        new_state = prev * e_x[t - 1:t, gsl] + _dot_tn(b_g, xw_b[:, gsl])
        state_ref[g] = new_state
        for pair in range(heads_per_group // 2):
            lo = g * gw + pair * 2 * hp
            x_pair = xdt_b[:, lo:lo + 2 * hp]
            y_pair = y_off[:, pair * 2 * hp:(pair + 1) * 2 * hp]
            for half in range(2):
                hd = g * heads_per_group + pair * 2 + half
                diff = acum[:, hd:hd + 1] - acum_t[hd:hd + 1, :]
                m = cb * jnp.exp(jnp.where(causal, diff, neg_inf))
                keep = (lane >= half * hp) & (lane < (half + 1) * hp)
                x_half = jnp.where(keep, x_pair, jnp.zeros_like(x_pair))
                y_pair = y_pair + _dot(m.astype(BF16), x_half)
            ybuf_ref[:, lo:lo + 2 * hp] = y_pair

    y = ybuf_ref[...] + dskip_ref[...] * xs
    y = y * zs_ref[0].astype(F32)
    for g in range(SSD_GROUPS):
        gsl = slice(g * gw, (g + 1) * gw)
        y_g = y[:, gsl]
        ms = jnp.mean(y_g * y_g, axis=-1, keepdims=True)
        y_ref[0, :, gsl] = (y_g * lax.rsqrt(ms + EPS) * normw_ref[:, gsl]).astype(BF16)


def _ssd(xs, bm, cm, zs, dt, dt_t, a_log, d_skip, ssd_norm_w):
    bsz, seq, d_ssd = xs.shape
    nheads = dt.shape[-1]
    n_bc = bm.shape[-1]
    t = CHUNK
    grid = (bsz, seq // t)
    row = lambda b, c: (b, c, 0)
    const2 = lambda b, c: (0, 0)
    in_specs = [
        pl.BlockSpec((1, t, d_ssd), row),
        pl.BlockSpec((1, t, n_bc), row),
        pl.BlockSpec((1, t, n_bc), row),
        pl.BlockSpec((1, t, d_ssd), row),
        pl.BlockSpec((1, t, nheads), row),
        pl.BlockSpec((1, nheads, t), lambda b, c: (b, 0, c)),
        pl.BlockSpec((1, nheads), const2),
        pl.BlockSpec((nheads, 1), const2),
        pl.BlockSpec((1, d_ssd), const2),
        pl.BlockSpec((1, d_ssd), const2),
    ]
    kern = functools.partial(_ssd_kernel, nheads=nheads, d_ssd=d_ssd)
    return pl.pallas_call(
        kern, grid=grid, in_specs=in_specs,
        out_specs=pl.BlockSpec((1, t, d_ssd), row),
        out_shape=jax.ShapeDtypeStruct((bsz, seq, d_ssd), BF16),
        scratch_shapes=[pltpu.VMEM((SSD_GROUPS, SSD_STATE, d_ssd // SSD_GROUPS), F32),
                        pltpu.VMEM((t, d_ssd), F32)],
        compiler_params=pltpu.CompilerParams(dimension_semantics=("arbitrary", "arbitrary")),
        name="ssd_scan",
    )(xs, bm, cm, zs, dt, dt_t, a_log.reshape(1, -1), a_log.reshape(-1, 1),
      jnp.repeat(d_skip, SSD_HEAD_DIM).reshape(1, -1), ssd_norm_w.reshape(1, -1))


def _attn_kernel(q_ref, k_ref, v_ref, za_ref, lq1_ref, lk1_ref, lq2_ref, lk2_ref, normw_ref,
                 o_ref, m_ref, l_ref, acc_ref, *, tq):
    qi = pl.program_id(2)
    dk = ATTN_QK_DIM
    lanes = 2 * dk

    q = q_ref[0]
    lane = lax.broadcasted_iota(jnp.int32, (1, lanes), 1)
    zero = jnp.zeros_like(q)
    qq = jnp.concatenate([jnp.where(lane < dk, q, zero), jnp.where(lane >= dk, q, zero)], axis=0)

    m_ref[...] = jnp.full_like(m_ref, -jnp.inf)
    l_ref[...] = jnp.zeros_like(l_ref)
    acc_ref[...] = jnp.zeros_like(acc_ref)

    def step(kv, masked):
        start = pl.multiple_of(kv * tq, tq)
        k = k_ref[0, pl.ds(start, tq), :]
        v = v_ref[0, pl.ds(start, tq), :]
        s = _dot_nt(qq, k)
        if masked:
            r = lax.broadcasted_iota(jnp.int32, (2 * tq, tq), 0)
            c = lax.broadcasted_iota(jnp.int32, (2 * tq, tq), 1)
            r = jnp.where(r >= tq, r - tq, r)
            s = jnp.where(c <= r, s, -jnp.inf)
        m_prev = m_ref[...]
        m_new = jnp.maximum(m_prev, jnp.max(s, axis=-1, keepdims=True))
        alpha = jnp.exp(m_prev - m_new)
        p = jnp.exp(s - m_new[:, :1])
        l_ref[...] = alpha * l_ref[...] + jnp.sum(p, axis=-1, keepdims=True)
        acc_ref[...] = alpha * acc_ref[...] + _dot(p.astype(BF16), v)
        m_ref[...] = m_new

    def body(kv, carry):
        step(kv, False)
        return carry

    lax.fori_loop(0, qi, body, 0)
    step(qi, True)

    lam = (jnp.exp(jnp.sum(lq1_ref[...] * lk1_ref[...], axis=-1, keepdims=True))
           - jnp.exp(jnp.sum(lq2_ref[...] * lk2_ref[...], axis=-1, keepdims=True))
           + LAMBDA_INIT_L0)
    o1 = acc_ref[0:tq, :] / l_ref[0:tq, :]
    o2 = acc_ref[tq:2 * tq, :] / l_ref[tq:2 * tq, :]
    o = o1 - lam * o2
    ms = jnp.mean(o * o, axis=-1, keepdims=True)
    o = o * lax.rsqrt(ms + EPS) * normw_ref[...] * (1.0 - LAMBDA_INIT_L0)
    o_ref[0] = (o * za_ref[0].astype(F32)).astype(BF16)


def _attention(q, k, v, za, lq1, lk1, lq2, lk2, attn_norm_w):
    bsz, seq, d_attn = q.shape
    hw = ATTN_V_DIM
    nh = d_attn // hw
    tq = TQ_ATTN
    grid = (bsz, nh, seq // tq)
    qrow = lambda b, h, i: (b, i, h)
    kvfull = lambda b, h, i: (b, 0, h)
    const2 = lambda b, h, i: (0, 0)
    vec = pl.BlockSpec((1, ATTN_QK_DIM), const2)
    in_specs = [
        pl.BlockSpec((1, tq, hw), qrow),
        pl.BlockSpec((1, seq, hw), kvfull),
        pl.BlockSpec((1, seq, hw), kvfull),
        pl.BlockSpec((1, tq, hw), qrow),
        vec, vec, vec, vec,
        pl.BlockSpec((1, hw), const2),
    ]
    vmem = 2 * 2 * seq * hw * 2 + 3 * 2 * tq * hw * 4 + 6 * 2 * tq * tq * 4
    return pl.pallas_call(
        functools.partial(_attn_kernel, tq=tq), grid=grid, in_specs=in_specs,
        out_specs=pl.BlockSpec((1, tq, hw), qrow),
        out_shape=jax.ShapeDtypeStruct((bsz, seq, d_attn), BF16),
        scratch_shapes=[pltpu.VMEM((2 * tq, hw), F32), pltpu.VMEM((2 * tq, hw), F32),
                        pltpu.VMEM((2 * tq, hw), F32)],
        compiler_params=pltpu.CompilerParams(
            dimension_semantics=("arbitrary", "arbitrary", "arbitrary"),
            vmem_limit_bytes=_vmem_limit(vmem)),
        name="diff_attention",
    )(q, k, v, za, lq1.reshape(1, -1), lk1.reshape(1, -1), lq2.reshape(1, -1), lk2.reshape(1, -1),
      attn_norm_w.reshape(1, -1))


def _outproj_kernel(ys_ref, ya_ref, w_ref, x_ref, gate_ref, g_ref, b_ref, o_ref, *, d_ssd, alpha):
    mixed = _dot(ys_ref[0], w_ref[0:d_ssd, :]) + _dot(ya_ref[0], w_ref[d_ssd:, :])
    xf = alpha * x_ref[0] + gate_ref[0] * mixed
    mu = jnp.mean(xf, axis=-1, keepdims=True)
    xc = xf - mu
    var = jnp.mean(xc * xc, axis=-1, keepdims=True)
    o_ref[0] = xc * lax.rsqrt(var + EPS) * g_ref[...] + b_ref[...]


def _out_projection(y_ssd, y_attn, w_out, x, gate, ln_g, ln_b, alpha):
    bsz, seq, d = x.shape
    d_ssd = y_ssd.shape[-1]
    d_attn = y_attn.shape[-1]
    tm = TM_OUT
    row = lambda b, l: (b, l, 0)
    const2 = lambda b, l: (0, 0)
    in_specs = [
        pl.BlockSpec((1, tm, d_ssd), row),
        pl.BlockSpec((1, tm, d_attn), row),
        pl.BlockSpec((d_ssd + d_attn, d), const2, pipeline_mode=pl.Buffered(1)),
        pl.BlockSpec((1, tm, d), row),
        pl.BlockSpec((1, 1, d), lambda b, l: (b, 0, 0)),
        pl.BlockSpec((1, d), const2),
        pl.BlockSpec((1, d), const2),
    ]
    vmem = (d_ssd + d_attn) * d * 2 + 2 * tm * (d_ssd + d_attn) * 2 + 4 * tm * d * 4 + 4 * tm * d * 4
    return pl.pallas_call(
        functools.partial(_outproj_kernel, d_ssd=d_ssd, alpha=alpha),
        grid=(bsz, seq // tm), in_specs=in_specs,
        out_specs=pl.BlockSpec((1, tm, d), row),
        out_shape=jax.ShapeDtypeStruct((bsz, seq, d), F32),
        compiler_params=pltpu.CompilerParams(
            dimension_semantics=("arbitrary", "arbitrary"), vmem_limit_bytes=_vmem_limit(vmem)),
        name="out_projection",
    )(y_ssd, y_attn, w_out, x, gate, ln_g.reshape(1, -1), ln_b.reshape(1, -1))


def _layer(x, c, w_ada, b_ada, w_in, conv_w, conv_b, dt_bias, a_log, d_skip, ssd_norm_w,
           lambda_q1, lambda_k1, lambda_q2, lambda_k2, attn_norm_w, w_out, ln_g, ln_b, depth):
    bsz, seq, d = x.shape
    nheads = a_log.shape[0]
    d_ssd = nheads * SSD_HEAD_DIM
    d_xbc = conv_w.shape[1]
    d_attn = w_out.shape[0] - d_ssd
    assert d_xbc == d_ssd + 2 * SSD_GROUPS * SSD_STATE
    assert w_in.shape[1] == d_ssd + d_xbc + nheads + 4 * d_attn
    assert seq % TM_INPROJ == 0 and seq % TQ_ATTN == 0 and seq % TM_OUT == 0 and seq % CHUNK == 0

    mod = _modulation(c, w_ada, b_ada)
    shift = mod[:, None, 0:d]
    scale = mod[:, None, d:2 * d]
    gate = mod[:, None, 2 * d:3 * d]

    off_dt = d_ssd + d_xbc
    w_main = jnp.concatenate([w_in[:, :off_dt], w_in[:, off_dt + nheads:]], axis=1).astype(BF16)
    w_dt = w_in[:, off_dt:off_dt + nheads]

    zs, xs, bm, cm, q, k, v, za, dt, dt_t = _in_projection(
        x, scale, shift, w_main, w_dt, conv_w, conv_b, dt_bias, d_ssd=d_ssd, d_xbc=d_xbc, d_attn=d_attn)
    y_ssd = _ssd(xs, bm, cm, zs, dt, dt_t, a_log, d_skip, ssd_norm_w)
    y_attn = _attention(q, k, v, za, lambda_q1, lambda_k1, lambda_q2, lambda_k2, attn_norm_w)
    alpha = (2.0 * depth) ** 0.25
    return _out_projection(y_ssd, y_attn, w_out.astype(BF16), x, gate, ln_g, ln_b, alpha)


def kernel(x, c, w_ada, b_ada, w_in, conv_w, conv_b, dt_bias, a_log, d_skip, ssd_norm_w, lambda_q1,
           lambda_k1, lambda_q2, lambda_k2, attn_norm_w, w_out, ln_g, ln_b):
    depth = w_in.shape[0]
    assert depth == 1, "lambda_init and the layer loop are specialised to a single layer"
    return _layer(x, c, w_ada[0], b_ada[0], w_in[0], conv_w[0], conv_b[0], dt_bias[0], a_log[0],
                  d_skip[0], ssd_norm_w[0], lambda_q1[0], lambda_k1[0], lambda_q2[0], lambda_k2[0],
                  attn_norm_w[0], w_out[0], ln_g[0], ln_b[0], depth)
```

```python
import functools
import math

import jax
import jax.numpy as jnp
from jax import lax
from jax.experimental import pallas as pl
from jax.experimental.pallas import tpu as pltpu

F32 = jnp.float32
BF16 = jnp.bfloat16

SSD_HEAD_DIM = 64
SSD_GROUPS = 2
SSD_STATE = 128
CONV_WIDTH = 4
CHUNK = 128
ATTN_QK_DIM = 64
ATTN_V_DIM = 2 * ATTN_QK_DIM
EPS = 1e-5
LAMBDA_INIT_L0 = 0.8 - 0.6 * math.exp(-0.3 * 0)
LOG2_E = math.log2(math.e)

V7X_LANES = 128
V7X_SUBLANES = 8
V7X_VMEM_BYTES = 64 * 1024 * 1024
V7X_VMEM_USABLE_BYTES = 56 * 1024 * 1024

TM_INPROJ = 512
TQ_ATTN = 512
TM_OUT = 512


def _vmem_limit(estimate_bytes):
    return int(min(V7X_VMEM_USABLE_BYTES, max(32 * 1024 * 1024, estimate_bytes)))


def _silu(x):
    return x * (1.0 / (1.0 + jnp.exp(-x)))


def _softplus(x):
    return jnp.maximum(x, 0.0) + jnp.log(1.0 + jnp.exp(-jnp.abs(x)))


def _split_bf16(a, pieces):
    out = []
    r = a
    for i in range(pieces):
        p = r.astype(BF16)
        out.append(p)
        if i + 1 < pieces:
            r = r - p.astype(F32)
    return out


def _dot(a, b):
    return jnp.dot(a, b, preferred_element_type=F32)


def _dot_nt(a, b):
    return lax.dot_general(a, b, (((1,), (1,)), ((), ())), preferred_element_type=F32)


def _dot_tn(a, b):
    return lax.dot_general(a, b, (((0,), (0,)), ((), ())), preferred_element_type=F32)


def _mod_kernel(c_ref, w_ref, b_ref, o_ref):
    c_hi, c_lo = _split_bf16(c_ref[...], 2)
    w_hi, w_lo = _split_bf16(w_ref[...], 2)
    o_ref[...] = _dot(c_hi, w_hi) + _dot(c_hi, w_lo) + _dot(c_lo, w_hi) + b_ref[...]


def _modulation(c, w_ada, b_ada):
    bsz, d = c.shape
    n = w_ada.shape[1]
    rows = V7X_SUBLANES
    c_pad = jnp.zeros((rows, d), F32).at[:bsz].set(c)
    bn = d
    out = pl.pallas_call(
        _mod_kernel,
        grid=(n // bn,),
        in_specs=[pl.BlockSpec((rows, d), lambda j: (0, 0)),
                  pl.BlockSpec((d, bn), lambda j: (0, j)),
                  pl.BlockSpec((1, bn), lambda j: (0, j))],
        out_specs=pl.BlockSpec((rows, bn), lambda j: (0, j)),
        out_shape=jax.ShapeDtypeStruct((rows, n), F32),
        name="modulation",
    )(c_pad, w_ada, b_ada.reshape(1, n))
    return out[:bsz]


def _inproj_kernel(x_ref, scale_ref, shift_ref, w_ref, wdt_ref, wdtT_ref, convw_ref, convb_ref,
                   dtb_ref, dtbT_ref,
                   zs_ref, xs_ref, bm_ref, cm_ref, q_ref, k_ref, v_ref, za_ref, dt_ref, dtT_ref,
                   carry_ref, ext_ref, *, tm, d_ssd, d_xbc, d_attn, n_bc):
    @pl.when(pl.program_id(1) == 0)
    def _():
        carry_ref[...] = jnp.zeros_like(carry_ref)

    h32 = x_ref[0] * (1.0 + scale_ref[0]) + shift_ref[0]
    h = h32.astype(BF16)

    def proj(lo, width):
        return _dot(h, w_ref[:, lo:lo + width])

    off = 0
    zs_ref[0] = _silu(proj(off, d_ssd)).astype(BF16)
    off += d_ssd

    u = proj(off, d_xbc)
    off += d_xbc
    pad = V7X_SUBLANES
    ext_ref[0:pad, :] = carry_ref[...]
    ext_ref[pad:pad + tm, :] = u
    carry_ref[...] = u[tm - pad:tm, :]
    cw = convw_ref[...]
    acc = convb_ref[...] + cw[CONV_WIDTH - 1:CONV_WIDTH, :] * u
    for tap in range(CONV_WIDTH - 1):
        shift = CONV_WIDTH - 1 - tap
        acc = acc + cw[tap:tap + 1, :] * ext_ref[pad - shift:pad - shift + tm, :]
    xbc = _silu(acc)
    xs_ref[0] = xbc[:, :d_ssd].astype(BF16)
    bm_ref[0] = xbc[:, d_ssd:d_ssd + n_bc].astype(BF16)
    cm_ref[0] = xbc[:, d_ssd + n_bc:d_ssd + 2 * n_bc].astype(BF16)

    q_ref[0] = (proj(off, d_attn) * (ATTN_QK_DIM ** -0.5 * LOG2_E)).astype(BF16)
    off += d_attn
    k_ref[0] = proj(off, d_attn).astype(BF16)
    off += d_attn
    v_ref[0] = proj(off, d_attn).astype(BF16)
    off += d_attn
    za_ref[0] = _silu(proj(off, d_attn)).astype(BF16)

    h_lo = (h32 - h.astype(F32)).astype(BF16)
    w_hi, w_lo = _split_bf16(wdt_ref[...], 2)
    dt_raw = _dot(h, w_hi) + _dot(h, w_lo) + _dot(h_lo, w_hi)
    dt_ref[0] = _softplus(dt_raw + dtb_ref[...])
    wT_hi, wT_lo = _split_bf16(wdtT_ref[...], 2)
    dtT_raw = _dot_nt(wT_hi, h) + _dot_nt(wT_lo, h) + _dot_nt(wT_hi, h_lo)
    dtT_ref[0] = _softplus(dtT_raw + dtbT_ref[...])


def _in_projection(x, scale, shift, w_main, w_dt, conv_w, conv_b, dt_bias, *, d_ssd, d_xbc, d_attn):
    bsz, seq, d = x.shape
    tm = TM_INPROJ
    nh = w_dt.shape[1]
    n_bc = (d_xbc - d_ssd) // 2
    n_main = w_main.shape[1]
    grid = (bsz, seq // tm)

    row = lambda b, l: (b, l, 0)
    const2 = lambda b, l: (0, 0)
    per_b = lambda b, l: (b, 0, 0)
    single = pl.Buffered(1)
    in_specs = [
        pl.BlockSpec((1, tm, d), row),
        pl.BlockSpec((1, 1, d), per_b),
        pl.BlockSpec((1, 1, d), per_b),
        pl.BlockSpec((d, n_main), const2, pipeline_mode=single),
        pl.BlockSpec((d, nh), const2),
        pl.BlockSpec((nh, d), const2),
        pl.BlockSpec((CONV_WIDTH, d_xbc), const2),
        pl.BlockSpec((1, d_xbc), const2),
        pl.BlockSpec((1, nh), const2),
        pl.BlockSpec((nh, 1), const2),
    ]
    out_specs = [
        pl.BlockSpec((1, tm, d_ssd), row),
        pl.BlockSpec((1, tm, d_ssd), row),
        pl.BlockSpec((1, tm, n_bc), row),
        pl.BlockSpec((1, tm, n_bc), row),
        pl.BlockSpec((1, tm, d_attn), row),
        pl.BlockSpec((1, tm, d_attn), row),
        pl.BlockSpec((1, tm, d_attn), row),
        pl.BlockSpec((1, tm, d_attn), row),
        pl.BlockSpec((1, tm, nh), row),
        pl.BlockSpec((1, nh, tm), lambda b, l: (b, 0, l)),
    ]
    out_shape = [
        jax.ShapeDtypeStruct((bsz, seq, d_ssd), BF16),
        jax.ShapeDtypeStruct((bsz, seq, d_ssd), BF16),
        jax.ShapeDtypeStruct((bsz, seq, n_bc), BF16),
        jax.ShapeDtypeStruct((bsz, seq, n_bc), BF16),
        jax.ShapeDtypeStruct((bsz, seq, d_attn), BF16),
        jax.ShapeDtypeStruct((bsz, seq, d_attn), BF16),
        jax.ShapeDtypeStruct((bsz, seq, d_attn), BF16),
        jax.ShapeDtypeStruct((bsz, seq, d_attn), BF16),
        jax.ShapeDtypeStruct((bsz, seq, nh), F32),
        jax.ShapeDtypeStruct((bsz, nh, seq), F32),
    ]
    pad = V7X_SUBLANES
    vmem = (d * n_main * 2 + 2 * tm * d * 4 + 2 * tm * (2 * d_ssd + 2 * n_bc + 4 * d_attn) * 2
            + (tm + 2 * pad) * d_xbc * 4 + 6 * tm * d_xbc * 4)
    kern = functools.partial(_inproj_kernel, tm=tm, d_ssd=d_ssd, d_xbc=d_xbc, d_attn=d_attn, n_bc=n_bc)
    return pl.pallas_call(
        kern, grid=grid, in_specs=in_specs, out_specs=out_specs, out_shape=out_shape,
        scratch_shapes=[pltpu.VMEM((pad, d_xbc), F32), pltpu.VMEM((tm + pad, d_xbc), F32)],
        compiler_params=pltpu.CompilerParams(
            dimension_semantics=("arbitrary", "arbitrary"), vmem_limit_bytes=_vmem_limit(vmem)),
        name="in_projection",
    )(x, scale, shift, w_main, w_dt, w_dt.T, conv_w, conv_b.reshape(1, -1),
      dt_bias.reshape(1, -1), dt_bias.reshape(-1, 1))


def _ssd_kernel(xs_ref, bm_ref, cm_ref, zs_ref, dt_ref, dtT_ref, alog_ref, alogT_ref, dskip_ref,
                normw_ref, y_ref, state_ref, ybuf_ref, *, nheads, d_ssd):
    t = CHUNK
    n = SSD_STATE
    hp = SSD_HEAD_DIM
    gw = d_ssd // SSD_GROUPS
    heads_per_group = nheads // SSD_GROUPS

    @pl.when(pl.program_id(1) == 0)
    def _():
        state_ref[...] = jnp.zeros_like(state_ref)

    row = lax.broadcasted_iota(jnp.int32, (t, t), 0)
    col = lax.broadcasted_iota(jnp.int32, (t, t), 1)
    causal = col <= row
    tri = jnp.where(causal, 1.0, 0.0).astype(BF16)
    tri_t = jnp.where(row <= col, 1.0, 0.0).astype(BF16)

    a = -jnp.exp(alog_ref[...])
    a_t = -jnp.exp(alogT_ref[...])
    dt = dt_ref[0]
    dt_t = dtT_ref[0]
    acum = sum(_dot(tri, p) for p in _split_bf16(dt * a, 3))
    acum_t = sum(_dot(p, tri_t) for p in _split_bf16(dt_t * a_t, 3))

    eh = lax.broadcasted_iota(jnp.int32, (nheads, d_ssd), 0)
    ej = lax.broadcasted_iota(jnp.int32, (nheads, d_ssd), 1)
    expand_m = jnp.where(ej // hp == eh, 1.0, 0.0).astype(BF16)

    def expand(v):
        return sum(_dot(p, expand_m) for p in _split_bf16(v, 2))

    last = acum[t - 1:t, :]
    dt_x = expand(dt)
    e_x = expand(jnp.exp(acum))
    w_x = expand(jnp.exp(last - acum))

    xs = xs_ref[0].astype(F32)
    xdt = xs * dt_x
    xdt_b = xdt.astype(BF16)
    xw_b = (xdt * w_x).astype(BF16)
    lane = lax.broadcasted_iota(jnp.int32, (1, 2 * hp), 1)
    neg_inf = jnp.float32(-jnp.inf)

    for g in range(SSD_GROUPS):
        b_g = bm_ref[0][:, g * n:(g + 1) * n]
        c_g = cm_ref[0][:, g * n:(g + 1) * n]
        gsl = slice(g * gw, (g + 1) * gw)
        cb = _dot_nt(c_g, b_g)
        prev = state_ref[g]
        y_off = _dot(c_g, prev.astype(BF16)) * e_x[:, gsl]
        new_state = prev * e_x[t - 1:t, gsl] + _dot_tn(b_g, xw_b[:, gsl])
        state_ref[g] = new_state
        for pair in range(heads_per_group // 2):
            lo = g * gw + pair * 2 * hp
            x_pair = xdt_b[:, lo:lo + 2 * hp]
            y_pair = y_off[:, pair * 2 * hp:(pair + 1) * 2 * hp]
            for half in range(2):
                hd = g * heads_per_group + pair * 2 + half
                diff = acum[:, hd:hd + 1] - acum_t[hd:hd + 1, :]
                m = cb * jnp.exp(jnp.where(causal, diff, neg_inf))
                keep = (lane >= half * hp) & (lane < (half + 1) * hp)
                x_half = jnp.where(keep, x_pair, jnp.zeros_like(x_pair))
                y_pair = y_pair + _dot(m.astype(BF16), x_half)
            ybuf_ref[:, lo:lo + 2 * hp] = y_pair

    y = ybuf_ref[...] + dskip_ref[...] * xs
    y = y * zs_ref[0].astype(F32)
    for g in range(SSD_GROUPS):
        gsl = slice(g * gw, (g + 1) * gw)
        y_g = y[:, gsl]
        ms = jnp.mean(y_g * y_g, axis=-1, keepdims=True)
        y_ref[0, :, gsl] = (y_g * lax.rsqrt(ms + EPS) * normw_ref[:, gsl]).astype(BF16)


def _ssd(xs, bm, cm, zs, dt, dt_t, a_log, d_skip, ssd_norm_w):
    bsz, seq, d_ssd = xs.shape
    nheads = dt.shape[-1]
    n_bc = bm.shape[-1]
    t = CHUNK
    grid = (bsz, seq // t)
    row = lambda b, c: (b, c, 0)
    const2 = lambda b, c: (0, 0)
    in_specs = [
        pl.BlockSpec((1, t, d_ssd), row),
        pl.BlockSpec((1, t, n_bc), row),
        pl.BlockSpec((1, t, n_bc), row),
        pl.BlockSpec((1, t, d_ssd), row),
        pl.BlockSpec((1, t, nheads), row),
        pl.BlockSpec((1, nheads, t), lambda b, c: (b, 0, c)),
        pl.BlockSpec((1, nheads), const2),
        pl.BlockSpec((nheads, 1), const2),
        pl.BlockSpec((1, d_ssd), const2),
        pl.BlockSpec((1, d_ssd), const2),
    ]
    kern = functools.partial(_ssd_kernel, nheads=nheads, d_ssd=d_ssd)
    return pl.pallas_call(
        kern, grid=grid, in_specs=in_specs,
        out_specs=pl.BlockSpec((1, t, d_ssd), row),
        out_shape=jax.ShapeDtypeStruct((bsz, seq, d_ssd), BF16),
        scratch_shapes=[pltpu.VMEM((SSD_GROUPS, SSD_STATE, d_ssd // SSD_GROUPS), F32),
                        pltpu.VMEM((t, d_ssd), F32)],
        compiler_params=pltpu.CompilerParams(dimension_semantics=("arbitrary", "arbitrary")),
        name="ssd_scan",
    )(xs, bm, cm, zs, dt, dt_t, a_log.reshape(1, -1), a_log.reshape(-1, 1),
      jnp.repeat(d_skip, SSD_HEAD_DIM).reshape(1, -1), ssd_norm_w.reshape(1, -1))


def _attn_kernel(q_ref, k_ref, v_ref, za_ref, lq1_ref, lk1_ref, lq2_ref, lk2_ref, normw_ref,
                 o_ref, qq_ref, m_ref, l_ref, acc_ref, *, tq):
    qi = pl.program_id(2)
    dk = ATTN_QK_DIM
    lanes = 2 * dk

    q = q_ref[0]
    lane = lax.broadcasted_iota(jnp.int32, (1, lanes), 1)
    zero = jnp.zeros_like(q)
    qq_ref[0:tq, :] = jnp.where(lane < dk, q, zero)
    qq_ref[tq:2 * tq, :] = jnp.where(lane >= dk, q, zero)

    m_ref[...] = jnp.full_like(m_ref, -jnp.inf)
    l_ref[...] = jnp.zeros_like(l_ref)
    acc_ref[...] = jnp.zeros_like(acc_ref)

    def scores(kv):
        start = pl.multiple_of(kv * tq, tq)
        return _dot_nt(qq_ref[...], k_ref[0, pl.ds(start, tq), :])

    def update(s, kv):
        start = pl.multiple_of(kv * tq, tq)
        v = v_ref[0, pl.ds(start, tq), :]
        m_prev = m_ref[...]
        m_new = jnp.maximum(m_prev, jnp.max(s, axis=-1, keepdims=True))
        alpha = jnp.exp2(m_prev - m_new)
        p = jnp.exp2(s - m_new[:, :1])
        l_ref[...] = alpha * l_ref[...] + jnp.sum(p, axis=-1, keepdims=True)
        acc_ref[...] = alpha * acc_ref[...] + _dot(p.astype(BF16), v)
        m_ref[...] = m_new

    def body(kv, s):
        s_next = scores(kv + 1)
        update(s, kv)
        return s_next

    s = lax.fori_loop(0, qi, body, scores(0))

    r = lax.broadcasted_iota(jnp.int32, (2 * tq, tq), 0)
    c = lax.broadcasted_iota(jnp.int32, (2 * tq, tq), 1)
    r = jnp.where(r >= tq, r - tq, r)
    update(jnp.where(c <= r, s, -jnp.inf), qi)

    lam = (jnp.exp(jnp.sum(lq1_ref[...] * lk1_ref[...], axis=-1, keepdims=True))
           - jnp.exp(jnp.sum(lq2_ref[...] * lk2_ref[...], axis=-1, keepdims=True))
           + LAMBDA_INIT_L0)
    o1 = acc_ref[0:tq, :] / l_ref[0:tq, :]
    o2 = acc_ref[tq:2 * tq, :] / l_ref[tq:2 * tq, :]
    o = o1 - lam * o2
    ms = jnp.mean(o * o, axis=-1, keepdims=True)
    o = o * lax.rsqrt(ms + EPS) * normw_ref[...] * (1.0 - LAMBDA_INIT_L0)
    o_ref[0] = (o * za_ref[0].astype(F32)).astype(BF16)


def _attention(q, k, v, za, lq1, lk1, lq2, lk2, attn_norm_w):
    bsz, seq, d_attn = q.shape
    hw = ATTN_V_DIM
    nh = d_attn // hw
    tq = TQ_ATTN
    grid = (bsz, nh, seq // tq)
    qrow = lambda b, h, i: (b, i, h)
    kvfull = lambda b, h, i: (b, 0, h)
    const2 = lambda b, h, i: (0, 0)
    vec = pl.BlockSpec((1, ATTN_QK_DIM), const2)
    in_specs = [
        pl.BlockSpec((1, tq, hw), qrow),
        pl.BlockSpec((1, seq, hw), kvfull),
        pl.BlockSpec((1, seq, hw), kvfull),
        pl.BlockSpec((1, tq, hw), qrow),
        vec, vec, vec, vec,
        pl.BlockSpec((1, hw), const2),
    ]
    vmem = 2 * 2 * seq * hw * 2 + 3 * 2 * tq * hw * 4 + 6 * 2 * tq * tq * 4
    return pl.pallas_call(
        functools.partial(_attn_kernel, tq=tq), grid=grid, in_specs=in_specs,
        out_specs=pl.BlockSpec((1, tq, hw), qrow),
        out_shape=jax.ShapeDtypeStruct((bsz, seq, d_attn), BF16),
        scratch_shapes=[pltpu.VMEM((2 * tq, hw), BF16),
                        pltpu.VMEM((2 * tq, hw), F32), pltpu.VMEM((2 * tq, hw), F32),
                        pltpu.VMEM((2 * tq, hw), F32)],
        compiler_params=pltpu.CompilerParams(
            dimension_semantics=("arbitrary", "arbitrary", "arbitrary"),
            vmem_limit_bytes=_vmem_limit(vmem)),
        name="diff_attention",
    )(q, k, v, za, lq1.reshape(1, -1), lk1.reshape(1, -1), lq2.reshape(1, -1), lk2.reshape(1, -1),
      attn_norm_w.reshape(1, -1))


def _outproj_kernel(ys_ref, ya_ref, w_ref, x_ref, gate_ref, g_ref, b_ref, o_ref, *, d_ssd, alpha):
    mixed = _dot(ys_ref[0], w_ref[0:d_ssd, :]) + _dot(ya_ref[0], w_ref[d_ssd:, :])
    xf = alpha * x_ref[0] + gate_ref[0] * mixed
    mu = jnp.mean(xf, axis=-1, keepdims=True)
    xc = xf - mu
    var = jnp.mean(xc * xc, axis=-1, keepdims=True)
    o_ref[0] = xc * lax.rsqrt(var + EPS) * g_ref[...] + b_ref[...]


def _out_projection(y_ssd, y_attn, w_out, x, gate, ln_g, ln_b, alpha):
    bsz, seq, d = x.shape
    d_ssd = y_ssd.shape[-1]
    d_attn = y_attn.shape[-1]
    tm = TM_OUT
    row = lambda b, l: (b, l, 0)
    const2 = lambda b, l: (0, 0)
    in_specs = [
        pl.BlockSpec((1, tm, d_ssd), row),
        pl.BlockSpec((1, tm, d_attn), row),
        pl.BlockSpec((d_ssd + d_attn, d), const2, pipeline_mode=pl.Buffered(1)),
        pl.BlockSpec((1, tm, d), row),
        pl.BlockSpec((1, 1, d), lambda b, l: (b, 0, 0)),
        pl.BlockSpec((1, d), const2),
        pl.BlockSpec((1, d), const2),
    ]
    vmem = (d_ssd + d_attn) * d * 2 + 2 * tm * (d_ssd + d_attn) * 2 + 4 * tm * d * 4 + 4 * tm * d * 4
    return pl.pallas_call(
        functools.partial(_outproj_kernel, d_ssd=d_ssd, alpha=alpha),
        grid=(bsz, seq // tm), in_specs=in_specs,
        out_specs=pl.BlockSpec((1, tm, d), row),
        out_shape=jax.ShapeDtypeStruct((bsz, seq, d), F32),
        compiler_params=pltpu.CompilerParams(
            dimension_semantics=("arbitrary", "arbitrary"), vmem_limit_bytes=_vmem_limit(vmem)),
        name="out_projection",
    )(y_ssd, y_attn, w_out, x, gate, ln_g.reshape(1, -1), ln_b.reshape(1, -1))


def _layer(x, c, w_ada, b_ada, w_in, conv_w, conv_b, dt_bias, a_log, d_skip, ssd_norm_w,
           lambda_q1, lambda_k1, lambda_q2, lambda_k2, attn_norm_w, w_out, ln_g, ln_b, depth):
    bsz, seq, d = x.shape
    nheads = a_log.shape[0]
    d_ssd = nheads * SSD_HEAD_DIM
    d_xbc = conv_w.shape[1]
    d_attn = w_out.shape[0] - d_ssd
    assert d_xbc == d_ssd + 2 * SSD_GROUPS * SSD_STATE
    assert w_in.shape[1] == d_ssd + d_xbc + nheads + 4 * d_attn
    assert seq % TM_INPROJ == 0 and seq % TQ_ATTN == 0 and seq % TM_OUT == 0 and seq % CHUNK == 0

    mod = _modulation(c, w_ada, b_ada)
    shift = mod[:, None, 0:d]
    scale = mod[:, None, d:2 * d]
    gate = mod[:, None, 2 * d:3 * d]

    off_dt = d_ssd + d_xbc
    w_main = jnp.concatenate([w_in[:, :off_dt], w_in[:, off_dt + nheads:]], axis=1).astype(BF16)
    w_dt = w_in[:, off_dt:off_dt + nheads]

    zs, xs, bm, cm, q, k, v, za, dt, dt_t = _in_projection(
        x, scale, shift, w_main, w_dt, conv_w, conv_b, dt_bias, d_ssd=d_ssd, d_xbc=d_xbc, d_attn=d_attn)
    y_ssd = _ssd(xs, bm, cm, zs, dt, dt_t, a_log, d_skip, ssd_norm_w)
    y_attn = _attention(q, k, v, za, lambda_q1, lambda_k1, lambda_q2, lambda_k2, attn_norm_w)
    alpha = (2.0 * depth) ** 0.25
    return _out_projection(y_ssd, y_attn, w_out.astype(BF16), x, gate, ln_g, ln_b, alpha)


def kernel(x, c, w_ada, b_ada, w_in, conv_w, conv_b, dt_bias, a_log, d_skip, ssd_norm_w, lambda_q1,
           lambda_k1, lambda_q2, lambda_k2, attn_norm_w, w_out, ln_g, ln_b):
    depth = w_in.shape[0]
    assert depth == 1, "lambda_init and the layer loop are specialised to a single layer"
    return _layer(x, c, w_ada[0], b_ada[0], w_in[0], conv_w[0], conv_b[0], dt_bias[0], a_log[0],
                  d_skip[0], ssd_norm_w[0], lambda_q1[0], lambda_k1[0], lambda_q2[0], lambda_k2[0],
                  attn_norm_w[0], w_out[0], ln_g[0], ln_b[0], depth)
```

```python
import functools
import math

import jax
import jax.numpy as jnp
from jax import lax
from jax.experimental import pallas as pl
from jax.experimental.pallas import tpu as pltpu

F32 = jnp.float32
BF16 = jnp.bfloat16

SSD_HEAD_DIM = 64
SSD_GROUPS = 2
SSD_STATE = 128
CONV_WIDTH = 4
CHUNK = 128
ATTN_QK_DIM = 64
ATTN_V_DIM = 2 * ATTN_QK_DIM
EPS = 1e-5
LAMBDA_INIT_L0 = 0.8 - 0.6 * math.exp(-0.3 * 0)
LOG2_E = math.log2(math.e)

V7X_LANES = 128
V7X_SUBLANES = 8
V7X_VMEM_BYTES = 64 * 1024 * 1024
V7X_VMEM_USABLE_BYTES = 56 * 1024 * 1024

TM_INPROJ = 512
TQ_ATTN = 512
TM_OUT = 512


def _vmem_limit(estimate_bytes):
    return int(min(V7X_VMEM_USABLE_BYTES, max(32 * 1024 * 1024, estimate_bytes)))


def _silu(x):
    return x * (1.0 / (1.0 + jnp.exp(-x)))


def _softplus(x):
    return jnp.maximum(x, 0.0) + jnp.log(1.0 + jnp.exp(-jnp.abs(x)))


def _split_bf16(a, pieces):
    out = []
    r = a
    for i in range(pieces):
        p = r.astype(BF16)
        out.append(p)
        if i + 1 < pieces:
            r = r - p.astype(F32)
    return out


def _dot(a, b):
    return jnp.dot(a, b, preferred_element_type=F32)


def _dot_nt(a, b):
    return lax.dot_general(a, b, (((1,), (1,)), ((), ())), preferred_element_type=F32)


def _dot_tn(a, b):
    return lax.dot_general(a, b, (((0,), (0,)), ((), ())), preferred_element_type=F32)


def _mod_kernel(c_ref, w_ref, b_ref, o_ref):
    c_hi, c_lo = _split_bf16(c_ref[...], 2)
    w_hi, w_lo = _split_bf16(w_ref[...], 2)
    o_ref[...] = _dot(c_hi, w_hi) + _dot(c_hi, w_lo) + _dot(c_lo, w_hi) + b_ref[...]


def _modulation(c, w_ada, b_ada):
    bsz, d = c.shape
    n = w_ada.shape[1]
    rows = V7X_SUBLANES
    c_pad = jnp.zeros((rows, d), F32).at[:bsz].set(c)
    bn = d
    out = pl.pallas_call(
        _mod_kernel,
        grid=(n // bn,),
        in_specs=[pl.BlockSpec((rows, d), lambda j: (0, 0)),
                  pl.BlockSpec((d, bn), lambda j: (0, j)),
                  pl.BlockSpec((1, bn), lambda j: (0, j))],
        out_specs=pl.BlockSpec((rows, bn), lambda j: (0, j)),
        out_shape=jax.ShapeDtypeStruct((rows, n), F32),
        name="modulation",
    )(c_pad, w_ada, b_ada.reshape(1, n))
    return out[:bsz]


def _inproj_kernel(x_ref, scale_ref, shift_ref, w_ref, wqvT_ref, wdt_ref, wdtT_ref, convw_ref,
                   convb_ref, dtb_ref, dtbT_ref,
                   zs_ref, xs_ref, bm_ref, cm_ref, qT_ref, k_ref, vT_ref, za_ref, dt_ref, dtT_ref,
                   carry_ref, ext_ref, *, tm, d_ssd, d_xbc, d_attn, n_bc):
    @pl.when(pl.program_id(1) == 0)
    def _():
        carry_ref[...] = jnp.zeros_like(carry_ref)

    h32 = x_ref[0] * (1.0 + scale_ref[0]) + shift_ref[0]
    h = h32.astype(BF16)

    def proj(lo, width):
        return _dot(h, w_ref[:, lo:lo + width])

    off = 0
    zs_ref[0] = _silu(proj(off, d_ssd)).astype(BF16)
    off += d_ssd

    u = proj(off, d_xbc)
    off += d_xbc
    pad = V7X_SUBLANES
    ext_ref[0:pad, :] = carry_ref[...]
    ext_ref[pad:pad + tm, :] = u
    carry_ref[...] = u[tm - pad:tm, :]
    cw = convw_ref[...]
    acc = convb_ref[...] + cw[CONV_WIDTH - 1:CONV_WIDTH, :] * u
    for tap in range(CONV_WIDTH - 1):
        shift = CONV_WIDTH - 1 - tap
        acc = acc + cw[tap:tap + 1, :] * ext_ref[pad - shift:pad - shift + tm, :]
    xbc = _silu(acc)
    xs_ref[0] = xbc[:, :d_ssd].astype(BF16)
    bm_ref[0] = xbc[:, d_ssd:d_ssd + n_bc].astype(BF16)
    cm_ref[0] = xbc[:, d_ssd + n_bc:d_ssd + 2 * n_bc].astype(BF16)

    k_ref[0] = proj(off, d_attn).astype(BF16)
    off += d_attn
    za_ref[0] = _silu(proj(off, d_attn)).astype(BF16)

    qvT = _dot_nt(wqvT_ref[...], h)
    nh_attn = d_attn // ATTN_V_DIM
    qT = qvT[:d_attn] * (ATTN_QK_DIM ** -0.5 * LOG2_E)
    qT_ref[0, :, 0] = qT.astype(BF16).reshape(nh_attn, ATTN_V_DIM, tm)
    vT_ref[0, :, 0] = qvT[d_attn:].astype(BF16).reshape(nh_attn, ATTN_V_DIM, tm)

    h_lo = (h32 - h.astype(F32)).astype(BF16)
    w_hi, w_lo = _split_bf16(wdt_ref[...], 2)
    dt_raw = _dot(h, w_hi) + _dot(h, w_lo) + _dot(h_lo, w_hi)
    dt_ref[0] = _softplus(dt_raw + dtb_ref[...])
    wT_hi, wT_lo = _split_bf16(wdtT_ref[...], 2)
    dtT_raw = _dot_nt(wT_hi, h) + _dot_nt(wT_lo, h) + _dot_nt(wT_hi, h_lo)
    dtT_ref[0] = _softplus(dtT_raw + dtbT_ref[...])


def _in_projection(x, scale, shift, w_main, w_qv_t, w_dt, conv_w, conv_b, dt_bias, *, d_ssd, d_xbc,
                   d_attn):
    bsz, seq, d = x.shape
    tm = TM_INPROJ
    nh = w_dt.shape[1]
    nh_attn = d_attn // ATTN_V_DIM
    n_bc = (d_xbc - d_ssd) // 2
    n_main = w_main.shape[1]
    grid = (bsz, seq // tm)

    row = lambda b, l: (b, l, 0)
    head_major = lambda b, l: (b, 0, l, 0, 0)
    const2 = lambda b, l: (0, 0)
    per_b = lambda b, l: (b, 0, 0)
    single = pl.Buffered(1)
    in_specs = [
        pl.BlockSpec((1, tm, d), row),
        pl.BlockSpec((1, 1, d), per_b),
        pl.BlockSpec((1, 1, d), per_b),
        pl.BlockSpec((d, n_main), const2, pipeline_mode=single),
        pl.BlockSpec((2 * d_attn, d), const2, pipeline_mode=single),
        pl.BlockSpec((d, nh), const2),
        pl.BlockSpec((nh, d), const2),
        pl.BlockSpec((CONV_WIDTH, d_xbc), const2),
        pl.BlockSpec((1, d_xbc), const2),
        pl.BlockSpec((1, nh), const2),
        pl.BlockSpec((nh, 1), const2),
    ]
    out_specs = [
        pl.BlockSpec((1, tm, d_ssd), row),
        pl.BlockSpec((1, tm, d_ssd), row),
        pl.BlockSpec((1, tm, n_bc), row),
        pl.BlockSpec((1, tm, n_bc), row),
        pl.BlockSpec((1, nh_attn, 1, ATTN_V_DIM, tm), head_major),
        pl.BlockSpec((1, tm, d_attn), row),
        pl.BlockSpec((1, nh_attn, 1, ATTN_V_DIM, tm), head_major),
        pl.BlockSpec((1, tm, d_attn), row),
        pl.BlockSpec((1, tm, nh), row),
        pl.BlockSpec((1, nh, tm), lambda b, l: (b, 0, l)),
    ]
    out_shape = [
        jax.ShapeDtypeStruct((bsz, seq, d_ssd), BF16),
        jax.ShapeDtypeStruct((bsz, seq, d_ssd), BF16),
        jax.ShapeDtypeStruct((bsz, seq, n_bc), BF16),
        jax.ShapeDtypeStruct((bsz, seq, n_bc), BF16),
        jax.ShapeDtypeStruct((bsz, nh_attn, seq // tm, ATTN_V_DIM, tm), BF16),
        jax.ShapeDtypeStruct((bsz, seq, d_attn), BF16),
        jax.ShapeDtypeStruct((bsz, nh_attn, seq // tm, ATTN_V_DIM, tm), BF16),
        jax.ShapeDtypeStruct((bsz, seq, d_attn), BF16),
        jax.ShapeDtypeStruct((bsz, seq, nh), F32),
        jax.ShapeDtypeStruct((bsz, nh, seq), F32),
    ]
    pad = V7X_SUBLANES
    vmem = (d * (n_main + 2 * d_attn) * 2 + 2 * tm * d * 4
            + 2 * tm * (2 * d_ssd + 2 * n_bc + 4 * d_attn) * 2
            + (tm + 2 * pad) * d_xbc * 4 + 6 * tm * d_xbc * 4)
    kern = functools.partial(_inproj_kernel, tm=tm, d_ssd=d_ssd, d_xbc=d_xbc, d_attn=d_attn, n_bc=n_bc)
    return pl.pallas_call(
        kern, grid=grid, in_specs=in_specs, out_specs=out_specs, out_shape=out_shape,
        scratch_shapes=[pltpu.VMEM((pad, d_xbc), F32), pltpu.VMEM((tm + pad, d_xbc), F32)],
        compiler_params=pltpu.CompilerParams(
            dimension_semantics=("arbitrary", "arbitrary"), vmem_limit_bytes=_vmem_limit(vmem)),
        name="in_projection",
    )(x, scale, shift, w_main, w_qv_t, w_dt, w_dt.T, conv_w, conv_b.reshape(1, -1),
      dt_bias.reshape(1, -1), dt_bias.reshape(-1, 1))


def _ssd_kernel(xs_ref, bm_ref, cm_ref, zs_ref, dt_ref, dtT_ref, alog_ref, alogT_ref, dskip_ref,
                normw_ref, y_ref, state_ref, ybuf_ref, *, nheads, d_ssd):
    t = CHUNK
    n = SSD_STATE
    hp = SSD_HEAD_DIM
    gw = d_ssd // SSD_GROUPS
    heads_per_group = nheads // SSD_GROUPS

    @pl.when(pl.program_id(1) == 0)
    def _():
        state_ref[...] = jnp.zeros_like(state_ref)

    row = lax.broadcasted_iota(jnp.int32, (t, t), 0)
    col = lax.broadcasted_iota(jnp.int32, (t, t), 1)
    causal = col <= row
    tri = jnp.where(causal, 1.0, 0.0).astype(BF16)
    tri_t = jnp.where(row <= col, 1.0, 0.0).astype(BF16)

    a = -jnp.exp(alog_ref[...])
    a_t = -jnp.exp(alogT_ref[...])
    dt = dt_ref[0]
    dt_t = dtT_ref[0]
    acum = sum(_dot(tri, p) for p in _split_bf16(dt * a, 3))
    acum_t = sum(_dot(p, tri_t) for p in _split_bf16(dt_t * a_t, 3))

    eh = lax.broadcasted_iota(jnp.int32, (nheads, d_ssd), 0)
    ej = lax.broadcasted_iota(jnp.int32, (nheads, d_ssd), 1)
    expand_m = jnp.where(ej // hp == eh, 1.0, 0.0).astype(BF16)

    def expand(v):
        return sum(_dot(p, expand_m) for p in _split_bf16(v, 2))

    last = acum[t - 1:t, :]
    dt_x = expand(dt)
    e_x = expand(jnp.exp(acum))
    w_x = expand(jnp.exp(last - acum))

    xs = xs_ref[0].astype(F32)
    xdt = xs * dt_x
    xdt_b = xdt.astype(BF16)
    xw_b = (xdt * w_x).astype(BF16)
    lane = lax.broadcasted_iota(jnp.int32, (1, 2 * hp), 1)
    neg_inf = jnp.float32(-jnp.inf)

    for g in range(SSD_GROUPS):
        b_g = bm_ref[0][:, g * n:(g + 1) * n]
        c_g = cm_ref[0][:, g * n:(g + 1) * n]
        gsl = slice(g * gw, (g + 1) * gw)
        cb = _dot_nt(c_g, b_g)
        prev = state_ref[g]
        y_off = _dot(c_g, prev.astype(BF16)) * e_x[:, gsl]
        new_state = prev * e_x[t - 1:t, gsl] + _dot_tn(b_g, xw_b[:, gsl])
        state_ref[g] = new_state
        for pair in range(heads_per_group // 2):
            lo = g * gw + pair * 2 * hp
            x_pair = xdt_b[:, lo:lo + 2 * hp]
            y_pair = y_off[:, pair * 2 * hp:(pair + 1) * 2 * hp]
            for half in range(2):
                hd = g * heads_per_group + pair * 2 + half
                diff = acum[:, hd:hd + 1] - acum_t[hd:hd + 1, :]
                m = cb * jnp.exp(jnp.where(causal, diff, neg_inf))
                keep = (lane >= half * hp) & (lane < (half + 1) * hp)
                x_half = jnp.where(keep, x_pair, jnp.zeros_like(x_pair))
                y_pair = y_pair + _dot(m.astype(BF16), x_half)
            ybuf_ref[:, lo:lo + 2 * hp] = y_pair

    y = ybuf_ref[...] + dskip_ref[...] * xs
    y = y * zs_ref[0].astype(F32)
    for g in range(SSD_GROUPS):
        gsl = slice(g * gw, (g + 1) * gw)
        y_g = y[:, gsl]
        ms = jnp.mean(y_g * y_g, axis=-1, keepdims=True)
        y_ref[0, :, gsl] = (y_g * lax.rsqrt(ms + EPS) * normw_ref[:, gsl]).astype(BF16)


def _ssd(xs, bm, cm, zs, dt, dt_t, a_log, d_skip, ssd_norm_w):
    bsz, seq, d_ssd = xs.shape
    nheads = dt.shape[-1]
    n_bc = bm.shape[-1]
    t = CHUNK
    grid = (bsz, seq // t)
    row = lambda b, c: (b, c, 0)
    const2 = lambda b, c: (0, 0)
    in_specs = [
        pl.BlockSpec((1, t, d_ssd), row),
        pl.BlockSpec((1, t, n_bc), row),
        pl.BlockSpec((1, t, n_bc), row),
        pl.BlockSpec((1, t, d_ssd), row),
        pl.BlockSpec((1, t, nheads), row),
        pl.BlockSpec((1, nheads, t), lambda b, c: (b, 0, c)),
        pl.BlockSpec((1, nheads), const2),
        pl.BlockSpec((nheads, 1), const2),
        pl.BlockSpec((1, d_ssd), const2),
        pl.BlockSpec((1, d_ssd), const2),
    ]
    kern = functools.partial(_ssd_kernel, nheads=nheads, d_ssd=d_ssd)
    return pl.pallas_call(
        kern, grid=grid, in_specs=in_specs,
        out_specs=pl.BlockSpec((1, t, d_ssd), row),
        out_shape=jax.ShapeDtypeStruct((bsz, seq, d_ssd), BF16),
        scratch_shapes=[pltpu.VMEM((SSD_GROUPS, SSD_STATE, d_ssd // SSD_GROUPS), F32),
                        pltpu.VMEM((t, d_ssd), F32)],
        compiler_params=pltpu.CompilerParams(dimension_semantics=("arbitrary", "arbitrary")),
        name="ssd_scan",
    )(xs, bm, cm, zs, dt, dt_t, a_log.reshape(1, -1), a_log.reshape(-1, 1),
      jnp.repeat(d_skip, SSD_HEAD_DIM).reshape(1, -1), ssd_norm_w.reshape(1, -1))


def _attn_kernel(qT_ref, k_ref, vT_ref, za_ref, lq1_ref, lk1_ref, lq2_ref, lk2_ref, normw_ref,
                 o_ref, qq_ref, s_ref, m_ref, l_ref, acc_ref, *, tq):
    qi = pl.program_id(2)
    dk = ATTN_QK_DIM

    qT = qT_ref[0, 0, 0]
    feat = lax.broadcasted_iota(jnp.int32, (2 * dk, 1), 0)
    zero = jnp.zeros_like(qT)
    qq_ref[:, 0:tq] = jnp.where(feat < dk, qT, zero)
    qq_ref[:, tq:2 * tq] = jnp.where(feat >= dk, qT, zero)

    m_ref[...] = jnp.full_like(m_ref, -jnp.inf)
    l_ref[...] = jnp.zeros_like(l_ref)
    acc_ref[...] = jnp.zeros_like(acc_ref)

    def scores(kv):
        start = pl.multiple_of(kv * tq, tq)
        return _dot(k_ref[0, pl.ds(start, tq), :], qq_ref[...])

    def update(s, kv):
        m_prev = m_ref[...]
        m_new = jnp.maximum(m_prev, jnp.max(s, axis=0, keepdims=True))
        alpha = jnp.exp2(m_prev - m_new)
        p = jnp.exp2(s - m_new)
        l_ref[...] = alpha * l_ref[...] + jnp.sum(p, axis=0, keepdims=True)
        acc_ref[...] = alpha * acc_ref[...] + _dot(vT_ref[0, 0, kv], p.astype(BF16))
        m_ref[...] = m_new

    def update_diagonal(s):
        r = lax.broadcasted_iota(jnp.int32, (tq, 2 * tq), 0)
        c = lax.broadcasted_iota(jnp.int32, (tq, 2 * tq), 1)
        c = jnp.where(c >= tq, c - tq, c)
        update(jnp.where(r <= c, s, -jnp.inf), qi)

    s_ref[0] = scores(0)

    def pair(i, carry):
        kv = 2 * i
        s_ref[1] = scores(kv + 1)
        update(s_ref[0], kv)
        s_ref[0] = scores(kv + 2)
        update(s_ref[1], kv + 1)
        return carry

    lax.fori_loop(0, qi // 2, pair, 0)

    @pl.when(qi % 2 == 1)
    def _():
        s_ref[1] = scores(qi)
        update(s_ref[0], qi - 1)
        update_diagonal(s_ref[1])

    @pl.when(qi % 2 == 0)
    def _():
        update_diagonal(s_ref[0])

    lam = (jnp.exp(jnp.sum(lq1_ref[...] * lk1_ref[...], axis=-1, keepdims=True))
           - jnp.exp(jnp.sum(lq2_ref[...] * lk2_ref[...], axis=-1, keepdims=True))
           + LAMBDA_INIT_L0)
    o1 = acc_ref[:, 0:tq] / l_ref[:, 0:tq]
    o2 = acc_ref[:, tq:2 * tq] / l_ref[:, tq:2 * tq]
    o = (o1 - lam * o2).T
    ms = jnp.mean(o * o, axis=-1, keepdims=True)
    o = o * lax.rsqrt(ms + EPS) * normw_ref[...] * (1.0 - LAMBDA_INIT_L0)
    o_ref[0] = (o * za_ref[0].astype(F32)).astype(BF16)


def _attention(q_t, k, v_t, za, lq1, lk1, lq2, lk2, attn_norm_w):
    bsz, seq, d_attn = k.shape
    hw = ATTN_V_DIM
    nh = d_attn // hw
    tq = TQ_ATTN
    nblk = seq // tq
    assert q_t.shape == (bsz, nh, nblk, hw, tq) and v_t.shape == q_t.shape
    grid = (bsz, nh, nblk)
    qrow = lambda b, h, i: (b, i, h)
    const2 = lambda b, h, i: (0, 0)
    vec = pl.BlockSpec((1, ATTN_QK_DIM), const2)
    in_specs = [
        pl.BlockSpec((1, 1, 1, hw, tq), lambda b, h, i: (b, h, i, 0, 0)),
        pl.BlockSpec((1, seq, hw), lambda b, h, i: (b, 0, h)),
        pl.BlockSpec((1, 1, nblk, hw, tq), lambda b, h, i: (b, h, 0, 0, 0)),
        pl.BlockSpec((1, tq, hw), qrow),
        vec, vec, vec, vec,
        pl.BlockSpec((1, hw), const2),
    ]
    vmem = 2 * 2 * seq * hw * 2 + 3 * 2 * tq * hw * 4 + 6 * 2 * tq * tq * 4
    return pl.pallas_call(
        functools.partial(_attn_kernel, tq=tq), grid=grid, in_specs=in_specs,
        out_specs=pl.BlockSpec((1, tq, hw), qrow),
        out_shape=jax.ShapeDtypeStruct((bsz, seq, d_attn), BF16),
        scratch_shapes=[pltpu.VMEM((hw, 2 * tq), BF16), pltpu.VMEM((2, tq, 2 * tq), F32),
                        pltpu.VMEM((1, 2 * tq), F32), pltpu.VMEM((1, 2 * tq), F32),
                        pltpu.VMEM((hw, 2 * tq), F32)],
        compiler_params=pltpu.CompilerParams(
            dimension_semantics=("arbitrary", "arbitrary", "arbitrary"),
            vmem_limit_bytes=_vmem_limit(vmem)),
        name="diff_attention",
    )(q_t, k, v_t, za, lq1.reshape(1, -1), lk1.reshape(1, -1), lq2.reshape(1, -1),
      lk2.reshape(1, -1), attn_norm_w.reshape(1, -1))


def _outproj_kernel(ys_ref, ya_ref, w_ref, x_ref, gate_ref, g_ref, b_ref, o_ref, *, d_ssd, alpha):
    mixed = _dot(ys_ref[0], w_ref[0:d_ssd, :]) + _dot(ya_ref[0], w_ref[d_ssd:, :])
    xf = alpha * x_ref[0] + gate_ref[0] * mixed
    mu = jnp.mean(xf, axis=-1, keepdims=True)
    xc = xf - mu
    var = jnp.mean(xc * xc, axis=-1, keepdims=True)
    o_ref[0] = xc * lax.rsqrt(var + EPS) * g_ref[...] + b_ref[...]


def _out_projection(y_ssd, y_attn, w_out, x, gate, ln_g, ln_b, alpha):
    bsz, seq, d = x.shape
    d_ssd = y_ssd.shape[-1]
    d_attn = y_attn.shape[-1]
    tm = TM_OUT
    row = lambda b, l: (b, l, 0)
    const2 = lambda b, l: (0, 0)
    in_specs = [
        pl.BlockSpec((1, tm, d_ssd), row),
        pl.BlockSpec((1, tm, d_attn), row),
        pl.BlockSpec((d_ssd + d_attn, d), const2, pipeline_mode=pl.Buffered(1)),
        pl.BlockSpec((1, tm, d), row),
        pl.BlockSpec((1, 1, d), lambda b, l: (b, 0, 0)),
        pl.BlockSpec((1, d), const2),
        pl.BlockSpec((1, d), const2),
    ]
    vmem = (d_ssd + d_attn) * d * 2 + 2 * tm * (d_ssd + d_attn) * 2 + 4 * tm * d * 4 + 4 * tm * d * 4
    return pl.pallas_call(
        functools.partial(_outproj_kernel, d_ssd=d_ssd, alpha=alpha),
        grid=(bsz, seq // tm), in_specs=in_specs,
        out_specs=pl.BlockSpec((1, tm, d), row),
        out_shape=jax.ShapeDtypeStruct((bsz, seq, d), F32),
        compiler_params=pltpu.CompilerParams(
            dimension_semantics=("arbitrary", "arbitrary"), vmem_limit_bytes=_vmem_limit(vmem)),
        name="out_projection",
    )(y_ssd, y_attn, w_out, x, gate, ln_g.reshape(1, -1), ln_b.reshape(1, -1))


def _layer(x, c, w_ada, b_ada, w_in, conv_w, conv_b, dt_bias, a_log, d_skip, ssd_norm_w,
           lambda_q1, lambda_k1, lambda_q2, lambda_k2, attn_norm_w, w_out, ln_g, ln_b, depth):
    bsz, seq, d = x.shape
    nheads = a_log.shape[0]
    d_ssd = nheads * SSD_HEAD_DIM
    d_xbc = conv_w.shape[1]
    d_attn = w_out.shape[0] - d_ssd
    assert d_xbc == d_ssd + 2 * SSD_GROUPS * SSD_STATE
    assert w_in.shape[1] == d_ssd + d_xbc + nheads + 4 * d_attn
    assert seq % TM_INPROJ == 0 and seq % TQ_ATTN == 0 and seq % TM_OUT == 0 and seq % CHUNK == 0
    assert TM_INPROJ == TQ_ATTN, "the in-projection writes q^T / v^T in attention-sized time blocks"

    mod = _modulation(c, w_ada, b_ada)
    shift = mod[:, None, 0:d]
    scale = mod[:, None, d:2 * d]
    gate = mod[:, None, 2 * d:3 * d]

    off_dt = d_ssd + d_xbc
    off_q = off_dt + nheads
    off_k, off_v, off_za = off_q + d_attn, off_q + 2 * d_attn, off_q + 3 * d_attn
    w_main = jnp.concatenate([w_in[:, :off_dt], w_in[:, off_k:off_v], w_in[:, off_za:]],
                             axis=1).astype(BF16)
    w_qv_t = jnp.concatenate([w_in[:, off_q:off_k], w_in[:, off_v:off_za]], axis=1).T.astype(BF16)
    w_dt = w_in[:, off_dt:off_q]

    zs, xs, bm, cm, q_t, k, v_t, za, dt, dt_t = _in_projection(
        x, scale, shift, w_main, w_qv_t, w_dt, conv_w, conv_b, dt_bias,
        d_ssd=d_ssd, d_xbc=d_xbc, d_attn=d_attn)
    y_ssd = _ssd(xs, bm, cm, zs, dt, dt_t, a_log, d_skip, ssd_norm_w)
    y_attn = _attention(q_t, k, v_t, za, lambda_q1, lambda_k1, lambda_q2, lambda_k2, attn_norm_w)
    alpha = (2.0 * depth) ** 0.25
    return _out_projection(y_ssd, y_attn, w_out.astype(BF16), x, gate, ln_g, ln_b, alpha)


def kernel(x, c, w_ada, b_ada, w_in, conv_w, conv_b, dt_bias, a_log, d_skip, ssd_norm_w, lambda_q1,
           lambda_k1, lambda_q2, lambda_k2, attn_norm_w, w_out, ln_g, ln_b):
    depth = w_in.shape[0]
    assert depth == 1, "lambda_init and the layer loop are specialised to a single layer"
    return _layer(x, c, w_ada[0], b_ada[0], w_in[0], conv_w[0], conv_b[0], dt_bias[0], a_log[0],
                  d_skip[0], ssd_norm_w[0], lambda_q1[0], lambda_k1[0], lambda_q2[0], lambda_k2[0],
                  attn_norm_w[0], w_out[0], ln_g[0], ln_b[0], depth)
```

```python
import functools
import math

import jax
import jax.numpy as jnp
from jax import lax
from jax.experimental import pallas as pl
from jax.experimental.pallas import tpu as pltpu

F32 = jnp.float32
BF16 = jnp.bfloat16

SSD_HEAD_DIM = 64
SSD_GROUPS = 2
SSD_STATE = 128
CONV_WIDTH = 4
CHUNK = 128
ATTN_QK_DIM = 64
ATTN_V_DIM = 2 * ATTN_QK_DIM
EPS = 1e-5
LAMBDA_INIT_L0 = 0.8 - 0.6 * math.exp(-0.3 * 0)
LOG2_E = math.log2(math.e)

V7X_LANES = 128
V7X_SUBLANES = 8
BF16_SUBLANE_PACK = 16
V7X_VMEM_BYTES = 64 * 1024 * 1024
V7X_VMEM_USABLE_BYTES = 56 * 1024 * 1024

TM_INPROJ = 512
TK_ATTN = 512
NSUB_ATTN = 2
TM_OUT = 512


def _vmem_limit(estimate_bytes):
    return int(min(V7X_VMEM_USABLE_BYTES, max(32 * 1024 * 1024, estimate_bytes)))


def _silu(x):
    return x * (1.0 / (1.0 + jnp.exp(-x)))


def _softplus(x):
    return jnp.maximum(x, 0.0) + jnp.log(1.0 + jnp.exp(-jnp.abs(x)))


def _split_bf16(a, pieces):
    out = []
    r = a
    for i in range(pieces):
        p = r.astype(BF16)
        out.append(p)
        if i + 1 < pieces:
            r = r - p.astype(F32)
    return out


def _dot(a, b):
    return jnp.dot(a, b, preferred_element_type=F32)


def _dot_nt(a, b):
    return lax.dot_general(a, b, (((1,), (1,)), ((), ())), preferred_element_type=F32)


def _dot_tn(a, b):
    return lax.dot_general(a, b, (((0,), (0,)), ((), ())), preferred_element_type=F32)


def _mod_kernel(c_ref, w_ref, b_ref, o_ref):
    c_hi, c_lo = _split_bf16(c_ref[...], 2)
    w_hi, w_lo = _split_bf16(w_ref[...], 2)
    o_ref[...] = _dot(c_hi, w_hi) + _dot(c_hi, w_lo) + _dot(c_lo, w_hi) + b_ref[...]


def _modulation(c, w_ada, b_ada):
    bsz, d = c.shape
    n = w_ada.shape[1]
    rows = V7X_SUBLANES
    c_pad = jnp.zeros((rows, d), F32).at[:bsz].set(c)
    bn = d
    out = pl.pallas_call(
        _mod_kernel,
        grid=(n // bn,),
        in_specs=[pl.BlockSpec((rows, d), lambda j: (0, 0)),
                  pl.BlockSpec((d, bn), lambda j: (0, j)),
                  pl.BlockSpec((1, bn), lambda j: (0, j))],
        out_specs=pl.BlockSpec((rows, bn), lambda j: (0, j)),
        out_shape=jax.ShapeDtypeStruct((rows, n), F32),
        name="modulation",
    )(c_pad, w_ada, b_ada.reshape(1, n))
    return out[:bsz]


def _inproj_kernel(x_ref, scale_ref, shift_ref, w_ref, wqvT_ref, wdt_ref, wdtT_ref, convw_ref,
                   convb_ref, dtb_ref, dtbT_ref,
                   zs_ref, xs_ref, bm_ref, cm_ref, qT_ref, k_ref, vT_ref, za_ref, dt_ref, dtT_ref,
                   carry_ref, ext_ref, *, tm, d_ssd, d_xbc, d_attn, n_bc):
    @pl.when(pl.program_id(1) == 0)
    def _():
        carry_ref[...] = jnp.zeros_like(carry_ref)

    h32 = x_ref[0] * (1.0 + scale_ref[0]) + shift_ref[0]
    h = h32.astype(BF16)

    def proj(lo, width):
        return _dot(h, w_ref[:, lo:lo + width])

    off = 0
    zs_ref[0] = _silu(proj(off, d_ssd)).astype(BF16)
    off += d_ssd

    u = proj(off, d_xbc)
    off += d_xbc
    pad = V7X_SUBLANES
    ext_ref[0:pad, :] = carry_ref[...]
    ext_ref[pad:pad + tm, :] = u
    carry_ref[...] = u[tm - pad:tm, :]
    cw = convw_ref[...]
    acc = convb_ref[...] + cw[CONV_WIDTH - 1:CONV_WIDTH, :] * u
    for tap in range(CONV_WIDTH - 1):
        shift = CONV_WIDTH - 1 - tap
        acc = acc + cw[tap:tap + 1, :] * ext_ref[pad - shift:pad - shift + tm, :]
    xbc = _silu(acc)
    xs_ref[0] = xbc[:, :d_ssd].astype(BF16)
    bm_ref[0] = xbc[:, d_ssd:d_ssd + n_bc].astype(BF16)
    cm_ref[0] = xbc[:, d_ssd + n_bc:d_ssd + 2 * n_bc].astype(BF16)

    k_ref[0] = proj(off, d_attn).astype(BF16)
    off += d_attn
    za_ref[0] = _silu(proj(off, d_attn)).astype(BF16)

    qvT = _dot_nt(wqvT_ref[...], h)
    nh_attn = d_attn // ATTN_V_DIM
    qT = qvT[:d_attn] * (ATTN_QK_DIM ** -0.5 * LOG2_E)
    qT_ref[0, :, 0] = qT.astype(BF16).reshape(nh_attn, ATTN_V_DIM, tm)
    vT_ref[0, :, 0] = qvT[d_attn:].astype(BF16).reshape(nh_attn, ATTN_V_DIM, tm)

    h_lo = (h32 - h.astype(F32)).astype(BF16)
    w_hi, w_lo = _split_bf16(wdt_ref[...], 2)
    dt_raw = _dot(h, w_hi) + _dot(h, w_lo) + _dot(h_lo, w_hi)
    dt_ref[0] = _softplus(dt_raw + dtb_ref[...])
    wT_hi, wT_lo = _split_bf16(wdtT_ref[...], 2)
    dtT_raw = _dot_nt(wT_hi, h) + _dot_nt(wT_lo, h) + _dot_nt(wT_hi, h_lo)
    dtT_ref[0] = _softplus(dtT_raw + dtbT_ref[...])


def _in_projection(x, scale, shift, w_main, w_qv_t, w_dt, conv_w, conv_b, dt_bias, *, d_ssd, d_xbc,
                   d_attn):
    bsz, seq, d = x.shape
    tm = TM_INPROJ
    nh = w_dt.shape[1]
    nh_attn = d_attn // ATTN_V_DIM
    n_bc = (d_xbc - d_ssd) // 2
    n_main = w_main.shape[1]
    grid = (bsz, seq // tm)

    row = lambda b, l: (b, l, 0)
    head_major = lambda b, l: (b, 0, l, 0, 0)
    const2 = lambda b, l: (0, 0)
    per_b = lambda b, l: (b, 0, 0)
    single = pl.Buffered(1)
    in_specs = [
        pl.BlockSpec((1, tm, d), row),
        pl.BlockSpec((1, 1, d), per_b),
        pl.BlockSpec((1, 1, d), per_b),
        pl.BlockSpec((d, n_main), const2, pipeline_mode=single),
        pl.BlockSpec((2 * d_attn, d), const2, pipeline_mode=single),
        pl.BlockSpec((d, nh), const2),
        pl.BlockSpec((nh, d), const2),
        pl.BlockSpec((CONV_WIDTH, d_xbc), const2),
        pl.BlockSpec((1, d_xbc), const2),
        pl.BlockSpec((1, nh), const2),
        pl.BlockSpec((nh, 1), const2),
    ]
    out_specs = [
        pl.BlockSpec((1, tm, d_ssd), row),
        pl.BlockSpec((1, tm, d_ssd), row),
        pl.BlockSpec((1, tm, n_bc), row),
        pl.BlockSpec((1, tm, n_bc), row),
        pl.BlockSpec((1, nh_attn, 1, ATTN_V_DIM, tm), head_major),
        pl.BlockSpec((1, tm, d_attn), row),
        pl.BlockSpec((1, nh_attn, 1, ATTN_V_DIM, tm), head_major),
        pl.BlockSpec((1, tm, d_attn), row),
        pl.BlockSpec((1, tm, nh), row),
        pl.BlockSpec((1, nh, tm), lambda b, l: (b, 0, l)),
    ]
    out_shape = [
        jax.ShapeDtypeStruct((bsz, seq, d_ssd), BF16),
        jax.ShapeDtypeStruct((bsz, seq, d_ssd), BF16),
        jax.ShapeDtypeStruct((bsz, seq, n_bc), BF16),
        jax.ShapeDtypeStruct((bsz, seq, n_bc), BF16),
        jax.ShapeDtypeStruct((bsz, nh_attn, seq // tm, ATTN_V_DIM, tm), BF16),
        jax.ShapeDtypeStruct((bsz, seq, d_attn), BF16),
        jax.ShapeDtypeStruct((bsz, nh_attn, seq // tm, ATTN_V_DIM, tm), BF16),
        jax.ShapeDtypeStruct((bsz, seq, d_attn), BF16),
        jax.ShapeDtypeStruct((bsz, seq, nh), F32),
        jax.ShapeDtypeStruct((bsz, nh, seq), F32),
    ]
    pad = V7X_SUBLANES
    vmem = (d * (n_main + 2 * d_attn) * 2 + 2 * tm * d * 4
            + 2 * tm * (2 * d_ssd + 2 * n_bc + 4 * d_attn) * 2
            + (tm + 2 * pad) * d_xbc * 4 + 6 * tm * d_xbc * 4)
    kern = functools.partial(_inproj_kernel, tm=tm, d_ssd=d_ssd, d_xbc=d_xbc, d_attn=d_attn, n_bc=n_bc)
    return pl.pallas_call(
        kern, grid=grid, in_specs=in_specs, out_specs=out_specs, out_shape=out_shape,
        scratch_shapes=[pltpu.VMEM((pad, d_xbc), F32), pltpu.VMEM((tm + pad, d_xbc), F32)],
        compiler_params=pltpu.CompilerParams(
            dimension_semantics=("arbitrary", "arbitrary"), vmem_limit_bytes=_vmem_limit(vmem)),
        name="in_projection",
    )(x, scale, shift, w_main, w_qv_t, w_dt, w_dt.T, conv_w, conv_b.reshape(1, -1),
      dt_bias.reshape(1, -1), dt_bias.reshape(-1, 1))


def _ssd_kernel(xs_ref, bm_ref, cm_ref, zs_ref, dt_ref, dtT_ref, alog_ref, alogT_ref, dskip_ref,
                normw_ref, y_ref, state_ref, ybuf_ref, *, nheads, d_ssd):
    t = CHUNK
    n = SSD_STATE
    hp = SSD_HEAD_DIM
    gw = d_ssd // SSD_GROUPS
    heads_per_group = nheads // SSD_GROUPS

    @pl.when(pl.program_id(1) == 0)
    def _():
        state_ref[...] = jnp.zeros_like(state_ref)

    row = lax.broadcasted_iota(jnp.int32, (t, t), 0)
    col = lax.broadcasted_iota(jnp.int32, (t, t), 1)
    causal = col <= row
    tri = jnp.where(causal, 1.0, 0.0).astype(BF16)
    tri_t = jnp.where(row <= col, 1.0, 0.0).astype(BF16)

    a = -jnp.exp(alog_ref[...])
    a_t = -jnp.exp(alogT_ref[...])
    dt = dt_ref[0]
    dt_t = dtT_ref[0]
    acum = sum(_dot(tri, p) for p in _split_bf16(dt * a, 3))
    acum_t = sum(_dot(p, tri_t) for p in _split_bf16(dt_t * a_t, 3))

    eh = lax.broadcasted_iota(jnp.int32, (nheads, d_ssd), 0)
    ej = lax.broadcasted_iota(jnp.int32, (nheads, d_ssd), 1)
    expand_m = jnp.where(ej // hp == eh, 1.0, 0.0).astype(BF16)

    def expand(v):
        return sum(_dot(p, expand_m) for p in _split_bf16(v, 2))

    last = acum[t - 1:t, :]
    dt_x = expand(dt)
    e_x = expand(jnp.exp(acum))
    w_x = expand(jnp.exp(last - acum))

    xs = xs_ref[0].astype(F32)
    xdt = xs * dt_x
    xdt_b = xdt.astype(BF16)
    xw_b = (xdt * w_x).astype(BF16)
    lane = lax.broadcasted_iota(jnp.int32, (1, 2 * hp), 1)
    neg_inf = jnp.float32(-jnp.inf)

    for g in range(SSD_GROUPS):
        b_g = bm_ref[0][:, g * n:(g + 1) * n]
        c_g = cm_ref[0][:, g * n:(g + 1) * n]
        gsl = slice(g * gw, (g + 1) * gw)
        cb = _dot_nt(c_g, b_g)
        prev = state_ref[g]
        y_off = _dot(c_g, prev.astype(BF16)) * e_x[:, gsl]
        new_state = prev * e_x[t - 1:t, gsl] + _dot_tn(b_g, xw_b[:, gsl])
        state_ref[g] = new_state
        for pair in range(heads_per_group // 2):
            lo = g * gw + pair * 2 * hp
            x_pair = xdt_b[:, lo:lo + 2 * hp]
            y_pair = y_off[:, pair * 2 * hp:(pair + 1) * 2 * hp]
            for half in range(2):
                hd = g * heads_per_group + pair * 2 + half
                diff = acum[:, hd:hd + 1] - acum_t[hd:hd + 1, :]
                m = cb * jnp.exp(jnp.where(causal, diff, neg_inf))
                keep = (lane >= half * hp) & (lane < (half + 1) * hp)
                x_half = jnp.where(keep, x_pair, jnp.zeros_like(x_pair))
                y_pair = y_pair + _dot(m.astype(BF16), x_half)
            ybuf_ref[:, lo:lo + 2 * hp] = y_pair

    y = ybuf_ref[...] + dskip_ref[...] * xs
    y = y * zs_ref[0].astype(F32)
    for g in range(SSD_GROUPS):
        gsl = slice(g * gw, (g + 1) * gw)
        y_g = y[:, gsl]
        ms = jnp.mean(y_g * y_g, axis=-1, keepdims=True)
        y_ref[0, :, gsl] = (y_g * lax.rsqrt(ms + EPS) * normw_ref[:, gsl]).astype(BF16)


def _ssd(xs, bm, cm, zs, dt, dt_t, a_log, d_skip, ssd_norm_w):
    bsz, seq, d_ssd = xs.shape
    nheads = dt.shape[-1]
    n_bc = bm.shape[-1]
    t = CHUNK
    grid = (bsz, seq // t)
    row = lambda b, c: (b, c, 0)
    const2 = lambda b, c: (0, 0)
    in_specs = [
        pl.BlockSpec((1, t, d_ssd), row),
        pl.BlockSpec((1, t, n_bc), row),
        pl.BlockSpec((1, t, n_bc), row),
        pl.BlockSpec((1, t, d_ssd), row),
        pl.BlockSpec((1, t, nheads), row),
        pl.BlockSpec((1, nheads, t), lambda b, c: (b, 0, c)),
        pl.BlockSpec((1, nheads), const2),
        pl.BlockSpec((nheads, 1), const2),
        pl.BlockSpec((1, d_ssd), const2),
        pl.BlockSpec((1, d_ssd), const2),
    ]
    kern = functools.partial(_ssd_kernel, nheads=nheads, d_ssd=d_ssd)
    return pl.pallas_call(
        kern, grid=grid, in_specs=in_specs,
        out_specs=pl.BlockSpec((1, t, d_ssd), row),
        out_shape=jax.ShapeDtypeStruct((bsz, seq, d_ssd), BF16),
        scratch_shapes=[pltpu.VMEM((SSD_GROUPS, SSD_STATE, d_ssd // SSD_GROUPS), F32),
                        pltpu.VMEM((t, d_ssd), F32)],
        compiler_params=pltpu.CompilerParams(dimension_semantics=("arbitrary", "arbitrary")),
        name="ssd_scan",
    )(xs, bm, cm, zs, dt, dt_t, a_log.reshape(1, -1), a_log.reshape(-1, 1),
      jnp.repeat(d_skip, SSD_HEAD_DIM).reshape(1, -1), ssd_norm_w.reshape(1, -1))


def _attn_kernel(qT_ref, k_ref, vT_ref, za_ref, lq1_ref, lk1_ref, lq2_ref, lk2_ref, normw_ref,
                 o_ref, qq_ref, s_ref, m_ref, acc_ref, *, tk, nsub):
    qi = pl.program_id(2)
    dk = ATTN_QK_DIM
    dv = 2 * dk
    sw = 2 * tk
    feat = lax.broadcasted_iota(jnp.int32, (2 * dk, 1), 0)
    for sub in range(nsub):
        qT = qT_ref[0, 0, sub]
        zero = jnp.zeros_like(qT)
        qq_ref[:, sub * sw:sub * sw + tk] = jnp.where(feat < dk, qT, zero)
        qq_ref[:, sub * sw + tk:(sub + 1) * sw] = jnp.where(feat >= dk, qT, zero)

    m_ref[...] = jnp.full_like(m_ref, -jnp.inf)
    acc_ref[...] = jnp.zeros_like(acc_ref)
    ones_rows = jnp.ones((BF16_SUBLANE_PACK, tk), BF16)

    def scores(kv, lo=0):
        start = pl.multiple_of(kv * tk, tk)
        return _dot(k_ref[0, pl.ds(start, tk), :], qq_ref[:, lo:])

    def update(s, kv, lo=0):
        m_prev = m_ref[:, lo:]
        m_new = jnp.maximum(m_prev, jnp.max(s, axis=0, keepdims=True))
        alpha = jnp.exp2(m_prev - m_new)
        p = jnp.exp2(s - m_new).astype(BF16)
        v_ext = jnp.concatenate([vT_ref[0, 0, kv], ones_rows], axis=0)
        acc_ref[:, lo:] = alpha * acc_ref[:, lo:] + _dot(v_ext, p)
        m_ref[:, lo:] = m_new

    def causal(s):
        r = lax.broadcasted_iota(jnp.int32, s.shape, 0)
        c = lax.broadcasted_iota(jnp.int32, s.shape, 1)
        qpos = jnp.where(c >= tk, c - tk, c)
        return jnp.where((c >= sw) | (r <= qpos), s, -jnp.inf)

    s_ref[0] = scores(0)

    def pair(i, carry):
        kv = 2 * i
        s_ref[1] = scores(kv + 1)
        update(s_ref[0], kv)
        s_ref[0] = scores(kv + 2)
        update(s_ref[1], kv + 1)
        return carry

    lax.fori_loop(0, (nsub // 2) * qi, pair, 0)

    kv0 = nsub * qi
    for d in range(nsub):
        lo = d * sw
        slot = d % 2
        if d + 1 < nsub:
            s_ref[1 - slot, :, lo + sw:] = scores(kv0 + d + 1, lo + sw)
        update(causal(s_ref[slot, :, lo:]), kv0 + d, lo)

    lam = (jnp.exp(jnp.sum(lq1_ref[...] * lk1_ref[...], axis=-1, keepdims=True))
           - jnp.exp(jnp.sum(lq2_ref[...] * lk2_ref[...], axis=-1, keepdims=True))
           + LAMBDA_INIT_L0)
    outs = []
    for sub in range(nsub):
        c1 = sub * sw
        c2 = c1 + tk
        o1 = acc_ref[0:dv, c1:c1 + tk] / acc_ref[dv:dv + 1, c1:c1 + tk]
        o2 = acc_ref[0:dv, c2:c2 + tk] / acc_ref[dv:dv + 1, c2:c2 + tk]
        outs.append(o1 - lam * o2)
    o = jnp.concatenate(outs, axis=1).T
    ms = jnp.mean(o * o, axis=-1, keepdims=True)
    o = o * lax.rsqrt(ms + EPS) * normw_ref[...] * (1.0 - LAMBDA_INIT_L0)
    o_ref[0] = (o * za_ref[0].astype(F32)).astype(BF16)


def _attention(q_t, k, v_t, za, lq1, lk1, lq2, lk2, attn_norm_w):
    bsz, seq, d_attn = k.shape
    hw = ATTN_V_DIM
    nh = d_attn // hw
    tk = TK_ATTN
    nsub = NSUB_ATTN
    tq = nsub * tk
    nblk = seq // tk
    assert nsub % 2 == 0 and seq % tq == 0
    assert q_t.shape == (bsz, nh, nblk, hw, tk) and v_t.shape == q_t.shape
    grid = (bsz, nh, seq // tq)
    qrow = lambda b, h, i: (b, i, h)
    const2 = lambda b, h, i: (0, 0)
    vec = pl.BlockSpec((1, ATTN_QK_DIM), const2)
    in_specs = [
        pl.BlockSpec((1, 1, nsub, hw, tk), lambda b, h, i: (b, h, i, 0, 0)),
        pl.BlockSpec((1, seq, hw), lambda b, h, i: (b, 0, h)),
        pl.BlockSpec((1, 1, nblk, hw, tk), lambda b, h, i: (b, h, 0, 0, 0)),
        pl.BlockSpec((1, tq, hw), qrow),
        vec, vec, vec, vec,
        pl.BlockSpec((1, hw), const2),
    ]
    width = 2 * tq
    vmem = (2 * 2 * seq * hw * 2 + 2 * tk * width * 4 + (2 * hw + BF16_SUBLANE_PACK) * width * 4
            + 4 * tk * width * 4)
    return pl.pallas_call(
        functools.partial(_attn_kernel, tk=tk, nsub=nsub), grid=grid, in_specs=in_specs,
        out_specs=pl.BlockSpec((1, tq, hw), qrow),
        out_shape=jax.ShapeDtypeStruct((bsz, seq, d_attn), BF16),
        scratch_shapes=[pltpu.VMEM((hw, width), BF16), pltpu.VMEM((2, tk, width), F32),
                        pltpu.VMEM((1, width), F32),
                        pltpu.VMEM((hw + BF16_SUBLANE_PACK, width), F32)],
        compiler_params=pltpu.CompilerParams(
            dimension_semantics=("arbitrary", "arbitrary", "arbitrary"),
            vmem_limit_bytes=_vmem_limit(vmem)),
        name="diff_attention",
    )(q_t, k, v_t, za, lq1.reshape(1, -1), lk1.reshape(1, -1), lq2.reshape(1, -1),
      lk2.reshape(1, -1), attn_norm_w.reshape(1, -1))


def _outproj_kernel(ys_ref, ya_ref, w_ref, x_ref, gate_ref, g_ref, b_ref, o_ref, *, d_ssd, alpha):
    mixed = _dot(ys_ref[0], w_ref[0:d_ssd, :]) + _dot(ya_ref[0], w_ref[d_ssd:, :])
    xf = alpha * x_ref[0] + gate_ref[0] * mixed
    mu = jnp.mean(xf, axis=-1, keepdims=True)
    xc = xf - mu
    var = jnp.mean(xc * xc, axis=-1, keepdims=True)
    o_ref[0] = xc * lax.rsqrt(var + EPS) * g_ref[...] + b_ref[...]


def _out_projection(y_ssd, y_attn, w_out, x, gate, ln_g, ln_b, alpha):
    bsz, seq, d = x.shape
    d_ssd = y_ssd.shape[-1]
    d_attn = y_attn.shape[-1]
    tm = TM_OUT
    row = lambda b, l: (b, l, 0)
    const2 = lambda b, l: (0, 0)
    in_specs = [
        pl.BlockSpec((1, tm, d_ssd), row),
        pl.BlockSpec((1, tm, d_attn), row),
        pl.BlockSpec((d_ssd + d_attn, d), const2, pipeline_mode=pl.Buffered(1)),
        pl.BlockSpec((1, tm, d), row),
        pl.BlockSpec((1, 1, d), lambda b, l: (b, 0, 0)),
        pl.BlockSpec((1, d), const2),
        pl.BlockSpec((1, d), const2),
    ]
    vmem = (d_ssd + d_attn) * d * 2 + 2 * tm * (d_ssd + d_attn) * 2 + 4 * tm * d * 4 + 4 * tm * d * 4
    return pl.pallas_call(
        functools.partial(_outproj_kernel, d_ssd=d_ssd, alpha=alpha),
        grid=(bsz, seq // tm), in_specs=in_specs,
        out_specs=pl.BlockSpec((1, tm, d), row),
        out_shape=jax.ShapeDtypeStruct((bsz, seq, d), F32),
        compiler_params=pltpu.CompilerParams(
            dimension_semantics=("arbitrary", "arbitrary"), vmem_limit_bytes=_vmem_limit(vmem)),
        name="out_projection",
    )(y_ssd, y_attn, w_out, x, gate, ln_g.reshape(1, -1), ln_b.reshape(1, -1))


def _layer(x, c, w_ada, b_ada, w_in, conv_w, conv_b, dt_bias, a_log, d_skip, ssd_norm_w,
           lambda_q1, lambda_k1, lambda_q2, lambda_k2, attn_norm_w, w_out, ln_g, ln_b, depth):
    bsz, seq, d = x.shape
    nheads = a_log.shape[0]
    d_ssd = nheads * SSD_HEAD_DIM
    d_xbc = conv_w.shape[1]
    d_attn = w_out.shape[0] - d_ssd
    assert d_xbc == d_ssd + 2 * SSD_GROUPS * SSD_STATE
    assert w_in.shape[1] == d_ssd + d_xbc + nheads + 4 * d_attn
    assert seq % TM_INPROJ == 0 and seq % TM_OUT == 0 and seq % CHUNK == 0
    assert TM_INPROJ == TK_ATTN, "the in-projection writes q^T / v^T in attention-sized time blocks"

    mod = _modulation(c, w_ada, b_ada)
    shift = mod[:, None, 0:d]
    scale = mod[:, None, d:2 * d]
    gate = mod[:, None, 2 * d:3 * d]

    off_dt = d_ssd + d_xbc
    off_q = off_dt + nheads
    off_k, off_v, off_za = off_q + d_attn, off_q + 2 * d_attn, off_q + 3 * d_attn
    w_main = jnp.concatenate([w_in[:, :off_dt], w_in[:, off_k:off_v], w_in[:, off_za:]],
                             axis=1).astype(BF16)
    w_qv_t = jnp.concatenate([w_in[:, off_q:off_k], w_in[:, off_v:off_za]], axis=1).T.astype(BF16)
    w_dt = w_in[:, off_dt:off_q]

    zs, xs, bm, cm, q_t, k, v_t, za, dt, dt_t = _in_projection(
        x, scale, shift, w_main, w_qv_t, w_dt, conv_w, conv_b, dt_bias,
        d_ssd=d_ssd, d_xbc=d_xbc, d_attn=d_attn)
    y_ssd = _ssd(xs, bm, cm, zs, dt, dt_t, a_log, d_skip, ssd_norm_w)
    y_attn = _attention(q_t, k, v_t, za, lambda_q1, lambda_k1, lambda_q2, lambda_k2, attn_norm_w)
    alpha = (2.0 * depth) ** 0.25
    return _out_projection(y_ssd, y_attn, w_out.astype(BF16), x, gate, ln_g, ln_b, alpha)


def kernel(x, c, w_ada, b_ada, w_in, conv_w, conv_b, dt_bias, a_log, d_skip, ssd_norm_w, lambda_q1,
           lambda_k1, lambda_q2, lambda_k2, attn_norm_w, w_out, ln_g, ln_b):
    depth = w_in.shape[0]
    assert depth == 1, "lambda_init and the layer loop are specialised to a single layer"
    return _layer(x, c, w_ada[0], b_ada[0], w_in[0], conv_w[0], conv_b[0], dt_bias[0], a_log[0],
                  d_skip[0], ssd_norm_w[0], lambda_q1[0], lambda_k1[0], lambda_q2[0], lambda_k2[0],
                  attn_norm_w[0], w_out[0], ln_g[0], ln_b[0], depth)
```

```python
import functools
import math

import jax
import jax.numpy as jnp
from jax import lax
from jax.experimental import pallas as pl
from jax.experimental.pallas import tpu as pltpu

F32 = jnp.float32
BF16 = jnp.bfloat16

SSD_HEAD_DIM = 64
SSD_GROUPS = 2
SSD_STATE = 128
CONV_WIDTH = 4
CHUNK = 128
ATTN_QK_DIM = 64
ATTN_V_DIM = 2 * ATTN_QK_DIM
EPS = 1e-5
LAMBDA_INIT_L0 = 0.8 - 0.6 * math.exp(-0.3 * 0)
LOG2_E = math.log2(math.e)

V7X_LANES = 128
V7X_SUBLANES = 8
BF16_SUBLANE_PACK = 16
V7X_VMEM_BYTES = 64 * 1024 * 1024
V7X_VMEM_USABLE_BYTES = 56 * 1024 * 1024

TM_INPROJ = 512
TK_ATTN = 512
NSUB_ATTN = 2
TM_OUT = 512


def _vmem_limit(estimate_bytes):
    return int(min(V7X_VMEM_USABLE_BYTES, max(32 * 1024 * 1024, estimate_bytes)))


def _silu(x):
    return x * (1.0 / (1.0 + jnp.exp(-x)))


def _softplus(x):
    return jnp.maximum(x, 0.0) + jnp.log(1.0 + jnp.exp(-jnp.abs(x)))


def _split_bf16(a, pieces):
    out = []
    r = a
    for i in range(pieces):
        p = r.astype(BF16)
        out.append(p)
        if i + 1 < pieces:
            r = r - p.astype(F32)
    return out


def _dot(a, b):
    return jnp.dot(a, b, preferred_element_type=F32)


def _dot_nt(a, b):
    return lax.dot_general(a, b, (((1,), (1,)), ((), ())), preferred_element_type=F32)


def _dot_tn(a, b):
    return lax.dot_general(a, b, (((0,), (0,)), ((), ())), preferred_element_type=F32)


def _mod_kernel(c_ref, w_ref, b_ref, o_ref):
    c_hi, c_lo = _split_bf16(c_ref[...], 2)
    w_hi, w_lo = _split_bf16(w_ref[...], 2)
    o_ref[...] = _dot(c_hi, w_hi) + _dot(c_hi, w_lo) + _dot(c_lo, w_hi) + b_ref[...]


def _modulation(c, w_ada, b_ada):
    bsz, d = c.shape
    n = w_ada.shape[1]
    rows = V7X_SUBLANES
    c_pad = jnp.zeros((rows, d), F32).at[:bsz].set(c)
    bn = d
    out = pl.pallas_call(
        _mod_kernel,
        grid=(n // bn,),
        in_specs=[pl.BlockSpec((rows, d), lambda j: (0, 0)),
                  pl.BlockSpec((d, bn), lambda j: (0, j)),
                  pl.BlockSpec((1, bn), lambda j: (0, j))],
        out_specs=pl.BlockSpec((rows, bn), lambda j: (0, j)),
        out_shape=jax.ShapeDtypeStruct((rows, n), F32),
        name="modulation",
    )(c_pad, w_ada, b_ada.reshape(1, n))
    return out[:bsz]


def _inproj_kernel(x_ref, scale_ref, shift_ref, w_ref, wqvT_ref, wdt_ref, wdtT_ref, convw_ref,
                   convb_ref, dtb_ref, dtbT_ref,
                   zs_ref, xs_ref, bm_ref, cm_ref, qT_ref, k_ref, vT_ref, za_ref, dt_ref, dtT_ref,
                   carry_ref, ext_ref, *, tm, d_ssd, d_xbc, d_attn, n_bc):
    @pl.when(pl.program_id(1) == 0)
    def _():
        carry_ref[...] = jnp.zeros_like(carry_ref)

    h32 = x_ref[0] * (1.0 + scale_ref[0]) + shift_ref[0]
    h = h32.astype(BF16)

    def proj(lo, width):
        return _dot(h, w_ref[:, lo:lo + width])

    off = 0
    zs_ref[0] = _silu(proj(off, d_ssd)).astype(BF16)
    off += d_ssd

    u = proj(off, d_xbc)
    off += d_xbc
    pad = V7X_SUBLANES
    ext_ref[0:pad, :] = carry_ref[...]
    ext_ref[pad:pad + tm, :] = u
    carry_ref[...] = u[tm - pad:tm, :]
    cw = convw_ref[...]
    acc = convb_ref[...] + cw[CONV_WIDTH - 1:CONV_WIDTH, :] * u
    for tap in range(CONV_WIDTH - 1):
        shift = CONV_WIDTH - 1 - tap
        acc = acc + cw[tap:tap + 1, :] * ext_ref[pad - shift:pad - shift + tm, :]
    xbc = _silu(acc)
    xs_ref[0] = xbc[:, :d_ssd].astype(BF16)
    bm_ref[0] = xbc[:, d_ssd:d_ssd + n_bc].astype(BF16)
    cm_ref[0] = xbc[:, d_ssd + n_bc:d_ssd + 2 * n_bc].astype(BF16)

    k_ref[0] = proj(off, d_attn).astype(BF16)
    off += d_attn
    za_ref[0] = _silu(proj(off, d_attn)).astype(BF16)

    qvT = _dot_nt(wqvT_ref[...], h)
    nh_attn = d_attn // ATTN_V_DIM
    qT = qvT[:d_attn] * (ATTN_QK_DIM ** -0.5 * LOG2_E)
    qT_ref[0, :, 0] = qT.astype(BF16).reshape(nh_attn, ATTN_V_DIM, tm)
    vT_ref[0, :, 0] = qvT[d_attn:].astype(BF16).reshape(nh_attn, ATTN_V_DIM, tm)

    h_lo = (h32 - h.astype(F32)).astype(BF16)
    w_hi, w_lo = _split_bf16(wdt_ref[...], 2)
    dt_raw = _dot(h, w_hi) + _dot(h, w_lo) + _dot(h_lo, w_hi)
    dt_ref[0] = _softplus(dt_raw + dtb_ref[...])
    wT_hi, wT_lo = _split_bf16(wdtT_ref[...], 2)
    dtT_raw = _dot_nt(wT_hi, h) + _dot_nt(wT_lo, h) + _dot_nt(wT_hi, h_lo)
    dtT_ref[0] = _softplus(dtT_raw + dtbT_ref[...])


def _in_projection(x, scale, shift, w_main, w_qv_t, w_dt, conv_w, conv_b, dt_bias, *, d_ssd, d_xbc,
                   d_attn):
    bsz, seq, d = x.shape
    tm = TM_INPROJ
    nh = w_dt.shape[1]
    nh_attn = d_attn // ATTN_V_DIM
    n_bc = (d_xbc - d_ssd) // 2
    n_main = w_main.shape[1]
    grid = (bsz, seq // tm)

    row = lambda b, l: (b, l, 0)
    head_major = lambda b, l: (b, 0, l, 0, 0)
    const2 = lambda b, l: (0, 0)
    per_b = lambda b, l: (b, 0, 0)
    single = pl.Buffered(1)
    in_specs = [
        pl.BlockSpec((1, tm, d), row),
        pl.BlockSpec((1, 1, d), per_b),
        pl.BlockSpec((1, 1, d), per_b),
        pl.BlockSpec((d, n_main), const2, pipeline_mode=single),
        pl.BlockSpec((2 * d_attn, d), const2, pipeline_mode=single),
        pl.BlockSpec((d, nh), const2),
        pl.BlockSpec((nh, d), const2),
        pl.BlockSpec((CONV_WIDTH, d_xbc), const2),
        pl.BlockSpec((1, d_xbc), const2),
        pl.BlockSpec((1, nh), const2),
        pl.BlockSpec((nh, 1), const2),
    ]
    out_specs = [
        pl.BlockSpec((1, tm, d_ssd), row),
        pl.BlockSpec((1, tm, d_ssd), row),
        pl.BlockSpec((1, tm, n_bc), row),
        pl.BlockSpec((1, tm, n_bc), row),
        pl.BlockSpec((1, nh_attn, 1, ATTN_V_DIM, tm), head_major),
        pl.BlockSpec((1, tm, d_attn), row),
        pl.BlockSpec((1, nh_attn, 1, ATTN_V_DIM, tm), head_major),
        pl.BlockSpec((1, tm, d_attn), row),
        pl.BlockSpec((1, tm, nh), row),
        pl.BlockSpec((1, nh, tm), lambda b, l: (b, 0, l)),
    ]
    out_shape = [
        jax.ShapeDtypeStruct((bsz, seq, d_ssd), BF16),
        jax.ShapeDtypeStruct((bsz, seq, d_ssd), BF16),
        jax.ShapeDtypeStruct((bsz, seq, n_bc), BF16),
        jax.ShapeDtypeStruct((bsz, seq, n_bc), BF16),
        jax.ShapeDtypeStruct((bsz, nh_attn, seq // tm, ATTN_V_DIM, tm), BF16),
        jax.ShapeDtypeStruct((bsz, seq, d_attn), BF16),
        jax.ShapeDtypeStruct((bsz, nh_attn, seq // tm, ATTN_V_DIM, tm), BF16),
        jax.ShapeDtypeStruct((bsz, seq, d_attn), BF16),
        jax.ShapeDtypeStruct((bsz, seq, nh), F32),
        jax.ShapeDtypeStruct((bsz, nh, seq), F32),
    ]
    pad = V7X_SUBLANES
    vmem = (d * (n_main + 2 * d_attn) * 2 + 2 * tm * d * 4
            + 2 * tm * (2 * d_ssd + 2 * n_bc + 4 * d_attn) * 2
            + (tm + 2 * pad) * d_xbc * 4 + 6 * tm * d_xbc * 4)
    kern = functools.partial(_inproj_kernel, tm=tm, d_ssd=d_ssd, d_xbc=d_xbc, d_attn=d_attn, n_bc=n_bc)
    return pl.pallas_call(
        kern, grid=grid, in_specs=in_specs, out_specs=out_specs, out_shape=out_shape,
        scratch_shapes=[pltpu.VMEM((pad, d_xbc), F32), pltpu.VMEM((tm + pad, d_xbc), F32)],
        compiler_params=pltpu.CompilerParams(
            dimension_semantics=("arbitrary", "arbitrary"), vmem_limit_bytes=_vmem_limit(vmem)),
        name="in_projection",
    )(x, scale, shift, w_main, w_qv_t, w_dt, w_dt.T, conv_w, conv_b.reshape(1, -1),
      dt_bias.reshape(1, -1), dt_bias.reshape(-1, 1))


def _ssd_kernel(xs_ref, bm_ref, cm_ref, zs_ref, dt_ref, dtT_ref, alog_ref, alogT_ref, dskip_ref,
                normw_ref, y_ref, state_ref, ybuf_ref, *, nheads, d_ssd):
    t = CHUNK
    n = SSD_STATE
    hp = SSD_HEAD_DIM
    gw = d_ssd // SSD_GROUPS
    heads_per_group = nheads // SSD_GROUPS

    @pl.when(pl.program_id(1) == 0)
    def _():
        state_ref[...] = jnp.zeros_like(state_ref)

    row = lax.broadcasted_iota(jnp.int32, (t, t), 0)
    col = lax.broadcasted_iota(jnp.int32, (t, t), 1)
    causal = col <= row
    tri = jnp.where(causal, 1.0, 0.0).astype(BF16)
    tri_t = jnp.where(row <= col, 1.0, 0.0).astype(BF16)

    a = -jnp.exp(alog_ref[...])
    a_t = -jnp.exp(alogT_ref[...])
    dt = dt_ref[0]
    dt_t = dtT_ref[0]
    acum = sum(_dot(tri, p) for p in _split_bf16(dt * a, 3))
    acum_t = sum(_dot(p, tri_t) for p in _split_bf16(dt_t * a_t, 3))

    eh = lax.broadcasted_iota(jnp.int32, (nheads, d_ssd), 0)
    ej = lax.broadcasted_iota(jnp.int32, (nheads, d_ssd), 1)
    expand_m = jnp.where(ej // hp == eh, 1.0, 0.0).astype(BF16)

    def expand(v):
        return sum(_dot(p, expand_m) for p in _split_bf16(v, 2))

    last = acum[t - 1:t, :]
    dt_x = expand(dt)
    e_x = expand(jnp.exp(acum))
    w_x = expand(jnp.exp(last - acum))

    xs = xs_ref[0].astype(F32)
    xdt = xs * dt_x
    xdt_b = xdt.astype(BF16)
    xw_b = (xdt * w_x).astype(BF16)
    lane = lax.broadcasted_iota(jnp.int32, (1, 2 * hp), 1)
    neg_inf = jnp.float32(-jnp.inf)

    for g in range(SSD_GROUPS):
        b_g = bm_ref[0][:, g * n:(g + 1) * n]
        c_g = cm_ref[0][:, g * n:(g + 1) * n]
        gsl = slice(g * gw, (g + 1) * gw)
        cb = _dot_nt(c_g, b_g)
        prev = state_ref[g]
        y_off = _dot(c_g, prev.astype(BF16)) * e_x[:, gsl]
        new_state = prev * e_x[t - 1:t, gsl] + _dot_tn(b_g, xw_b[:, gsl])
        state_ref[g] = new_state
        for pair in range(heads_per_group // 2):
            lo = g * gw + pair * 2 * hp
            x_pair = xdt_b[:, lo:lo + 2 * hp]
            y_pair = y_off[:, pair * 2 * hp:(pair + 1) * 2 * hp]
            for half in range(2):
                hd = g * heads_per_group + pair * 2 + half
                diff = acum[:, hd:hd + 1] - acum_t[hd:hd + 1, :]
                m = cb * jnp.exp(jnp.where(causal, diff, neg_inf))
                keep = (lane >= half * hp) & (lane < (half + 1) * hp)
                x_half = jnp.where(keep, x_pair, jnp.zeros_like(x_pair))
                y_pair = y_pair + _dot(m.astype(BF16), x_half)
            ybuf_ref[:, lo:lo + 2 * hp] = y_pair

    y = ybuf_ref[...] + dskip_ref[...] * xs
    y = y * zs_ref[0].astype(F32)
    for g in range(SSD_GROUPS):
        gsl = slice(g * gw, (g + 1) * gw)
        y_g = y[:, gsl]
        ms = jnp.mean(y_g * y_g, axis=-1, keepdims=True)
        y_ref[0, :, gsl] = (y_g * lax.rsqrt(ms + EPS) * normw_ref[:, gsl]).astype(BF16)


def _ssd(xs, bm, cm, zs, dt, dt_t, a_log, d_skip, ssd_norm_w):
    bsz, seq, d_ssd = xs.shape
    nheads = dt.shape[-1]
    n_bc = bm.shape[-1]
    t = CHUNK
    grid = (bsz, seq // t)
    row = lambda b, c: (b, c, 0)
    const2 = lambda b, c: (0, 0)
    in_specs = [
        pl.BlockSpec((1, t, d_ssd), row),
        pl.BlockSpec((1, t, n_bc), row),
        pl.BlockSpec((1, t, n_bc), row),
        pl.BlockSpec((1, t, d_ssd), row),
        pl.BlockSpec((1, t, nheads), row),
        pl.BlockSpec((1, nheads, t), lambda b, c: (b, 0, c)),
        pl.BlockSpec((1, nheads), const2),
        pl.BlockSpec((nheads, 1), const2),
        pl.BlockSpec((1, d_ssd), const2),
        pl.BlockSpec((1, d_ssd), const2),
    ]
    kern = functools.partial(_ssd_kernel, nheads=nheads, d_ssd=d_ssd)
    return pl.pallas_call(
        kern, grid=grid, in_specs=in_specs,
        out_specs=pl.BlockSpec((1, t, d_ssd), row),
        out_shape=jax.ShapeDtypeStruct((bsz, seq, d_ssd), BF16),
        scratch_shapes=[pltpu.VMEM((SSD_GROUPS, SSD_STATE, d_ssd // SSD_GROUPS), F32),
                        pltpu.VMEM((t, d_ssd), F32)],
        compiler_params=pltpu.CompilerParams(dimension_semantics=("arbitrary", "arbitrary")),
        name="ssd_scan",
    )(xs, bm, cm, zs, dt, dt_t, a_log.reshape(1, -1), a_log.reshape(-1, 1),
      jnp.repeat(d_skip, SSD_HEAD_DIM).reshape(1, -1), ssd_norm_w.reshape(1, -1))


def _attn_kernel(qT_ref, k_ref, vT_ref, za_ref, lq1_ref, lk1_ref, lq2_ref, lk2_ref, normw_ref,
                 o_ref, qq_ref, s_ref, cmax_ref, m_ref, acc_ref, *, tk, nsub):
    qi = pl.program_id(2)
    dk = ATTN_QK_DIM
    dv = 2 * dk
    sw = 2 * tk
    feat = lax.broadcasted_iota(jnp.int32, (2 * dk, 1), 0)
    for sub in range(nsub):
        qT = qT_ref[0, 0, sub]
        zero = jnp.zeros_like(qT)
        qq_ref[:, sub * sw:sub * sw + tk] = jnp.where(feat < dk, qT, zero)
        qq_ref[:, sub * sw + tk:(sub + 1) * sw] = jnp.where(feat >= dk, qT, zero)

    m_ref[...] = jnp.full_like(m_ref, -jnp.inf)
    acc_ref[...] = jnp.zeros_like(acc_ref)
    ones_rows = jnp.ones((BF16_SUBLANE_PACK, tk), BF16)

    nchunk = 2 * nsub

    def col(c):
        return slice(c * tk, (c + 1) * tk)

    def produce(slot, kv, c):
        start = pl.multiple_of(kv * tk, tk)
        s = _dot(k_ref[0, pl.ds(start, tk), :], qq_ref[:, col(c)])
        s_ref[slot, :, col(c)] = s
        cmax_ref[slot, :, col(c)] = jnp.max(s, axis=0, keepdims=True)

    def consume(slot, kv, c, diagonal=False):
        s = s_ref[slot, :, col(c)]
        if diagonal:
            r = lax.broadcasted_iota(jnp.int32, s.shape, 0)
            qpos = lax.broadcasted_iota(jnp.int32, s.shape, 1)
            s = jnp.where(r <= qpos, s, -jnp.inf)
            cmax = jnp.max(s, axis=0, keepdims=True)
        else:
            cmax = cmax_ref[slot, :, col(c)]
        m_prev = m_ref[:, col(c)]
        m_new = jnp.maximum(m_prev, cmax)
        alpha = jnp.exp2(m_prev - m_new)
        p = jnp.exp2(s - m_new).astype(BF16)
        v_ext = jnp.concatenate([vT_ref[0, 0, kv], ones_rows], axis=0)
        acc_ref[:, col(c)] = alpha * acc_ref[:, col(c)] + _dot(v_ext, p)
        m_ref[:, col(c)] = m_new

    for c in range(nchunk):
        produce(0, 0, c)

    def pair(i, carry):
        kv = 2 * i
        for c in range(nchunk):
            produce(1, kv + 1, c)
            consume(0, kv, c)
        for c in range(nchunk):
            produce(0, kv + 2, c)
            consume(1, kv + 1, c)
        return carry

    lax.fori_loop(0, (nsub // 2) * qi, pair, 0)

    kv0 = nsub * qi
    for d in range(nsub):
        slot = d % 2
        cur = [c for c in range(nchunk) if c // 2 >= d]
        nxt = [c for c in range(nchunk) if c // 2 >= d + 1]
        for idx, c in enumerate(cur):
            if idx < len(nxt):
                produce(1 - slot, kv0 + d + 1, nxt[idx])
            consume(slot, kv0 + d, c, diagonal=(c // 2 == d))

    lam = (jnp.exp(jnp.sum(lq1_ref[...] * lk1_ref[...], axis=-1, keepdims=True))
           - jnp.exp(jnp.sum(lq2_ref[...] * lk2_ref[...], axis=-1, keepdims=True))
           + LAMBDA_INIT_L0)
    outs = []
    for sub in range(nsub):
        c1 = sub * sw
        c2 = c1 + tk
        o1 = acc_ref[0:dv, c1:c1 + tk] / acc_ref[dv:dv + 1, c1:c1 + tk]
        o2 = acc_ref[0:dv, c2:c2 + tk] / acc_ref[dv:dv + 1, c2:c2 + tk]
        outs.append(o1 - lam * o2)
    o = jnp.concatenate(outs, axis=1).T
    ms = jnp.mean(o * o, axis=-1, keepdims=True)
    o = o * lax.rsqrt(ms + EPS) * normw_ref[...] * (1.0 - LAMBDA_INIT_L0)
    o_ref[0] = (o * za_ref[0].astype(F32)).astype(BF16)


def _attention(q_t, k, v_t, za, lq1, lk1, lq2, lk2, attn_norm_w):
    bsz, seq, d_attn = k.shape
    hw = ATTN_V_DIM
    nh = d_attn // hw
    tk = TK_ATTN
    nsub = NSUB_ATTN
    tq = nsub * tk
    nblk = seq // tk
    assert nsub % 2 == 0 and seq % tq == 0
    assert q_t.shape == (bsz, nh, nblk, hw, tk) and v_t.shape == q_t.shape
    grid = (bsz, nh, seq // tq)
    qrow = lambda b, h, i: (b, i, h)
    const2 = lambda b, h, i: (0, 0)
    vec = pl.BlockSpec((1, ATTN_QK_DIM), const2)
    in_specs = [
        pl.BlockSpec((1, 1, nsub, hw, tk), lambda b, h, i: (b, h, i, 0, 0)),
        pl.BlockSpec((1, seq, hw), lambda b, h, i: (b, 0, h)),
        pl.BlockSpec((1, 1, nblk, hw, tk), lambda b, h, i: (b, h, 0, 0, 0)),
        pl.BlockSpec((1, tq, hw), qrow),
        vec, vec, vec, vec,
        pl.BlockSpec((1, hw), const2),
    ]
    width = 2 * tq
    vmem = (2 * 2 * seq * hw * 2 + 2 * tk * width * 4 + (2 * hw + BF16_SUBLANE_PACK) * width * 4
            + 4 * tk * width * 4)
    return pl.pallas_call(
        functools.partial(_attn_kernel, tk=tk, nsub=nsub), grid=grid, in_specs=in_specs,
        out_specs=pl.BlockSpec((1, tq, hw), qrow),
        out_shape=jax.ShapeDtypeStruct((bsz, seq, d_attn), BF16),
        scratch_shapes=[pltpu.VMEM((hw, width), BF16), pltpu.VMEM((2, tk, width), F32),
                        pltpu.VMEM((2, 1, width), F32), pltpu.VMEM((1, width), F32),
                        pltpu.VMEM((hw + BF16_SUBLANE_PACK, width), F32)],
        compiler_params=pltpu.CompilerParams(
            dimension_semantics=("arbitrary", "arbitrary", "arbitrary"),
            vmem_limit_bytes=_vmem_limit(vmem)),
        name="diff_attention",
    )(q_t, k, v_t, za, lq1.reshape(1, -1), lk1.reshape(1, -1), lq2.reshape(1, -1),
      lk2.reshape(1, -1), attn_norm_w.reshape(1, -1))


def _outproj_kernel(ys_ref, ya_ref, w_ref, x_ref, gate_ref, g_ref, b_ref, o_ref, *, d_ssd, alpha):
    mixed = _dot(ys_ref[0], w_ref[0:d_ssd, :]) + _dot(ya_ref[0], w_ref[d_ssd:, :])
    xf = alpha * x_ref[0] + gate_ref[0] * mixed
    mu = jnp.mean(xf, axis=-1, keepdims=True)
    xc = xf - mu
    var = jnp.mean(xc * xc, axis=-1, keepdims=True)
    o_ref[0] = xc * lax.rsqrt(var + EPS) * g_ref[...] + b_ref[...]


def _out_projection(y_ssd, y_attn, w_out, x, gate, ln_g, ln_b, alpha):
    bsz, seq, d = x.shape
    d_ssd = y_ssd.shape[-1]
    d_attn = y_attn.shape[-1]
    tm = TM_OUT
    row = lambda b, l: (b, l, 0)
    const2 = lambda b, l: (0, 0)
    in_specs = [
        pl.BlockSpec((1, tm, d_ssd), row),
        pl.BlockSpec((1, tm, d_attn), row),
        pl.BlockSpec((d_ssd + d_attn, d), const2, pipeline_mode=pl.Buffered(1)),
        pl.BlockSpec((1, tm, d), row),
        pl.BlockSpec((1, 1, d), lambda b, l: (b, 0, 0)),
        pl.BlockSpec((1, d), const2),
        pl.BlockSpec((1, d), const2),
    ]
    vmem = (d_ssd + d_attn) * d * 2 + 2 * tm * (d_ssd + d_attn) * 2 + 4 * tm * d * 4 + 4 * tm * d * 4
    return pl.pallas_call(
        functools.partial(_outproj_kernel, d_ssd=d_ssd, alpha=alpha),
        grid=(bsz, seq // tm), in_specs=in_specs,
        out_specs=pl.BlockSpec((1, tm, d), row),
        out_shape=jax.ShapeDtypeStruct((bsz, seq, d), F32),
        compiler_params=pltpu.CompilerParams(
            dimension_semantics=("arbitrary", "arbitrary"), vmem_limit_bytes=_vmem_limit(vmem)),
        name="out_projection",
    )(y_ssd, y_attn, w_out, x, gate, ln_g.reshape(1, -1), ln_b.reshape(1, -1))


def _layer(x, c, w_ada, b_ada, w_in, conv_w, conv_b, dt_bias, a_log, d_skip, ssd_norm_w,
           lambda_q1, lambda_k1, lambda_q2, lambda_k2, attn_norm_w, w_out, ln_g, ln_b, depth):
    bsz, seq, d = x.shape
    nheads = a_log.shape[0]
    d_ssd = nheads * SSD_HEAD_DIM
    d_xbc = conv_w.shape[1]
    d_attn = w_out.shape[0] - d_ssd
    assert d_xbc == d_ssd + 2 * SSD_GROUPS * SSD_STATE
    assert w_in.shape[1] == d_ssd + d_xbc + nheads + 4 * d_attn
    assert seq % TM_INPROJ == 0 and seq % TM_OUT == 0 and seq % CHUNK == 0
    assert TM_INPROJ == TK_ATTN, "the in-projection writes q^T / v^T in attention-sized time blocks"

    mod = _modulation(c, w_ada, b_ada)
    shift = mod[:, None, 0:d]
    scale = mod[:, None, d:2 * d]
    gate = mod[:, None, 2 * d:3 * d]

    off_dt = d_ssd + d_xbc
    off_q = off_dt + nheads
    off_k, off_v, off_za = off_q + d_attn, off_q + 2 * d_attn, off_q + 3 * d_attn
    w_main = jnp.concatenate([w_in[:, :off_dt], w_in[:, off_k:off_v], w_in[:, off_za:]],
                             axis=1).astype(BF16)
    w_qv_t = jnp.concatenate([w_in[:, off_q:off_k], w_in[:, off_v:off_za]], axis=1).T.astype(BF16)
    w_dt = w_in[:, off_dt:off_q]

    zs, xs, bm, cm, q_t, k, v_t, za, dt, dt_t = _in_projection(
        x, scale, shift, w_main, w_qv_t, w_dt, conv_w, conv_b, dt_bias,
        d_ssd=d_ssd, d_xbc=d_xbc, d_attn=d_attn)
    y_ssd = _ssd(xs, bm, cm, zs, dt, dt_t, a_log, d_skip, ssd_norm_w)
    y_attn = _attention(q_t, k, v_t, za, lambda_q1, lambda_k1, lambda_q2, lambda_k2, attn_norm_w)
    alpha = (2.0 * depth) ** 0.25
    return _out_projection(y_ssd, y_attn, w_out.astype(BF16), x, gate, ln_g, ln_b, alpha)


def kernel(x, c, w_ada, b_ada, w_in, conv_w, conv_b, dt_bias, a_log, d_skip, ssd_norm_w, lambda_q1,
           lambda_k1, lambda_q2, lambda_k2, attn_norm_w, w_out, ln_g, ln_b):
    depth = w_in.shape[0]
    assert depth == 1, "lambda_init and the layer loop are specialised to a single layer"
    return _layer(x, c, w_ada[0], b_ada[0], w_in[0], conv_w[0], conv_b[0], dt_bias[0], a_log[0],
                  d_skip[0], ssd_norm_w[0], lambda_q1[0], lambda_k1[0], lambda_q2[0], lambda_k2[0],
                  attn_norm_w[0], w_out[0], ln_g[0], ln_b[0], depth)
```

```python
import functools
import math

import jax
import jax.numpy as jnp
from jax import lax
from jax.experimental import pallas as pl
from jax.experimental.pallas import tpu as pltpu

F32 = jnp.float32
BF16 = jnp.bfloat16

SSD_HEAD_DIM = 64
SSD_GROUPS = 2
SSD_STATE = 128
CONV_WIDTH = 4
CHUNK = 128
ATTN_QK_DIM = 64
ATTN_V_DIM = 2 * ATTN_QK_DIM
EPS = 1e-5
LAMBDA_INIT_L0 = 0.8 - 0.6 * math.exp(-0.3 * 0)
LOG2_E = math.log2(math.e)

V7X_LANES = 128
V7X_SUBLANES = 8
BF16_SUBLANE_PACK = 16
V7X_VMEM_BYTES = 64 * 1024 * 1024
V7X_VMEM_USABLE_BYTES = 56 * 1024 * 1024

TM_INPROJ = 512
TK_ATTN = 512
NSUB_ATTN = 2
TM_OUT = 512
SSD_CHUNKS_PER_STEP = 4


def _vmem_limit(estimate_bytes):
    return int(min(V7X_VMEM_USABLE_BYTES, max(32 * 1024 * 1024, estimate_bytes)))


def _silu(x):
    return x * (1.0 / (1.0 + jnp.exp(-x)))


def _softplus(x):
    return jnp.maximum(x, 0.0) + jnp.log(1.0 + jnp.exp(-jnp.abs(x)))


def _split_bf16(a, pieces):
    out = []
    r = a
    for i in range(pieces):
        p = r.astype(BF16)
        out.append(p)
        if i + 1 < pieces:
            r = r - p.astype(F32)
    return out


def _dot(a, b):
    return jnp.dot(a, b, preferred_element_type=F32)


def _dot_nt(a, b):
    return lax.dot_general(a, b, (((1,), (1,)), ((), ())), preferred_element_type=F32)


def _dot_tn(a, b):
    return lax.dot_general(a, b, (((0,), (0,)), ((), ())), preferred_element_type=F32)


def _mod_kernel(c_ref, w_ref, b_ref, o_ref):
    c_hi, c_lo = _split_bf16(c_ref[...], 2)
    w_hi, w_lo = _split_bf16(w_ref[...], 2)
    o_ref[...] = _dot(c_hi, w_hi) + _dot(c_hi, w_lo) + _dot(c_lo, w_hi) + b_ref[...]


def _modulation(c, w_ada, b_ada):
    bsz, d = c.shape
    n = w_ada.shape[1]
    rows = V7X_SUBLANES
    c_pad = jnp.zeros((rows, d), F32).at[:bsz].set(c)
    bn = d
    out = pl.pallas_call(
        _mod_kernel,
        grid=(n // bn,),
        in_specs=[pl.BlockSpec((rows, d), lambda j: (0, 0)),
                  pl.BlockSpec((d, bn), lambda j: (0, j)),
                  pl.BlockSpec((1, bn), lambda j: (0, j))],
        out_specs=pl.BlockSpec((rows, bn), lambda j: (0, j)),
        out_shape=jax.ShapeDtypeStruct((rows, n), F32),
        name="modulation",
    )(c_pad, w_ada, b_ada.reshape(1, n))
    return out[:bsz]


def _inproj_kernel(x_ref, scale_ref, shift_ref, w_ref, wqvT_ref, wdt_ref, convw_ref,
                   convb_ref, dtb_ref,
                   zs_ref, xs_ref, bm_ref, cm_ref, qT_ref, k_ref, vT_ref, za_ref, dt_ref, dtT_ref,
                   carry_ref, ext_ref, *, tm, d_ssd, d_xbc, d_attn, n_bc):
    @pl.when(pl.program_id(1) == 0)
    def _():
        carry_ref[...] = jnp.zeros_like(carry_ref)

    h32 = x_ref[0] * (1.0 + scale_ref[0]) + shift_ref[0]
    h = h32.astype(BF16)

    def proj(lo, width):
        return _dot(h, w_ref[:, lo:lo + width])

    off = 0
    zs_ref[0] = _silu(proj(off, d_ssd)).astype(BF16)
    off += d_ssd

    u = proj(off, d_xbc)
    off += d_xbc
    pad = V7X_SUBLANES
    ext_ref[0:pad, :] = carry_ref[...]
    ext_ref[pad:pad + tm, :] = u
    carry_ref[...] = u[tm - pad:tm, :]
    cw = convw_ref[...]
    acc = convb_ref[...] + cw[CONV_WIDTH - 1:CONV_WIDTH, :] * u
    for tap in range(CONV_WIDTH - 1):
        shift = CONV_WIDTH - 1 - tap
        acc = acc + cw[tap:tap + 1, :] * ext_ref[pad - shift:pad - shift + tm, :]
    xbc = _silu(acc)
    xs_ref[0] = xbc[:, :d_ssd].astype(BF16)
    bm_ref[0] = xbc[:, d_ssd:d_ssd + n_bc].astype(BF16)
    cm_ref[0] = xbc[:, d_ssd + n_bc:d_ssd + 2 * n_bc].astype(BF16)

    k_ref[0] = proj(off, d_attn).astype(BF16)
    off += d_attn
    za_ref[0] = _silu(proj(off, d_attn)).astype(BF16)

    qvT = _dot_nt(wqvT_ref[...], h)
    nh_attn = d_attn // ATTN_V_DIM
    qT = qvT[:d_attn] * (ATTN_QK_DIM ** -0.5 * LOG2_E)
    qT_ref[0, :, 0] = qT.astype(BF16).reshape(nh_attn, ATTN_V_DIM, tm)
    vT_ref[0, :, 0] = qvT[d_attn:].astype(BF16).reshape(nh_attn, ATTN_V_DIM, tm)

    nh = dt_ref.shape[-1]
    h_lo = (h32 - h.astype(F32)).astype(BF16)
    w_hi, w_lo = _split_bf16(wdt_ref[...], 2)
    wlane = lax.broadcasted_iota(jnp.int32, (1, V7X_LANES), 1)
    r1 = _dot(h, jnp.where(wlane < nh, w_hi, w_lo))
    r2 = _dot(h_lo, w_hi)
    dt_raw = r1 + pltpu.roll(r1, V7X_LANES - nh, 1) + r2
    dt = _softplus(dt_raw + dtb_ref[...])
    dt_ref[0] = dt[:, :nh]
    dtT_ref[0] = dt.T[:nh, :]


def _in_projection(x, scale, shift, w_main, w_qv_t, w_dt, conv_w, conv_b, dt_bias, *, d_ssd, d_xbc,
                   d_attn):
    bsz, seq, d = x.shape
    tm = TM_INPROJ
    nh = w_dt.shape[1]
    nh_attn = d_attn // ATTN_V_DIM
    n_bc = (d_xbc - d_ssd) // 2
    n_main = w_main.shape[1]
    grid = (bsz, seq // tm)

    row = lambda b, l: (b, l, 0)
    head_major = lambda b, l: (b, 0, l, 0, 0)
    const2 = lambda b, l: (0, 0)
    per_b = lambda b, l: (b, 0, 0)
    single = pl.Buffered(1)
    in_specs = [
        pl.BlockSpec((1, tm, d), row),
        pl.BlockSpec((1, 1, d), per_b),
        pl.BlockSpec((1, 1, d), per_b),
        pl.BlockSpec((d, n_main), const2, pipeline_mode=single),
        pl.BlockSpec((2 * d_attn, d), const2, pipeline_mode=single),
        pl.BlockSpec((d, V7X_LANES), const2),
        pl.BlockSpec((CONV_WIDTH, d_xbc), const2),
        pl.BlockSpec((1, d_xbc), const2),
        pl.BlockSpec((1, V7X_LANES), const2),
    ]
    out_specs = [
        pl.BlockSpec((1, tm, d_ssd), row),
        pl.BlockSpec((1, tm, d_ssd), row),
        pl.BlockSpec((1, tm, n_bc), row),
        pl.BlockSpec((1, tm, n_bc), row),
        pl.BlockSpec((1, nh_attn, 1, ATTN_V_DIM, tm), head_major),
        pl.BlockSpec((1, tm, d_attn), row),
        pl.BlockSpec((1, nh_attn, 1, ATTN_V_DIM, tm), head_major),
        pl.BlockSpec((1, tm, d_attn), row),
        pl.BlockSpec((1, tm, nh), row),
        pl.BlockSpec((1, nh, tm), lambda b, l: (b, 0, l)),
    ]
    out_shape = [
        jax.ShapeDtypeStruct((bsz, seq, d_ssd), BF16),
        jax.ShapeDtypeStruct((bsz, seq, d_ssd), BF16),
        jax.ShapeDtypeStruct((bsz, seq, n_bc), BF16),
        jax.ShapeDtypeStruct((bsz, seq, n_bc), BF16),
        jax.ShapeDtypeStruct((bsz, nh_attn, seq // tm, ATTN_V_DIM, tm), BF16),
        jax.ShapeDtypeStruct((bsz, seq, d_attn), BF16),
        jax.ShapeDtypeStruct((bsz, nh_attn, seq // tm, ATTN_V_DIM, tm), BF16),
        jax.ShapeDtypeStruct((bsz, seq, d_attn), BF16),
        jax.ShapeDtypeStruct((bsz, seq, nh), F32),
        jax.ShapeDtypeStruct((bsz, nh, seq), F32),
    ]
    pad = V7X_SUBLANES
    vmem = (d * (n_main + 2 * d_attn) * 2 + 2 * tm * d * 4
            + 2 * tm * (2 * d_ssd + 2 * n_bc + 4 * d_attn) * 2
            + (tm + 2 * pad) * d_xbc * 4 + 6 * tm * d_xbc * 4)
    kern = functools.partial(_inproj_kernel, tm=tm, d_ssd=d_ssd, d_xbc=d_xbc, d_attn=d_attn, n_bc=n_bc)
    assert 2 * nh <= V7X_LANES
    w_dt_pad = jnp.zeros((d, V7X_LANES), F32).at[:, :nh].set(w_dt).at[:, nh:2 * nh].set(w_dt)
    dt_bias_pad = jnp.zeros((1, V7X_LANES), F32).at[0, :nh].set(dt_bias)
    return pl.pallas_call(
        kern, grid=grid, in_specs=in_specs, out_specs=out_specs, out_shape=out_shape,
        scratch_shapes=[pltpu.VMEM((pad, d_xbc), F32), pltpu.VMEM((tm + pad, d_xbc), F32)],
        compiler_params=pltpu.CompilerParams(
            dimension_semantics=("arbitrary", "arbitrary"), vmem_limit_bytes=_vmem_limit(vmem)),
        name="in_projection",
    )(x, scale, shift, w_main, w_qv_t, w_dt_pad, conv_w, conv_b.reshape(1, -1), dt_bias_pad)


def _ssd_kernel(xs_ref, bm_ref, cm_ref, zs_ref, dt_ref, dtT_ref, alog_ref, alogT_ref, dskip_ref,
                normw_ref, y_ref, state_ref, ybuf_ref, *, nheads, d_ssd, cps):
    t = CHUNK
    n = SSD_STATE
    hp = SSD_HEAD_DIM
    gw = d_ssd // SSD_GROUPS
    heads_per_group = nheads // SSD_GROUPS

    @pl.when(pl.program_id(1) == 0)
    def _():
        state_ref[...] = jnp.zeros_like(state_ref)

    row = lax.broadcasted_iota(jnp.int32, (t, t), 0)
    col = lax.broadcasted_iota(jnp.int32, (t, t), 1)
    causal = col <= row
    tri = jnp.where(causal, 1.0, 0.0).astype(BF16)
    tri_t = jnp.where(row <= col, 1.0, 0.0).astype(BF16)
    eh = lax.broadcasted_iota(jnp.int32, (nheads, d_ssd), 0)
    ej = lax.broadcasted_iota(jnp.int32, (nheads, d_ssd), 1)
    expand_m = jnp.where(ej // hp == eh, 1.0, 0.0).astype(BF16)
    lane = lax.broadcasted_iota(jnp.int32, (1, 2 * hp), 1)
    neg_inf = jnp.float32(-jnp.inf)
    a = -jnp.exp(alog_ref[...])
    a_t = -jnp.exp(alogT_ref[...])

    def expand(v, pieces):
        return sum(_dot(p, expand_m) for p in _split_bf16(v, pieces))

    for ci in range(cps):
        rows = slice(ci * t, (ci + 1) * t)
        dt = dt_ref[0, rows, :]
        dt_t = dtT_ref[0, :, rows]
        acum = sum(_dot(tri, p) for p in _split_bf16(dt * a, 3))
        acum_t = sum(_dot(p, tri_t) for p in _split_bf16(dt_t * a_t, 3))
        last = acum[t - 1:t, :]
        dt_x = expand(dt, 1)
        w_x = expand(jnp.exp(last - acum), 1)
        e_x = expand(jnp.exp(acum), 2)

        xs = xs_ref[0, rows, :].astype(F32)
        xdt = xs * dt_x
        xdt_b = xdt.astype(BF16)
        xw_b = (xdt * w_x).astype(BF16)

        for g in range(SSD_GROUPS):
            b_g = bm_ref[0, rows, g * n:(g + 1) * n]
            c_g = cm_ref[0, rows, g * n:(g + 1) * n]
            gsl = slice(g * gw, (g + 1) * gw)
            cb = _dot_nt(c_g, b_g)
            prev = state_ref[g]
            y_off = _dot(c_g, prev.astype(BF16)) * e_x[:, gsl]
            state_ref[g] = prev * e_x[t - 1:t, gsl] + _dot_tn(b_g, xw_b[:, gsl])
            for pair in range(heads_per_group // 2):
                lo = g * gw + pair * 2 * hp
                x_pair = xdt_b[:, lo:lo + 2 * hp]
                decays = []
                for half in range(2):
                    hd = g * heads_per_group + pair * 2 + half
                    diff = acum[:, hd:hd + 1] - acum_t[hd:hd + 1, :]
                    decays.append((cb * jnp.exp(jnp.where(causal, diff, neg_inf))).astype(BF16))
                zero = jnp.zeros_like(x_pair)
                x_diag = jnp.concatenate([jnp.where(lane < hp, x_pair, zero),
                                          jnp.where(lane >= hp, x_pair, zero)], axis=0)
                y_pair = (y_off[:, pair * 2 * hp:(pair + 1) * 2 * hp]
                          + _dot(jnp.concatenate(decays, axis=1), x_diag))
                ybuf_ref[rows, lo:lo + 2 * hp] = y_pair

    xs = xs_ref[0].astype(F32)
    y = ybuf_ref[...] + dskip_ref[...] * xs
    y = y * zs_ref[0].astype(F32)
    for g in range(SSD_GROUPS):
        gsl = slice(g * gw, (g + 1) * gw)
        y_g = y[:, gsl]
        ms = jnp.mean(y_g * y_g, axis=-1, keepdims=True)
        y_ref[0, :, gsl] = (y_g * lax.rsqrt(ms + EPS) * normw_ref[:, gsl]).astype(BF16)


def _ssd(xs, bm, cm, zs, dt, dt_t, a_log, d_skip, ssd_norm_w):
    bsz, seq, d_ssd = xs.shape
    nheads = dt.shape[-1]
    n_bc = bm.shape[-1]
    cps = SSD_CHUNKS_PER_STEP
    t = cps * CHUNK
    assert seq % t == 0
    grid = (bsz, seq // t)
    row = lambda b, c: (b, c, 0)
    const2 = lambda b, c: (0, 0)
    in_specs = [
        pl.BlockSpec((1, t, d_ssd), row),
        pl.BlockSpec((1, t, n_bc), row),
        pl.BlockSpec((1, t, n_bc), row),
        pl.BlockSpec((1, t, d_ssd), row),
        pl.BlockSpec((1, t, nheads), row),
        pl.BlockSpec((1, nheads, t), lambda b, c: (b, 0, c)),
        pl.BlockSpec((1, nheads), const2),
        pl.BlockSpec((nheads, 1), const2),
        pl.BlockSpec((1, d_ssd), const2),
        pl.BlockSpec((1, d_ssd), const2),
    ]
    kern = functools.partial(_ssd_kernel, nheads=nheads, d_ssd=d_ssd, cps=cps)
    return pl.pallas_call(
        kern, grid=grid, in_specs=in_specs,
        out_specs=pl.BlockSpec((1, t, d_ssd), row),
        out_shape=jax.ShapeDtypeStruct((bsz, seq, d_ssd), BF16),
        scratch_shapes=[pltpu.VMEM((SSD_GROUPS, SSD_STATE, d_ssd // SSD_GROUPS), F32),
                        pltpu.VMEM((t, d_ssd), F32)],
        compiler_params=pltpu.CompilerParams(dimension_semantics=("arbitrary", "arbitrary")),
        name="ssd_scan",
    )(xs, bm, cm, zs, dt, dt_t, a_log.reshape(1, -1), a_log.reshape(-1, 1),
      jnp.repeat(d_skip, SSD_HEAD_DIM).reshape(1, -1), ssd_norm_w.reshape(1, -1))


def _attn_kernel(qT_ref, k_ref, vT_ref, za_ref, lq1_ref, lk1_ref, lq2_ref, lk2_ref, normw_ref,
                 o_ref, qq_ref, s_ref, cmax_ref, m_ref, acc_ref, *, tk, nsub):
    qi = pl.program_id(2)
    dk = ATTN_QK_DIM
    dv = 2 * dk
    sw = 2 * tk
    feat = lax.broadcasted_iota(jnp.int32, (2 * dk, 1), 0)
    for sub in range(nsub):
        qT = qT_ref[0, 0, sub]
        zero = jnp.zeros_like(qT)
        qq_ref[:, sub * sw:sub * sw + tk] = jnp.where(feat < dk, qT, zero)
        qq_ref[:, sub * sw + tk:(sub + 1) * sw] = jnp.where(feat >= dk, qT, zero)

    m_ref[...] = jnp.full_like(m_ref, -jnp.inf)
    acc_ref[...] = jnp.zeros_like(acc_ref)
    ones_rows = jnp.ones((BF16_SUBLANE_PACK, tk), BF16)

    nchunk = 2 * nsub

    def col(c):
        return slice(c * tk, (c + 1) * tk)

    def produce(slot, kv, c):
        start = pl.multiple_of(kv * tk, tk)
        s = _dot(k_ref[0, pl.ds(start, tk), :], qq_ref[:, col(c)])
        s_ref[slot, :, col(c)] = s
        cmax_ref[slot, :, col(c)] = jnp.max(s, axis=0, keepdims=True)

    def consume(slot, kv, c, diagonal=False):
        s = s_ref[slot, :, col(c)]
        if diagonal:
            r = lax.broadcasted_iota(jnp.int32, s.shape, 0)
            qpos = lax.broadcasted_iota(jnp.int32, s.shape, 1)
            s = jnp.where(r <= qpos, s, -jnp.inf)
            cmax = jnp.max(s, axis=0, keepdims=True)
        else:
            cmax = cmax_ref[slot, :, col(c)]
        m_prev = m_ref[:, col(c)]
        m_new = jnp.maximum(m_prev, cmax)
        alpha = jnp.exp2(m_prev - m_new)
        p = jnp.exp2(s - m_new).astype(BF16)
        v_ext = jnp.concatenate([vT_ref[0, 0, kv], ones_rows], axis=0)
        acc_ref[:, col(c)] = alpha * acc_ref[:, col(c)] + _dot(v_ext, p)
        m_ref[:, col(c)] = m_new

    for c in range(nchunk):
        produce(0, 0, c)

    def pair(i, carry):
        kv = 2 * i
        for c in range(nchunk):
            produce(1, kv + 1, c)
            consume(0, kv, c)
        for c in range(nchunk):
            produce(0, kv + 2, c)
            consume(1, kv + 1, c)
        return carry

    lax.fori_loop(0, (nsub // 2) * qi, pair, 0)

    kv0 = nsub * qi
    for d in range(nsub):
        slot = d % 2
        cur = [c for c in range(nchunk) if c // 2 >= d]
        nxt = [c for c in range(nchunk) if c // 2 >= d + 1]
        for idx, c in enumerate(cur):
            if idx < len(nxt):
                produce(1 - slot, kv0 + d + 1, nxt[idx])
            consume(slot, kv0 + d, c, diagonal=(c // 2 == d))

    lam = (jnp.exp(jnp.sum(lq1_ref[...] * lk1_ref[...], axis=-1, keepdims=True))
           - jnp.exp(jnp.sum(lq2_ref[...] * lk2_ref[...], axis=-1, keepdims=True))
           + LAMBDA_INIT_L0)
    outs = []
    for sub in range(nsub):
        c1 = sub * sw
        c2 = c1 + tk
        o1 = acc_ref[0:dv, c1:c1 + tk] / acc_ref[dv:dv + 1, c1:c1 + tk]
        o2 = acc_ref[0:dv, c2:c2 + tk] / acc_ref[dv:dv + 1, c2:c2 + tk]
        outs.append(o1 - lam * o2)
    o = jnp.concatenate(outs, axis=1).T
    ms = jnp.mean(o * o, axis=-1, keepdims=True)
    o = o * lax.rsqrt(ms + EPS) * normw_ref[...] * (1.0 - LAMBDA_INIT_L0)
    o_ref[0] = (o * za_ref[0].astype(F32)).astype(BF16)


def _attention(q_t, k, v_t, za, lq1, lk1, lq2, lk2, attn_norm_w):
    bsz, seq, d_attn = k.shape
    hw = ATTN_V_DIM
    nh = d_attn // hw
    tk = TK_ATTN
    nsub = NSUB_ATTN
    tq = nsub * tk
    nblk = seq // tk
    assert nsub % 2 == 0 and seq % tq == 0
    assert q_t.shape == (bsz, nh, nblk, hw, tk) and v_t.shape == q_t.shape
    grid = (bsz, nh, seq // tq)
    qrow = lambda b, h, i: (b, i, h)
    const2 = lambda b, h, i: (0, 0)
    vec = pl.BlockSpec((1, ATTN_QK_DIM), const2)
    in_specs = [
        pl.BlockSpec((1, 1, nsub, hw, tk), lambda b, h, i: (b, h, i, 0, 0)),
        pl.BlockSpec((1, seq, hw), lambda b, h, i: (b, 0, h)),
        pl.BlockSpec((1, 1, nblk, hw, tk), lambda b, h, i: (b, h, 0, 0, 0)),
        pl.BlockSpec((1, tq, hw), qrow),
        vec, vec, vec, vec,
        pl.BlockSpec((1, hw), const2),
    ]
    width = 2 * tq
    vmem = (2 * 2 * seq * hw * 2 + 2 * tk * width * 4 + (2 * hw + BF16_SUBLANE_PACK) * width * 4
            + 4 * tk * width * 4)
    return pl.pallas_call(
        functools.partial(_attn_kernel, tk=tk, nsub=nsub), grid=grid, in_specs=in_specs,
        out_specs=pl.BlockSpec((1, tq, hw), qrow),
        out_shape=jax.ShapeDtypeStruct((bsz, seq, d_attn), BF16),
        scratch_shapes=[pltpu.VMEM((hw, width), BF16), pltpu.VMEM((2, tk, width), F32),
                        pltpu.VMEM((2, 1, width), F32), pltpu.VMEM((1, width), F32),
                        pltpu.VMEM((hw + BF16_SUBLANE_PACK, width), F32)],
        compiler_params=pltpu.CompilerParams(
            dimension_semantics=("arbitrary", "arbitrary", "arbitrary"),
            vmem_limit_bytes=_vmem_limit(vmem)),
        name="diff_attention",
    )(q_t, k, v_t, za, lq1.reshape(1, -1), lk1.reshape(1, -1), lq2.reshape(1, -1),
      lk2.reshape(1, -1), attn_norm_w.reshape(1, -1))


def _outproj_kernel(ys_ref, ya_ref, w_ref, x_ref, gate_ref, g_ref, b_ref, o_ref, *, d_ssd, alpha):
    mixed = _dot(ys_ref[0], w_ref[0:d_ssd, :]) + _dot(ya_ref[0], w_ref[d_ssd:, :])
    xf = alpha * x_ref[0] + gate_ref[0] * mixed
    mu = jnp.mean(xf, axis=-1, keepdims=True)
    xc = xf - mu
    var = jnp.mean(xc * xc, axis=-1, keepdims=True)
    o_ref[0] = xc * lax.rsqrt(var + EPS) * g_ref[...] + b_ref[...]


def _out_projection(y_ssd, y_attn, w_out, x, gate, ln_g, ln_b, alpha):
    bsz, seq, d = x.shape
    d_ssd = y_ssd.shape[-1]
    d_attn = y_attn.shape[-1]
    tm = TM_OUT
    row = lambda b, l: (b, l, 0)
    const2 = lambda b, l: (0, 0)
    in_specs = [
        pl.BlockSpec((1, tm, d_ssd), row),
        pl.BlockSpec((1, tm, d_attn), row),
        pl.BlockSpec((d_ssd + d_attn, d), const2, pipeline_mode=pl.Buffered(1)),
        pl.BlockSpec((1, tm, d), row),
        pl.BlockSpec((1, 1, d), lambda b, l: (b, 0, 0)),
        pl.BlockSpec((1, d), const2),
        pl.BlockSpec((1, d), const2),
    ]
    vmem = (d_ssd + d_attn) * d * 2 + 2 * tm * (d_ssd + d_attn) * 2 + 4 * tm * d * 4 + 4 * tm * d * 4
    return pl.pallas_call(
        functools.partial(_outproj_kernel, d_ssd=d_ssd, alpha=alpha),
        grid=(bsz, seq // tm), in_specs=in_specs,
        out_specs=pl.BlockSpec((1, tm, d), row),
        out_shape=jax.ShapeDtypeStruct((bsz, seq, d), F32),
        compiler_params=pltpu.CompilerParams(
            dimension_semantics=("arbitrary", "arbitrary"), vmem_limit_bytes=_vmem_limit(vmem)),
        name="out_projection",
    )(y_ssd, y_attn, w_out, x, gate, ln_g.reshape(1, -1), ln_b.reshape(1, -1))


def _layer(x, c, w_ada, b_ada, w_in, conv_w, conv_b, dt_bias, a_log, d_skip, ssd_norm_w,
           lambda_q1, lambda_k1, lambda_q2, lambda_k2, attn_norm_w, w_out, ln_g, ln_b, depth):
    bsz, seq, d = x.shape
    nheads = a_log.shape[0]
    d_ssd = nheads * SSD_HEAD_DIM
    d_xbc = conv_w.shape[1]
    d_attn = w_out.shape[0] - d_ssd
    assert d_xbc == d_ssd + 2 * SSD_GROUPS * SSD_STATE
    assert w_in.shape[1] == d_ssd + d_xbc + nheads + 4 * d_attn
    assert seq % TM_INPROJ == 0 and seq % TM_OUT == 0 and seq % CHUNK == 0
    assert TM_INPROJ == TK_ATTN, "the in-projection writes q^T / v^T in attention-sized time blocks"

    mod = _modulation(c, w_ada, b_ada)
    shift = mod[:, None, 0:d]
    scale = mod[:, None, d:2 * d]
    gate = mod[:, None, 2 * d:3 * d]

    off_dt = d_ssd + d_xbc
    off_q = off_dt + nheads
    off_k, off_v, off_za = off_q + d_attn, off_q + 2 * d_attn, off_q + 3 * d_attn
    w_main = jnp.concatenate([w_in[:, :off_dt], w_in[:, off_k:off_v], w_in[:, off_za:]],
                             axis=1).astype(BF16)
    w_qv_t = jnp.concatenate([w_in[:, off_q:off_k], w_in[:, off_v:off_za]], axis=1).T.astype(BF16)
    w_dt = w_in[:, off_dt:off_q]

    zs, xs, bm, cm, q_t, k, v_t, za, dt, dt_t = _in_projection(
        x, scale, shift, w_main, w_qv_t, w_dt, conv_w, conv_b, dt_bias,
        d_ssd=d_ssd, d_xbc=d_xbc, d_attn=d_attn)
    y_ssd = _ssd(xs, bm, cm, zs, dt, dt_t, a_log, d_skip, ssd_norm_w)
    y_attn = _attention(q_t, k, v_t, za, lambda_q1, lambda_k1, lambda_q2, lambda_k2, attn_norm_w)
    alpha = (2.0 * depth) ** 0.25
    return _out_projection(y_ssd, y_attn, w_out.astype(BF16), x, gate, ln_g, ln_b, alpha)


def kernel(x, c, w_ada, b_ada, w_in, conv_w, conv_b, dt_bias, a_log, d_skip, ssd_norm_w, lambda_q1,
           lambda_k1, lambda_q2, lambda_k2, attn_norm_w, w_out, ln_g, ln_b):
    depth = w_in.shape[0]
    assert depth == 1, "lambda_init and the layer loop are specialised to a single layer"
    return _layer(x, c, w_ada[0], b_ada[0], w_in[0], conv_w[0], conv_b[0], dt_bias[0], a_log[0],
                  d_skip[0], ssd_norm_w[0], lambda_q1[0], lambda_k1[0], lambda_q2[0], lambda_k2[0],
                  attn_norm_w[0], w_out[0], ln_g[0], ln_b[0], depth)
```

```python
import functools
import math

import jax
import jax.numpy as jnp
from jax import lax
from jax.experimental import pallas as pl
from jax.experimental.pallas import tpu as pltpu

F32 = jnp.float32
BF16 = jnp.bfloat16

SSD_HEAD_DIM = 64
SSD_GROUPS = 2
SSD_STATE = 128
CONV_WIDTH = 4
CHUNK = 128
ATTN_QK_DIM = 64
ATTN_V_DIM = 2 * ATTN_QK_DIM
EPS = 1e-5
LAMBDA_INIT_L0 = 0.8 - 0.6 * math.exp(-0.3 * 0)
LOG2_E = math.log2(math.e)

V7X_LANES = 128
V7X_SUBLANES = 8
BF16_SUBLANE_PACK = 16
V7X_VMEM_BYTES = 64 * 1024 * 1024
V7X_VMEM_USABLE_BYTES = 56 * 1024 * 1024

TM_INPROJ = 512
TK_ATTN = 512
NSUB_ATTN = 2
ATTN_PRODUCE_LEAD = 2
TM_OUT = 512
SSD_CHUNKS_PER_STEP = 4


def _vmem_limit(estimate_bytes):
    return int(min(V7X_VMEM_USABLE_BYTES, max(32 * 1024 * 1024, estimate_bytes)))


def _silu(x):
    hx = 0.5 * x
    return hx + hx * jnp.tanh(hx)


def _softplus(x):
    return jnp.maximum(x, 0.0) + jnp.log(1.0 + jnp.exp(-jnp.abs(x)))


def _split_bf16(a, pieces):
    out = []
    r = a
    for i in range(pieces):
        p = r.astype(BF16)
        out.append(p)
        if i + 1 < pieces:
            r = r - p.astype(F32)
    return out


def _dot(a, b):
    return jnp.dot(a, b, preferred_element_type=F32)


def _dot_nt(a, b):
    return lax.dot_general(a, b, (((1,), (1,)), ((), ())), preferred_element_type=F32)


def _dot_tn(a, b):
    return lax.dot_general(a, b, (((0,), (0,)), ((), ())), preferred_element_type=F32)


def _mod_kernel(c_ref, w_ref, b_ref, o_ref):
    c_hi, c_lo = _split_bf16(c_ref[...], 2)
    w_hi, w_lo = _split_bf16(w_ref[...], 2)
    o_ref[...] = _dot(c_hi, w_hi) + _dot(c_hi, w_lo) + _dot(c_lo, w_hi) + b_ref[...]


def _modulation(c, w_ada, b_ada):
    bsz, d = c.shape
    n = w_ada.shape[1]
    rows = V7X_SUBLANES
    c_pad = jnp.zeros((rows, d), F32).at[:bsz].set(c)
    bn = d
    out = pl.pallas_call(
        _mod_kernel,
        grid=(n // bn,),
        in_specs=[pl.BlockSpec((rows, d), lambda j: (0, 0)),
                  pl.BlockSpec((d, bn), lambda j: (0, j)),
                  pl.BlockSpec((1, bn), lambda j: (0, j))],
        out_specs=pl.BlockSpec((rows, bn), lambda j: (0, j)),
        out_shape=jax.ShapeDtypeStruct((rows, n), F32),
        name="modulation",
    )(c_pad, w_ada, b_ada.reshape(1, n))
    return out[:bsz]


def _inproj_kernel(x_ref, scale_ref, shift_ref, w_ref, wqvT_ref, wdt_ref, convw_ref,
                   convb_ref, dtb_ref,
                   zs_ref, xs_ref, bm_ref, cm_ref, qT_ref, k_ref, vT_ref, za_ref, dt_ref, dtT_ref,
                   carry_ref, *, tm, d_ssd, d_xbc, d_attn, n_bc):
    @pl.when(pl.program_id(1) == 0)
    def _():
        carry_ref[...] = jnp.zeros_like(carry_ref)

    h32 = x_ref[0] * (1.0 + scale_ref[0]) + shift_ref[0]
    h = h32.astype(BF16)

    def proj(lo, width):
        return _dot(h, w_ref[:, lo:lo + width])

    off_z, off_xbc, off_k, off_za = 0, d_ssd, d_ssd + d_xbc, d_ssd + d_xbc + d_attn

    u = proj(off_xbc, d_xbc)
    pad = V7X_SUBLANES
    u_ext = jnp.concatenate([carry_ref[...], u], axis=0)
    carry_ref[...] = u[tm - pad:tm, :]
    cw = convw_ref[...]
    acc = convb_ref[...] + cw[CONV_WIDTH - 1:CONV_WIDTH, :] * u
    for tap in range(CONV_WIDTH - 1):
        shift = CONV_WIDTH - 1 - tap
        acc = acc + cw[tap:tap + 1, :] * pltpu.roll(u_ext, shift, 0)[pad:pad + tm, :]
    xbc = _silu(acc)
    xs_ref[0] = xbc[:, :d_ssd].astype(BF16)
    bm_ref[0] = xbc[:, d_ssd:d_ssd + n_bc].astype(BF16)
    cm_ref[0] = xbc[:, d_ssd + n_bc:d_ssd + 2 * n_bc].astype(BF16)

    zs_ref[0] = _silu(proj(off_z, d_ssd)).astype(BF16)
    k_ref[0] = proj(off_k, d_attn).astype(BF16)
    za_ref[0] = _silu(proj(off_za, d_attn)).astype(BF16)

    qvT = _dot_nt(wqvT_ref[...], h)
    nh_attn = d_attn // ATTN_V_DIM
    qT = qvT[:d_attn] * (ATTN_QK_DIM ** -0.5 * LOG2_E)
    qT_ref[0, :, 0] = qT.astype(BF16).reshape(nh_attn, ATTN_V_DIM, tm)
    vT_ref[0, :, 0] = qvT[d_attn:].astype(BF16).reshape(nh_attn, ATTN_V_DIM, tm)

    nh = dt_ref.shape[-1]
    h_lo = (h32 - h.astype(F32)).astype(BF16)
    w_hi, w_lo = _split_bf16(wdt_ref[...], 2)
    wlane = lax.broadcasted_iota(jnp.int32, (1, V7X_LANES), 1)
    r1 = _dot(h, jnp.where(wlane < nh, w_hi, w_lo))
    r2 = _dot(h_lo, w_hi)
    dt_raw = r1 + pltpu.roll(r1, V7X_LANES - nh, 1) + r2
    dt = _softplus(dt_raw + dtb_ref[...])
    dt_ref[0] = dt[:, :nh]
    dtT_ref[0] = dt.T[:nh, :]


def _in_projection(x, scale, shift, w_main, w_qv_t, w_dt, conv_w, conv_b, dt_bias, *, d_ssd, d_xbc,
                   d_attn):
    bsz, seq, d = x.shape
    tm = TM_INPROJ
    nh = w_dt.shape[1]
    nh_attn = d_attn // ATTN_V_DIM
    n_bc = (d_xbc - d_ssd) // 2
    n_main = w_main.shape[1]
    grid = (bsz, seq // tm)

    row = lambda b, l: (b, l, 0)
    head_major = lambda b, l: (b, 0, l, 0, 0)
    const2 = lambda b, l: (0, 0)
    per_b = lambda b, l: (b, 0, 0)
    single = pl.Buffered(1)
    in_specs = [
        pl.BlockSpec((1, tm, d), row),
        pl.BlockSpec((1, 1, d), per_b),
        pl.BlockSpec((1, 1, d), per_b),
        pl.BlockSpec((d, n_main), const2, pipeline_mode=single),
        pl.BlockSpec((2 * d_attn, d), const2, pipeline_mode=single),
        pl.BlockSpec((d, V7X_LANES), const2),
        pl.BlockSpec((CONV_WIDTH, d_xbc), const2),
        pl.BlockSpec((1, d_xbc), const2),
        pl.BlockSpec((1, V7X_LANES), const2),
    ]
    out_specs = [
        pl.BlockSpec((1, tm, d_ssd), row),
        pl.BlockSpec((1, tm, d_ssd), row),
        pl.BlockSpec((1, tm, n_bc), row),
        pl.BlockSpec((1, tm, n_bc), row),
        pl.BlockSpec((1, nh_attn, 1, ATTN_V_DIM, tm), head_major),
        pl.BlockSpec((1, tm, d_attn), row),
        pl.BlockSpec((1, nh_attn, 1, ATTN_V_DIM, tm), head_major),
        pl.BlockSpec((1, tm, d_attn), row),
        pl.BlockSpec((1, tm, nh), row),
        pl.BlockSpec((1, nh, tm), lambda b, l: (b, 0, l)),
    ]
    out_shape = [
        jax.ShapeDtypeStruct((bsz, seq, d_ssd), BF16),
        jax.ShapeDtypeStruct((bsz, seq, d_ssd), BF16),
        jax.ShapeDtypeStruct((bsz, seq, n_bc), BF16),
        jax.ShapeDtypeStruct((bsz, seq, n_bc), BF16),
        jax.ShapeDtypeStruct((bsz, nh_attn, seq // tm, ATTN_V_DIM, tm), BF16),
        jax.ShapeDtypeStruct((bsz, seq, d_attn), BF16),
        jax.ShapeDtypeStruct((bsz, nh_attn, seq // tm, ATTN_V_DIM, tm), BF16),
        jax.ShapeDtypeStruct((bsz, seq, d_attn), BF16),
        jax.ShapeDtypeStruct((bsz, seq, nh), F32),
        jax.ShapeDtypeStruct((bsz, nh, seq), F32),
    ]
    pad = V7X_SUBLANES
    vmem = (d * (n_main + 2 * d_attn) * 2 + 2 * tm * d * 4
            + 2 * tm * (2 * d_ssd + 2 * n_bc + 4 * d_attn) * 2
            + (tm + 2 * pad) * d_xbc * 4 + 6 * tm * d_xbc * 4)
    kern = functools.partial(_inproj_kernel, tm=tm, d_ssd=d_ssd, d_xbc=d_xbc, d_attn=d_attn, n_bc=n_bc)
    assert 2 * nh <= V7X_LANES
    w_dt_pad = jnp.zeros((d, V7X_LANES), F32).at[:, :nh].set(w_dt).at[:, nh:2 * nh].set(w_dt)
    dt_bias_pad = jnp.zeros((1, V7X_LANES), F32).at[0, :nh].set(dt_bias)
    return pl.pallas_call(
        kern, grid=grid, in_specs=in_specs, out_specs=out_specs, out_shape=out_shape,
        scratch_shapes=[pltpu.VMEM((pad, d_xbc), F32)],
        compiler_params=pltpu.CompilerParams(
            dimension_semantics=("arbitrary", "arbitrary"), vmem_limit_bytes=_vmem_limit(vmem)),
        name="in_projection",
    )(x, scale, shift, w_main, w_qv_t, w_dt_pad, conv_w, conv_b.reshape(1, -1), dt_bias_pad)


def _ssd_kernel(xs_ref, bm_ref, cm_ref, zs_ref, dt_ref, dtT_ref, alog_ref, alogT_ref, dskip_ref,
                normw_ref, y_ref, state_ref, ybuf_ref, *, nheads, d_ssd, cps):
    t = CHUNK
    n = SSD_STATE
    hp = SSD_HEAD_DIM
    gw = d_ssd // SSD_GROUPS
    heads_per_group = nheads // SSD_GROUPS

    @pl.when(pl.program_id(1) == 0)
    def _():
        state_ref[...] = jnp.zeros_like(state_ref)

    row = lax.broadcasted_iota(jnp.int32, (t, t), 0)
    col = lax.broadcasted_iota(jnp.int32, (t, t), 1)
    causal = col <= row
    tri = jnp.where(causal, 1.0, 0.0).astype(BF16)
    tri_t = jnp.where(row <= col, 1.0, 0.0).astype(BF16)
    eh = lax.broadcasted_iota(jnp.int32, (nheads, d_ssd), 0)
    ej = lax.broadcasted_iota(jnp.int32, (nheads, d_ssd), 1)
    expand_m = jnp.where(ej // hp == eh, 1.0, 0.0).astype(BF16)
    lane = lax.broadcasted_iota(jnp.int32, (1, 2 * hp), 1)
    neg_inf = jnp.float32(-jnp.inf)
    a = -jnp.exp(alog_ref[...])
    a_t = -jnp.exp(alogT_ref[...])

    def expand(v, pieces):
        return sum(_dot(p, expand_m) for p in _split_bf16(v, pieces))

    for ci in range(cps):
        rows = slice(ci * t, (ci + 1) * t)
        dt = dt_ref[0, rows, :]
        dt_t = dtT_ref[0, :, rows]
        acum = sum(_dot(tri, p) for p in _split_bf16(dt * a, 3))
        acum_t = sum(_dot(p, tri_t) for p in _split_bf16(dt_t * a_t, 3))
        last = acum[t - 1:t, :]
        dt_x = expand(dt, 1)
        w_x = expand(jnp.exp(last - acum), 1)
        e_x = expand(jnp.exp(acum), 2)

        xs = xs_ref[0, rows, :].astype(F32)
        xdt = xs * dt_x
        xdt_b = xdt.astype(BF16)
        xw_b = (xdt * w_x).astype(BF16)

        for g in range(SSD_GROUPS):
            b_g = bm_ref[0, rows, g * n:(g + 1) * n]
            c_g = cm_ref[0, rows, g * n:(g + 1) * n]
            gsl = slice(g * gw, (g + 1) * gw)
            cb = _dot_nt(c_g, b_g)
            prev = state_ref[g]
            y_off = _dot(c_g, prev.astype(BF16)) * e_x[:, gsl]
            state_ref[g] = prev * e_x[t - 1:t, gsl] + _dot_tn(b_g, xw_b[:, gsl])
            for pair in range(heads_per_group // 2):
                lo = g * gw + pair * 2 * hp
                x_pair = xdt_b[:, lo:lo + 2 * hp]
                decays = []
                for half in range(2):
                    hd = g * heads_per_group + pair * 2 + half
                    diff = acum[:, hd:hd + 1] - acum_t[hd:hd + 1, :]
                    decays.append((cb * jnp.exp(jnp.where(causal, diff, neg_inf))).astype(BF16))
                zero = jnp.zeros_like(x_pair)
                x_diag = jnp.concatenate([jnp.where(lane < hp, x_pair, zero),
                                          jnp.where(lane >= hp, x_pair, zero)], axis=0)
                y_pair = (y_off[:, pair * 2 * hp:(pair + 1) * 2 * hp]
                          + _dot(jnp.concatenate(decays, axis=1), x_diag))
                ybuf_ref[rows, lo:lo + 2 * hp] = y_pair

    xs = xs_ref[0].astype(F32)
    y = ybuf_ref[...] + dskip_ref[...] * xs
    y = y * zs_ref[0].astype(F32)
    for g in range(SSD_GROUPS):
        gsl = slice(g * gw, (g + 1) * gw)
        y_g = y[:, gsl]
        ms = jnp.mean(y_g * y_g, axis=-1, keepdims=True)
        y_ref[0, :, gsl] = (y_g * lax.rsqrt(ms + EPS) * normw_ref[:, gsl]).astype(BF16)


def _ssd(xs, bm, cm, zs, dt, dt_t, a_log, d_skip, ssd_norm_w):
    bsz, seq, d_ssd = xs.shape
    nheads = dt.shape[-1]
    n_bc = bm.shape[-1]
    cps = SSD_CHUNKS_PER_STEP
    t = cps * CHUNK
    assert seq % t == 0
    grid = (bsz, seq // t)
    row = lambda b, c: (b, c, 0)
    const2 = lambda b, c: (0, 0)
    in_specs = [
        pl.BlockSpec((1, t, d_ssd), row),
        pl.BlockSpec((1, t, n_bc), row),
        pl.BlockSpec((1, t, n_bc), row),
        pl.BlockSpec((1, t, d_ssd), row),
        pl.BlockSpec((1, t, nheads), row),
        pl.BlockSpec((1, nheads, t), lambda b, c: (b, 0, c)),
        pl.BlockSpec((1, nheads), const2),
        pl.BlockSpec((nheads, 1), const2),
        pl.BlockSpec((1, d_ssd), const2),
        pl.BlockSpec((1, d_ssd), const2),
    ]
    kern = functools.partial(_ssd_kernel, nheads=nheads, d_ssd=d_ssd, cps=cps)
    return pl.pallas_call(
        kern, grid=grid, in_specs=in_specs,
        out_specs=pl.BlockSpec((1, t, d_ssd), row),
        out_shape=jax.ShapeDtypeStruct((bsz, seq, d_ssd), BF16),
        scratch_shapes=[pltpu.VMEM((SSD_GROUPS, SSD_STATE, d_ssd // SSD_GROUPS), F32),
                        pltpu.VMEM((t, d_ssd), F32)],
        compiler_params=pltpu.CompilerParams(dimension_semantics=("arbitrary", "arbitrary")),
        name="ssd_scan",
    )(xs, bm, cm, zs, dt, dt_t, a_log.reshape(1, -1), a_log.reshape(-1, 1),
      jnp.repeat(d_skip, SSD_HEAD_DIM).reshape(1, -1), ssd_norm_w.reshape(1, -1))


def _attn_kernel(qT_ref, k_ref, vT_ref, za_ref, lq1_ref, lk1_ref, lq2_ref, lk2_ref, normw_ref,
                 o_ref, qq_ref, s_ref, cmax_ref, m_ref, acc_ref, *, tk, nsub):
    qi = pl.program_id(2)
    dk = ATTN_QK_DIM
    dv = 2 * dk
    sw = 2 * tk
    feat = lax.broadcasted_iota(jnp.int32, (2 * dk, 1), 0)
    for sub in range(nsub):
        qT = qT_ref[0, 0, sub]
        zero = jnp.zeros_like(qT)
        qq_ref[:, sub * sw:sub * sw + tk] = jnp.where(feat < dk, qT, zero)
        qq_ref[:, sub * sw + tk:(sub + 1) * sw] = jnp.where(feat >= dk, qT, zero)

    m_ref[...] = jnp.full_like(m_ref, -jnp.inf)
    acc_ref[...] = jnp.zeros_like(acc_ref)
    ones_rows = jnp.ones((BF16_SUBLANE_PACK, tk), BF16)

    nchunk = 2 * nsub

    def col(c):
        return slice(c * tk, (c + 1) * tk)

    def produce(slot, kv, c):
        start = pl.multiple_of(kv * tk, tk)
        s = _dot(k_ref[0, pl.ds(start, tk), :], qq_ref[:, col(c)])
        s_ref[slot, :, col(c)] = s
        cmax_ref[slot, :, col(c)] = jnp.max(s, axis=0, keepdims=True)

    def consume(slot, kv, c, diagonal=False):
        s = s_ref[slot, :, col(c)]
        if diagonal:
            r = lax.broadcasted_iota(jnp.int32, s.shape, 0)
            qpos = lax.broadcasted_iota(jnp.int32, s.shape, 1)
            s = jnp.where(r <= qpos, s, -jnp.inf)
            cmax = jnp.max(s, axis=0, keepdims=True)
        else:
            cmax = cmax_ref[slot, :, col(c)]
        m_prev = m_ref[:, col(c)]
        m_new = jnp.maximum(m_prev, cmax)
        alpha = jnp.exp2(m_prev - m_new)
        p = jnp.exp2(s - m_new).astype(BF16)
        v_ext = jnp.concatenate([vT_ref[0, 0, kv], ones_rows], axis=0)
        acc_ref[:, col(c)] = alpha * acc_ref[:, col(c)] + _dot(v_ext, p)
        m_ref[:, col(c)] = m_new

    lead = ATTN_PRODUCE_LEAD
    assert lead <= nchunk

    for c in range(lead):
        produce(0, 0, c)

    def pair(i, carry):
        kv = 2 * i
        for idx in range(2 * nchunk):
            ahead = idx + lead
            produce((ahead // nchunk) % 2, kv + ahead // nchunk, ahead % nchunk)
            consume((idx // nchunk) % 2, kv + idx // nchunk, idx % nchunk)
        return carry

    lax.fori_loop(0, (nsub // 2) * qi, pair, 0)

    kv0 = nsub * qi
    units = [(d, c) for d in range(nsub) for c in range(nchunk) if c // 2 >= d]
    assert [u for u in units[:lead]] == [(0, c) for c in range(lead)]
    lam = (jnp.exp(jnp.sum(lq1_ref[...] * lk1_ref[...], axis=-1, keepdims=True))
           - jnp.exp(jnp.sum(lq2_ref[...] * lk2_ref[...], axis=-1, keepdims=True))
           + LAMBDA_INIT_L0)

    def finalize(sub):
        c1 = sub * sw
        c2 = c1 + tk
        o1 = acc_ref[0:dv, c1:c1 + tk] / acc_ref[dv:dv + 1, c1:c1 + tk]
        o2 = acc_ref[0:dv, c2:c2 + tk] / acc_ref[dv:dv + 1, c2:c2 + tk]
        o = (o1 - lam * o2).T
        ms = jnp.mean(o * o, axis=-1, keepdims=True)
        o = o * lax.rsqrt(ms + EPS) * normw_ref[...] * (1.0 - LAMBDA_INIT_L0)
        rows = slice(sub * tk, (sub + 1) * tk)
        o_ref[0, rows, :] = (o * za_ref[0, rows, :].astype(F32)).astype(BF16)

    for i, (d, c) in enumerate(units):
        if i + lead < len(units):
            nd, nc = units[i + lead]
            produce(nd % 2, kv0 + nd, nc)
        consume(d % 2, kv0 + d, c, diagonal=(c // 2 == d))
        if c % 2 == 1 and c // 2 == d:
            finalize(d)


def _attention(q_t, k, v_t, za, lq1, lk1, lq2, lk2, attn_norm_w):
    bsz, seq, d_attn = k.shape
    hw = ATTN_V_DIM
    nh = d_attn // hw
    tk = TK_ATTN
    nsub = NSUB_ATTN
    tq = nsub * tk
    nblk = seq // tk
    assert nsub % 2 == 0 and seq % tq == 0
    assert q_t.shape == (bsz, nh, nblk, hw, tk) and v_t.shape == q_t.shape
    grid = (bsz, nh, seq // tq)
    qrow = lambda b, h, i: (b, i, h)
    const2 = lambda b, h, i: (0, 0)
    vec = pl.BlockSpec((1, ATTN_QK_DIM), const2)
    in_specs = [
        pl.BlockSpec((1, 1, nsub, hw, tk), lambda b, h, i: (b, h, i, 0, 0)),
        pl.BlockSpec((1, seq, hw), lambda b, h, i: (b, 0, h)),
        pl.BlockSpec((1, 1, nblk, hw, tk), lambda b, h, i: (b, h, 0, 0, 0)),
        pl.BlockSpec((1, tq, hw), qrow),
        vec, vec, vec, vec,
        pl.BlockSpec((1, hw), const2),
    ]
    width = 2 * tq
    vmem = (2 * 2 * seq * hw * 2 + 2 * tk * width * 4 + (2 * hw + BF16_SUBLANE_PACK) * width * 4
            + 4 * tk * width * 4)
    return pl.pallas_call(
        functools.partial(_attn_kernel, tk=tk, nsub=nsub), grid=grid, in_specs=in_specs,
        out_specs=pl.BlockSpec((1, tq, hw), qrow),
        out_shape=jax.ShapeDtypeStruct((bsz, seq, d_attn), BF16),
        scratch_shapes=[pltpu.VMEM((hw, width), BF16), pltpu.VMEM((2, tk, width), F32),
                        pltpu.VMEM((2, 1, width), F32), pltpu.VMEM((1, width), F32),
                        pltpu.VMEM((hw + BF16_SUBLANE_PACK, width), F32)],
        compiler_params=pltpu.CompilerParams(
            dimension_semantics=("arbitrary", "arbitrary", "arbitrary"),
            vmem_limit_bytes=_vmem_limit(vmem)),
        name="diff_attention",
    )(q_t, k, v_t, za, lq1.reshape(1, -1), lk1.reshape(1, -1), lq2.reshape(1, -1),
      lk2.reshape(1, -1), attn_norm_w.reshape(1, -1))


def _outproj_kernel(ys_ref, ya_ref, w_ref, x_ref, gate_ref, g_ref, b_ref, o_ref, *, d_ssd, alpha):
    mixed = _dot(ys_ref[0], w_ref[0:d_ssd, :]) + _dot(ya_ref[0], w_ref[d_ssd:, :])
    xf = alpha * x_ref[0] + gate_ref[0] * mixed
    mu = jnp.mean(xf, axis=-1, keepdims=True)
    xc = xf - mu
    var = jnp.mean(xc * xc, axis=-1, keepdims=True)
    o_ref[0] = xc * lax.rsqrt(var + EPS) * g_ref[...] + b_ref[...]


def _out_projection(y_ssd, y_attn, w_out, x, gate, ln_g, ln_b, alpha):
    bsz, seq, d = x.shape
    d_ssd = y_ssd.shape[-1]
    d_attn = y_attn.shape[-1]
    tm = TM_OUT
    row = lambda b, l: (b, l, 0)
    const2 = lambda b, l: (0, 0)
    in_specs = [
        pl.BlockSpec((1, tm, d_ssd), row),
        pl.BlockSpec((1, tm, d_attn), row),
        pl.BlockSpec((d_ssd + d_attn, d), const2, pipeline_mode=pl.Buffered(1)),
        pl.BlockSpec((1, tm, d), row),
        pl.BlockSpec((1, 1, d), lambda b, l: (b, 0, 0)),
        pl.BlockSpec((1, d), const2),
        pl.BlockSpec((1, d), const2),
    ]
    vmem = (d_ssd + d_attn) * d * 2 + 2 * tm * (d_ssd + d_attn) * 2 + 4 * tm * d * 4 + 4 * tm * d * 4
    return pl.pallas_call(
        functools.partial(_outproj_kernel, d_ssd=d_ssd, alpha=alpha),
        grid=(bsz, seq // tm), in_specs=in_specs,
        out_specs=pl.BlockSpec((1, tm, d), row),
        out_shape=jax.ShapeDtypeStruct((bsz, seq, d), F32),
        compiler_params=pltpu.CompilerParams(
            dimension_semantics=("arbitrary", "arbitrary"), vmem_limit_bytes=_vmem_limit(vmem)),
        name="out_projection",
    )(y_ssd, y_attn, w_out, x, gate, ln_g.reshape(1, -1), ln_b.reshape(1, -1))


def _layer(x, c, w_ada, b_ada, w_in, conv_w, conv_b, dt_bias, a_log, d_skip, ssd_norm_w,
           lambda_q1, lambda_k1, lambda_q2, lambda_k2, attn_norm_w, w_out, ln_g, ln_b, depth):
    bsz, seq, d = x.shape
    nheads = a_log.shape[0]
    d_ssd = nheads * SSD_HEAD_DIM
    d_xbc = conv_w.shape[1]
    d_attn = w_out.shape[0] - d_ssd
    assert d_xbc == d_ssd + 2 * SSD_GROUPS * SSD_STATE
    assert w_in.shape[1] == d_ssd + d_xbc + nheads + 4 * d_attn
    assert seq % TM_INPROJ == 0 and seq % TM_OUT == 0 and seq % CHUNK == 0
    assert TM_INPROJ == TK_ATTN, "the in-projection writes q^T / v^T in attention-sized time blocks"

    mod = _modulation(c, w_ada, b_ada)
    shift = mod[:, None, 0:d]
    scale = mod[:, None, d:2 * d]
    gate = mod[:, None, 2 * d:3 * d]

    off_dt = d_ssd + d_xbc
    off_q = off_dt + nheads
    off_k, off_v, off_za = off_q + d_attn, off_q + 2 * d_attn, off_q + 3 * d_attn
    w_main = jnp.concatenate([w_in[:, :off_dt], w_in[:, off_k:off_v], w_in[:, off_za:]],
                             axis=1).astype(BF16)
    w_qv_t = jnp.concatenate([w_in[:, off_q:off_k], w_in[:, off_v:off_za]], axis=1).T.astype(BF16)
    w_dt = w_in[:, off_dt:off_q]

    zs, xs, bm, cm, q_t, k, v_t, za, dt, dt_t = _in_projection(
        x, scale, shift, w_main, w_qv_t, w_dt, conv_w, conv_b, dt_bias,
        d_ssd=d_ssd, d_xbc=d_xbc, d_attn=d_attn)
    y_ssd = _ssd(xs, bm, cm, zs, dt, dt_t, a_log, d_skip, ssd_norm_w)
    y_attn = _attention(q_t, k, v_t, za, lambda_q1, lambda_k1, lambda_q2, lambda_k2, attn_norm_w)
    alpha = (2.0 * depth) ** 0.25
    return _out_projection(y_ssd, y_attn, w_out.astype(BF16), x, gate, ln_g, ln_b, alpha)


def kernel(x, c, w_ada, b_ada, w_in, conv_w, conv_b, dt_bias, a_log, d_skip, ssd_norm_w, lambda_q1,
           lambda_k1, lambda_q2, lambda_k2, attn_norm_w, w_out, ln_g, ln_b):
    depth = w_in.shape[0]
    assert depth == 1, "lambda_init and the layer loop are specialised to a single layer"
    return _layer(x, c, w_ada[0], b_ada[0], w_in[0], conv_w[0], conv_b[0], dt_bias[0], a_log[0],
                  d_skip[0], ssd_norm_w[0], lambda_q1[0], lambda_k1[0], lambda_q2[0], lambda_k2[0],
                  attn_norm_w[0], w_out[0], ln_g[0], ln_b[0], depth)
```

```python
import functools
import math

import jax
import jax.numpy as jnp
from jax import lax
from jax.experimental import pallas as pl
from jax.experimental.pallas import tpu as pltpu

F32 = jnp.float32
BF16 = jnp.bfloat16

SSD_HEAD_DIM = 64
SSD_GROUPS = 2
SSD_STATE = 128
CONV_WIDTH = 4
CHUNK = 128
ATTN_QK_DIM = 64
ATTN_V_DIM = 2 * ATTN_QK_DIM
EPS = 1e-5
LAMBDA_INIT_L0 = 0.8 - 0.6 * math.exp(-0.3 * 0)
LOG2_E = math.log2(math.e)

V7X_LANES = 128
V7X_SUBLANES = 8
BF16_SUBLANE_PACK = 16
V7X_VMEM_BYTES = 64 * 1024 * 1024
V7X_VMEM_USABLE_BYTES = 56 * 1024 * 1024

TM_INPROJ = 512
TM_INPROJ_PASS = 256
TK_ATTN = 512
NSUB_ATTN = 2
ATTN_PRODUCE_LEAD = 2
ATTN_BLOCKS_PER_TRIP = 4
TM_OUT = 1024
TM_OUT_PASS = 256
SSD_CHUNKS_PER_STEP = 4


def _vmem_limit(estimate_bytes):
    return int(min(V7X_VMEM_USABLE_BYTES, max(32 * 1024 * 1024, estimate_bytes)))


def _silu(x):
    hx = 0.5 * x
    return hx + hx * jnp.tanh(hx)


def _softplus(x):
    return jnp.maximum(x, 0.0) + jnp.log(1.0 + jnp.exp(-jnp.abs(x)))


def _split_bf16(a, pieces):
    out = []
    r = a
    for i in range(pieces):
        p = r.astype(BF16)
        out.append(p)
        if i + 1 < pieces:
            r = r - p.astype(F32)
    return out


def _dot(a, b):
    return jnp.dot(a, b, preferred_element_type=F32)


def _dot_nt(a, b):
    return lax.dot_general(a, b, (((1,), (1,)), ((), ())), preferred_element_type=F32)


def _dot_tn(a, b):
    return lax.dot_general(a, b, (((0,), (0,)), ((), ())), preferred_element_type=F32)


def _mod_kernel(c_ref, w_ref, b_ref, o_ref):
    c_hi, c_lo = _split_bf16(c_ref[...], 2)
    w_hi, w_lo = _split_bf16(w_ref[...], 2)
    o_ref[...] = _dot(c_hi, w_hi) + _dot(c_hi, w_lo) + _dot(c_lo, w_hi) + b_ref[...]


def _modulation(c, w_ada, b_ada):
    bsz, d = c.shape
    n = w_ada.shape[1]
    rows = V7X_SUBLANES
    c_pad = jnp.zeros((rows, d), F32).at[:bsz].set(c)
    bn = d
    out = pl.pallas_call(
        _mod_kernel,
        grid=(n // bn,),
        in_specs=[pl.BlockSpec((rows, d), lambda j: (0, 0)),
                  pl.BlockSpec((d, bn), lambda j: (0, j)),
                  pl.BlockSpec((1, bn), lambda j: (0, j))],
        out_specs=pl.BlockSpec((rows, bn), lambda j: (0, j)),
        out_shape=jax.ShapeDtypeStruct((rows, n), F32),
        name="modulation",
    )(c_pad, w_ada, b_ada.reshape(1, n))
    return out[:bsz]


def _inproj_kernel(x_ref, scale_ref, shift_ref, w_ref, wqvT_ref, wdt_ref, convw_ref,
                   convb_ref, dtb_ref,
                   zs_ref, xs_ref, bm_ref, cm_ref, qT_ref, k_ref, vT_ref, za_ref, dt_ref, dtT_ref,
                   carry_ref, *, tm, rows_per_pass, d_ssd, d_xbc, d_attn, n_bc):
    @pl.when(pl.program_id(1) == 0)
    def _():
        carry_ref[...] = jnp.zeros_like(carry_ref)

    off_z, off_xbc, off_k, off_za = 0, d_ssd, d_ssd + d_xbc, d_ssd + d_xbc + d_attn
    pad = V7X_SUBLANES
    nh = dt_ref.shape[-1]
    nh_attn = d_attn // ATTN_V_DIM
    cw = convw_ref[...]
    w_hi, w_lo = _split_bf16(wdt_ref[...], 2)
    wlane = lax.broadcasted_iota(jnp.int32, (1, V7X_LANES), 1)
    w_hilo = jnp.where(wlane < nh, w_hi, w_lo)

    tail = carry_ref[...]
    for r0 in range(0, tm, rows_per_pass):
        rows = slice(r0, r0 + rows_per_pass)
        h32 = x_ref[0, rows, :] * (1.0 + scale_ref[0]) + shift_ref[0]
        h = h32.astype(BF16)

        def proj(lo, width):
            return _dot(h, w_ref[:, lo:lo + width])

        u = proj(off_xbc, d_xbc)
        u_ext = jnp.concatenate([tail, u], axis=0)
        tail = u[rows_per_pass - pad:rows_per_pass, :]
        acc = convb_ref[...] + cw[CONV_WIDTH - 1:CONV_WIDTH, :] * u
        for tap in range(CONV_WIDTH - 1):
            shift = CONV_WIDTH - 1 - tap
            acc = acc + (cw[tap:tap + 1, :]
                         * pltpu.roll(u_ext, shift, 0)[pad:pad + rows_per_pass, :])
        xbc = _silu(acc)
        xs_ref[0, rows, :] = xbc[:, :d_ssd].astype(BF16)
        bm_ref[0, rows, :] = xbc[:, d_ssd:d_ssd + n_bc].astype(BF16)
        cm_ref[0, rows, :] = xbc[:, d_ssd + n_bc:d_ssd + 2 * n_bc].astype(BF16)

        zs_ref[0, rows, :] = _silu(proj(off_z, d_ssd)).astype(BF16)
        k_ref[0, rows, :] = proj(off_k, d_attn).astype(BF16)
        za_ref[0, rows, :] = _silu(proj(off_za, d_attn)).astype(BF16)

        qvT = _dot_nt(wqvT_ref[...], h)
        qT = qvT[:d_attn] * (ATTN_QK_DIM ** -0.5 * LOG2_E)
        qT = qT.astype(BF16).reshape(nh_attn, ATTN_V_DIM, rows_per_pass)
        feat = lax.broadcasted_iota(jnp.int32, (1, ATTN_V_DIM, 1), 1)
        zero = jnp.zeros_like(qT)
        qT_ref[0, :, 0, :, rows] = jnp.where(feat < ATTN_QK_DIM, qT, zero)
        qT_ref[0, :, 0, :, tm + r0:tm + r0 + rows_per_pass] = jnp.where(feat >= ATTN_QK_DIM, qT, zero)
        vT_ref[0, :, 0, :, rows] = qvT[d_attn:].astype(BF16).reshape(nh_attn, ATTN_V_DIM,
                                                                      rows_per_pass)

        h_lo = (h32 - h.astype(F32)).astype(BF16)
        r1 = _dot(h, w_hilo)
        r2 = _dot(h_lo, w_hi)
        dt_raw = r1 + pltpu.roll(r1, V7X_LANES - nh, 1) + r2
        dt = _softplus(dt_raw + dtb_ref[...])
        dt_ref[0, rows, :] = dt[:, :nh]
        dtT_ref[0, :, rows] = dt.T[:nh, :]
    carry_ref[...] = tail


def _in_projection(x, scale, shift, w_main, w_qv_t, w_dt, conv_w, conv_b, dt_bias, *, d_ssd, d_xbc,
                   d_attn):
    bsz, seq, d = x.shape
    tm = TM_INPROJ
    nh = w_dt.shape[1]
    nh_attn = d_attn // ATTN_V_DIM
    n_bc = (d_xbc - d_ssd) // 2
    n_main = w_main.shape[1]
    grid = (bsz, seq // tm)

    row = lambda b, l: (b, l, 0)
    head_major = lambda b, l: (b, 0, l, 0, 0)
    const2 = lambda b, l: (0, 0)
    per_b = lambda b, l: (b, 0, 0)
    single = pl.Buffered(1)
    in_specs = [
        pl.BlockSpec((1, tm, d), row),
        pl.BlockSpec((1, 1, d), per_b),
        pl.BlockSpec((1, 1, d), per_b),
        pl.BlockSpec((d, n_main), const2, pipeline_mode=single),
        pl.BlockSpec((2 * d_attn, d), const2, pipeline_mode=single),
        pl.BlockSpec((d, V7X_LANES), const2),
        pl.BlockSpec((CONV_WIDTH, d_xbc), const2),
        pl.BlockSpec((1, d_xbc), const2),
        pl.BlockSpec((1, V7X_LANES), const2),
    ]
    out_specs = [
        pl.BlockSpec((1, tm, d_ssd), row),
        pl.BlockSpec((1, tm, d_ssd), row),
        pl.BlockSpec((1, tm, n_bc), row),
        pl.BlockSpec((1, tm, n_bc), row),
        pl.BlockSpec((1, nh_attn, 1, ATTN_V_DIM, 2 * tm), head_major),
        pl.BlockSpec((1, tm, d_attn), row),
        pl.BlockSpec((1, nh_attn, 1, ATTN_V_DIM, tm), head_major),
        pl.BlockSpec((1, tm, d_attn), row),
        pl.BlockSpec((1, tm, nh), row),
        pl.BlockSpec((1, nh, tm), lambda b, l: (b, 0, l)),
    ]
    out_shape = [
        jax.ShapeDtypeStruct((bsz, seq, d_ssd), BF16),
        jax.ShapeDtypeStruct((bsz, seq, d_ssd), BF16),
        jax.ShapeDtypeStruct((bsz, seq, n_bc), BF16),
        jax.ShapeDtypeStruct((bsz, seq, n_bc), BF16),
        jax.ShapeDtypeStruct((bsz, nh_attn, seq // tm, ATTN_V_DIM, 2 * tm), BF16),
        jax.ShapeDtypeStruct((bsz, seq, d_attn), BF16),
        jax.ShapeDtypeStruct((bsz, nh_attn, seq // tm, ATTN_V_DIM, tm), BF16),
        jax.ShapeDtypeStruct((bsz, seq, d_attn), BF16),
        jax.ShapeDtypeStruct((bsz, seq, nh), F32),
        jax.ShapeDtypeStruct((bsz, nh, seq), F32),
    ]
    pad = V7X_SUBLANES
    vmem = (d * (n_main + 2 * d_attn) * 2 + 2 * tm * d * 4
            + 2 * tm * (2 * d_ssd + 2 * n_bc + 4 * d_attn) * 2
            + (tm + 2 * pad) * d_xbc * 4 + 6 * tm * d_xbc * 4)
    kern = functools.partial(_inproj_kernel, tm=tm, rows_per_pass=TM_INPROJ_PASS, d_ssd=d_ssd,
                             d_xbc=d_xbc, d_attn=d_attn, n_bc=n_bc)
    assert 2 * nh <= V7X_LANES
    w_dt_pad = jnp.zeros((d, V7X_LANES), F32).at[:, :nh].set(w_dt).at[:, nh:2 * nh].set(w_dt)
    dt_bias_pad = jnp.zeros((1, V7X_LANES), F32).at[0, :nh].set(dt_bias)
    return pl.pallas_call(
        kern, grid=grid, in_specs=in_specs, out_specs=out_specs, out_shape=out_shape,
        scratch_shapes=[pltpu.VMEM((pad, d_xbc), F32)],
        compiler_params=pltpu.CompilerParams(
            dimension_semantics=("arbitrary", "arbitrary"), vmem_limit_bytes=_vmem_limit(vmem)),
        name="in_projection",
    )(x, scale, shift, w_main, w_qv_t, w_dt_pad, conv_w, conv_b.reshape(1, -1), dt_bias_pad)


def _ssd_kernel(xs_ref, bm_ref, cm_ref, zs_ref, dt_ref, dtT_ref, alog_ref, alogT_ref, dskip_ref,
                normw_ref, y_ref, state_ref, ybuf_ref, *, nheads, d_ssd, cps):
    t = CHUNK
    n = SSD_STATE
    hp = SSD_HEAD_DIM
    gw = d_ssd // SSD_GROUPS
    heads_per_group = nheads // SSD_GROUPS

    @pl.when(pl.program_id(1) == 0)
    def _():
        state_ref[...] = jnp.zeros_like(state_ref)

    row = lax.broadcasted_iota(jnp.int32, (t, t), 0)
    col = lax.broadcasted_iota(jnp.int32, (t, t), 1)
    causal = col <= row
    tri = jnp.where(causal, 1.0, 0.0).astype(BF16)
    tri_t = jnp.where(row <= col, 1.0, 0.0).astype(BF16)
    eh = lax.broadcasted_iota(jnp.int32, (nheads, d_ssd), 0)
    ej = lax.broadcasted_iota(jnp.int32, (nheads, d_ssd), 1)
    expand_m = jnp.where(ej // hp == eh, 1.0, 0.0).astype(BF16)
    lane = lax.broadcasted_iota(jnp.int32, (1, 2 * hp), 1)
    neg_inf = jnp.float32(-jnp.inf)
    a = -jnp.exp(alog_ref[...])
    a_t = -jnp.exp(alogT_ref[...])

    def expand(v, pieces):
        return sum(_dot(p, expand_m) for p in _split_bf16(v, pieces))

    for ci in range(cps):
        rows = slice(ci * t, (ci + 1) * t)
        dt = dt_ref[0, rows, :]
        dt_t = dtT_ref[0, :, rows]
        acum = sum(_dot(tri, p) for p in _split_bf16(dt * a, 3))
        acum_t = sum(_dot(p, tri_t) for p in _split_bf16(dt_t * a_t, 3))
        last = acum[t - 1:t, :]
        dt_x = expand(dt, 1)
        w_x = expand(jnp.exp(last - acum), 1)
        e_x = expand(jnp.exp(acum), 2)

        xs = xs_ref[0, rows, :].astype(F32)
        xdt = xs * dt_x
        xdt_b = xdt.astype(BF16)
        xw_b = (xdt * w_x).astype(BF16)

        for g in range(SSD_GROUPS):
            b_g = bm_ref[0, rows, g * n:(g + 1) * n]
            c_g = cm_ref[0, rows, g * n:(g + 1) * n]
            gsl = slice(g * gw, (g + 1) * gw)
            cb = _dot_nt(c_g, b_g)
            prev = state_ref[g]
            y_off = _dot(c_g, prev.astype(BF16)) * e_x[:, gsl]
            state_ref[g] = prev * e_x[t - 1:t, gsl] + _dot_tn(b_g, xw_b[:, gsl])
            for pair in range(heads_per_group // 2):
                lo = g * gw + pair * 2 * hp
                x_pair = xdt_b[:, lo:lo + 2 * hp]
                decays = []
                for half in range(2):
                    hd = g * heads_per_group + pair * 2 + half
                    diff = acum[:, hd:hd + 1] - acum_t[hd:hd + 1, :]
                    decays.append((cb * jnp.exp(jnp.where(causal, diff, neg_inf))).astype(BF16))
                zero = jnp.zeros_like(x_pair)
                x_diag = jnp.concatenate([jnp.where(lane < hp, x_pair, zero),
                                          jnp.where(lane >= hp, x_pair, zero)], axis=0)
                y_pair = (y_off[:, pair * 2 * hp:(pair + 1) * 2 * hp]
                          + _dot(jnp.concatenate(decays, axis=1), x_diag))
                ybuf_ref[rows, lo:lo + 2 * hp] = y_pair

    xs = xs_ref[0].astype(F32)
    y = ybuf_ref[...] + dskip_ref[...] * xs
    y = y * zs_ref[0].astype(F32)
    for g in range(SSD_GROUPS):
        gsl = slice(g * gw, (g + 1) * gw)
        y_g = y[:, gsl]
        ms = jnp.mean(y_g * y_g, axis=-1, keepdims=True)
        y_ref[0, :, gsl] = (y_g * lax.rsqrt(ms + EPS) * normw_ref[:, gsl]).astype(BF16)


def _ssd(xs, bm, cm, zs, dt, dt_t, a_log, d_skip, ssd_norm_w):
    bsz, seq, d_ssd = xs.shape
    nheads = dt.shape[-1]
    n_bc = bm.shape[-1]
    cps = SSD_CHUNKS_PER_STEP
    t = cps * CHUNK
    assert seq % t == 0
    grid = (bsz, seq // t)
    row = lambda b, c: (b, c, 0)
    const2 = lambda b, c: (0, 0)
    in_specs = [
        pl.BlockSpec((1, t, d_ssd), row),
        pl.BlockSpec((1, t, n_bc), row),
        pl.BlockSpec((1, t, n_bc), row),
        pl.BlockSpec((1, t, d_ssd), row),
        pl.BlockSpec((1, t, nheads), row),
        pl.BlockSpec((1, nheads, t), lambda b, c: (b, 0, c)),
        pl.BlockSpec((1, nheads), const2),
        pl.BlockSpec((nheads, 1), const2),
        pl.BlockSpec((1, d_ssd), const2),
        pl.BlockSpec((1, d_ssd), const2),
    ]
    kern = functools.partial(_ssd_kernel, nheads=nheads, d_ssd=d_ssd, cps=cps)
    return pl.pallas_call(
        kern, grid=grid, in_specs=in_specs,
        out_specs=pl.BlockSpec((1, t, d_ssd), row),
        out_shape=jax.ShapeDtypeStruct((bsz, seq, d_ssd), BF16),
        scratch_shapes=[pltpu.VMEM((SSD_GROUPS, SSD_STATE, d_ssd // SSD_GROUPS), F32),
                        pltpu.VMEM((t, d_ssd), F32)],
        compiler_params=pltpu.CompilerParams(dimension_semantics=("arbitrary", "arbitrary")),
        name="ssd_scan",
    )(xs, bm, cm, zs, dt, dt_t, a_log.reshape(1, -1), a_log.reshape(-1, 1),
      jnp.repeat(d_skip, SSD_HEAD_DIM).reshape(1, -1), ssd_norm_w.reshape(1, -1))


def _attn_kernel(qT_ref, k_ref, vT_ref, za_ref, lq1_ref, lk1_ref, lq2_ref, lk2_ref, normw_ref,
                 o_ref, s_ref, cmax_ref, m_ref, acc_ref, *, tk, nsub):
    qi = pl.program_id(2)
    dk = ATTN_QK_DIM
    dv = 2 * dk
    sw = 2 * tk

    m_ref[...] = jnp.full_like(m_ref, -jnp.inf)
    acc_ref[...] = jnp.zeros_like(acc_ref)
    ones_rows = jnp.ones((BF16_SUBLANE_PACK, tk), BF16)

    nchunk = 2 * nsub

    def col(c):
        return slice(c * tk, (c + 1) * tk)

    def produce(slot, kv, c):
        start = pl.multiple_of(kv * tk, tk)
        q_cols = qT_ref[0, 0, c // 2, :, (c % 2) * tk:(c % 2 + 1) * tk]
        s = _dot(k_ref[0, pl.ds(start, tk), :], q_cols)
        s_ref[slot, :, col(c)] = s
        cmax_ref[slot, :, col(c)] = jnp.max(s, axis=0, keepdims=True)

    def consume(slot, kv, c, diagonal=False):
        s = s_ref[slot, :, col(c)]
        if diagonal:
            r = lax.broadcasted_iota(jnp.int32, s.shape, 0)
            qpos = lax.broadcasted_iota(jnp.int32, s.shape, 1)
            s = jnp.where(r <= qpos, s, -jnp.inf)
            cmax = jnp.max(s, axis=0, keepdims=True)
        else:
            cmax = cmax_ref[slot, :, col(c)]
        m_prev = m_ref[:, col(c)]
        m_new = jnp.maximum(m_prev, cmax)
        alpha = jnp.exp2(m_prev - m_new)
        p = jnp.exp2(s - m_new).astype(BF16)
        v_ext = jnp.concatenate([vT_ref[0, 0, kv], ones_rows], axis=0)
        acc_ref[:, col(c)] = alpha * acc_ref[:, col(c)] + _dot(v_ext, p)
        m_ref[:, col(c)] = m_new

    lead = ATTN_PRODUCE_LEAD
    assert lead <= nchunk

    for c in range(lead):
        produce(0, 0, c)

    def off_diagonal(kv, nblocks):
        for idx in range(nblocks * nchunk):
            ahead = idx + lead
            produce((ahead // nchunk) % 2, kv + ahead // nchunk, ahead % nchunk)
            consume((idx // nchunk) % 2, kv + idx // nchunk, idx % nchunk)

    kv0 = nsub * qi
    per_trip = ATTN_BLOCKS_PER_TRIP
    assert per_trip % 2 == 0 and nsub % 2 == 0
    trips = kv0 // per_trip

    def trip(i, carry):
        off_diagonal(i * per_trip, per_trip)
        return carry

    lax.fori_loop(0, trips, trip, 0)
    for left in range(2, per_trip, 2):
        @pl.when(kv0 - trips * per_trip == left)
        def _(left=left):
            off_diagonal(trips * per_trip, left)

    units = [(d, c) for d in range(nsub) for c in range(nchunk) if c // 2 >= d]
    assert [u for u in units[:lead]] == [(0, c) for c in range(lead)]
    lam = (jnp.exp(jnp.sum(lq1_ref[...] * lk1_ref[...], axis=-1, keepdims=True))
           - jnp.exp(jnp.sum(lq2_ref[...] * lk2_ref[...], axis=-1, keepdims=True))
           + LAMBDA_INIT_L0)

    def finalize(sub):
        c1 = sub * sw
        c2 = c1 + tk
        o1 = acc_ref[0:dv, c1:c1 + tk] / acc_ref[dv:dv + 1, c1:c1 + tk]
        o2 = acc_ref[0:dv, c2:c2 + tk] / acc_ref[dv:dv + 1, c2:c2 + tk]
        o = (o1 - lam * o2).T
        ms = jnp.mean(o * o, axis=-1, keepdims=True)
        o = o * lax.rsqrt(ms + EPS) * normw_ref[...] * (1.0 - LAMBDA_INIT_L0)
        rows = slice(sub * tk, (sub + 1) * tk)
        o_ref[0, rows, :] = (o * za_ref[0, rows, :].astype(F32)).astype(BF16)

    for i, (d, c) in enumerate(units):
        if i + lead < len(units):
            nd, nc = units[i + lead]
            produce(nd % 2, kv0 + nd, nc)
        consume(d % 2, kv0 + d, c, diagonal=(c // 2 == d))
        if c % 2 == 1 and c // 2 == d:
            finalize(d)


def _attention(q_t, k, v_t, za, lq1, lk1, lq2, lk2, attn_norm_w):
    bsz, seq, d_attn = k.shape
    hw = ATTN_V_DIM
    nh = d_attn // hw
    tk = TK_ATTN
    nsub = NSUB_ATTN
    tq = nsub * tk
    nblk = seq // tk
    assert nsub % 2 == 0 and seq % tq == 0
    assert q_t.shape == (bsz, nh, nblk, hw, 2 * tk) and v_t.shape == (bsz, nh, nblk, hw, tk)
    grid = (bsz, nh, seq // tq)
    qrow = lambda b, h, i: (b, i, h)
    const2 = lambda b, h, i: (0, 0)
    vec = pl.BlockSpec((1, ATTN_QK_DIM), const2)
    in_specs = [
        pl.BlockSpec((1, 1, nsub, hw, 2 * tk), lambda b, h, i: (b, h, i, 0, 0)),
        pl.BlockSpec((1, seq, hw), lambda b, h, i: (b, 0, h)),
        pl.BlockSpec((1, 1, nblk, hw, tk), lambda b, h, i: (b, h, 0, 0, 0)),
        pl.BlockSpec((1, tq, hw), qrow),
        vec, vec, vec, vec,
        pl.BlockSpec((1, hw), const2),
    ]
    width = 2 * tq
    vmem = (2 * 2 * seq * hw * 2 + 2 * tk * width * 4 + (2 * hw + BF16_SUBLANE_PACK) * width * 4
            + 4 * tk * width * 4)
    return pl.pallas_call(
        functools.partial(_attn_kernel, tk=tk, nsub=nsub), grid=grid, in_specs=in_specs,
        out_specs=pl.BlockSpec((1, tq, hw), qrow),
        out_shape=jax.ShapeDtypeStruct((bsz, seq, d_attn), BF16),
        scratch_shapes=[pltpu.VMEM((2, tk, width), F32),
                        pltpu.VMEM((2, 1, width), F32), pltpu.VMEM((1, width), F32),
                        pltpu.VMEM((hw + BF16_SUBLANE_PACK, width), F32)],
        compiler_params=pltpu.CompilerParams(
            dimension_semantics=("arbitrary", "arbitrary", "arbitrary"),
            vmem_limit_bytes=_vmem_limit(vmem)),
        name="diff_attention",
    )(q_t, k, v_t, za, lq1.reshape(1, -1), lk1.reshape(1, -1), lq2.reshape(1, -1),
      lk2.reshape(1, -1), attn_norm_w.reshape(1, -1))


def _outproj_kernel(ys_ref, ya_ref, w_ref, x_ref, gate_ref, g_ref, b_ref, o_ref, *, d_ssd, alpha,
                    tm, rows_per_pass):
    for r0 in range(0, tm, rows_per_pass):
        rows = slice(r0, r0 + rows_per_pass)
        mixed = (_dot(ys_ref[0, rows, :], w_ref[0:d_ssd, :])
                 + _dot(ya_ref[0, rows, :], w_ref[d_ssd:, :]))
        xf = alpha * x_ref[0, rows, :] + gate_ref[0] * mixed
        mu = jnp.mean(xf, axis=-1, keepdims=True)
        xc = xf - mu
        var = jnp.mean(xc * xc, axis=-1, keepdims=True)
        o_ref[0, rows, :] = xc * lax.rsqrt(var + EPS) * g_ref[...] + b_ref[...]


def _out_projection(y_ssd, y_attn, w_out, x, gate, ln_g, ln_b, alpha):
    bsz, seq, d = x.shape
    d_ssd = y_ssd.shape[-1]
    d_attn = y_attn.shape[-1]
    tm = TM_OUT
    row = lambda b, l: (b, l, 0)
    const2 = lambda b, l: (0, 0)
    in_specs = [
        pl.BlockSpec((1, tm, d_ssd), row),
        pl.BlockSpec((1, tm, d_attn), row),
        pl.BlockSpec((d_ssd + d_attn, d), const2, pipeline_mode=pl.Buffered(1)),
        pl.BlockSpec((1, tm, d), row),
        pl.BlockSpec((1, 1, d), lambda b, l: (b, 0, 0)),
        pl.BlockSpec((1, d), const2),
        pl.BlockSpec((1, d), const2),
    ]
    vmem = (d_ssd + d_attn) * d * 2 + 2 * tm * (d_ssd + d_attn) * 2 + 4 * tm * d * 4 + 4 * tm * d * 4
    return pl.pallas_call(
        functools.partial(_outproj_kernel, d_ssd=d_ssd, alpha=alpha, tm=tm,
                          rows_per_pass=TM_OUT_PASS),
        grid=(bsz, seq // tm), in_specs=in_specs,
        out_specs=pl.BlockSpec((1, tm, d), row),
        out_shape=jax.ShapeDtypeStruct((bsz, seq, d), F32),
        compiler_params=pltpu.CompilerParams(
            dimension_semantics=("arbitrary", "arbitrary"), vmem_limit_bytes=_vmem_limit(vmem)),
        name="out_projection",
    )(y_ssd, y_attn, w_out, x, gate, ln_g.reshape(1, -1), ln_b.reshape(1, -1))


def _layer(x, c, w_ada, b_ada, w_in, conv_w, conv_b, dt_bias, a_log, d_skip, ssd_norm_w,
           lambda_q1, lambda_k1, lambda_q2, lambda_k2, attn_norm_w, w_out, ln_g, ln_b, depth):
    bsz, seq, d = x.shape
    nheads = a_log.shape[0]
    d_ssd = nheads * SSD_HEAD_DIM
    d_xbc = conv_w.shape[1]
    d_attn = w_out.shape[0] - d_ssd
    assert d_xbc == d_ssd + 2 * SSD_GROUPS * SSD_STATE
    assert w_in.shape[1] == d_ssd + d_xbc + nheads + 4 * d_attn
    assert seq % TM_INPROJ == 0 and seq % TM_OUT == 0 and seq % CHUNK == 0
    assert TM_INPROJ == TK_ATTN, "the in-projection writes q^T / v^T in attention-sized time blocks"

    mod = _modulation(c, w_ada, b_ada)
    shift = mod[:, None, 0:d]
    scale = mod[:, None, d:2 * d]
    gate = mod[:, None, 2 * d:3 * d]

    off_dt = d_ssd + d_xbc
    off_q = off_dt + nheads
    off_k, off_v, off_za = off_q + d_attn, off_q + 2 * d_attn, off_q + 3 * d_attn
    w_main = jnp.concatenate([w_in[:, :off_dt], w_in[:, off_k:off_v], w_in[:, off_za:]],
                             axis=1).astype(BF16)
    w_qv_t = jnp.concatenate([w_in[:, off_q:off_k], w_in[:, off_v:off_za]], axis=1).T.astype(BF16)
    w_dt = w_in[:, off_dt:off_q]

    zs, xs, bm, cm, q_t, k, v_t, za, dt, dt_t = _in_projection(
        x, scale, shift, w_main, w_qv_t, w_dt, conv_w, conv_b, dt_bias,
        d_ssd=d_ssd, d_xbc=d_xbc, d_attn=d_attn)
    y_ssd = _ssd(xs, bm, cm, zs, dt, dt_t, a_log, d_skip, ssd_norm_w)
    y_attn = _attention(q_t, k, v_t, za, lambda_q1, lambda_k1, lambda_q2, lambda_k2, attn_norm_w)
    alpha = (2.0 * depth) ** 0.25
    return _out_projection(y_ssd, y_attn, w_out.astype(BF16), x, gate, ln_g, ln_b, alpha)


def kernel(x, c, w_ada, b_ada, w_in, conv_w, conv_b, dt_bias, a_log, d_skip, ssd_norm_w, lambda_q1,
           lambda_k1, lambda_q2, lambda_k2, attn_norm_w, w_out, ln_g, ln_b):
    depth = w_in.shape[0]
    assert depth == 1, "lambda_init and the layer loop are specialised to a single layer"
    return _layer(x, c, w_ada[0], b_ada[0], w_in[0], conv_w[0], conv_b[0], dt_bias[0], a_log[0],
                  d_skip[0], ssd_norm_w[0], lambda_q1[0], lambda_k1[0], lambda_q2[0], lambda_k2[0],
                  attn_norm_w[0], w_out[0], ln_g[0], ln_b[0], depth)
```

```python
import functools
import math

import jax
import jax.numpy as jnp
from jax import lax
from jax.experimental import pallas as pl
from jax.experimental.pallas import tpu as pltpu

F32 = jnp.float32
BF16 = jnp.bfloat16

SSD_HEAD_DIM = 64
SSD_GROUPS = 2
SSD_STATE = 128
CONV_WIDTH = 4
CHUNK = 128
ATTN_QK_DIM = 64
ATTN_V_DIM = 2 * ATTN_QK_DIM
EPS = 1e-5
LAMBDA_INIT_L0 = 0.8 - 0.6 * math.exp(-0.3 * 0)
LOG2_E = math.log2(math.e)

V7X_LANES = 128
V7X_SUBLANES = 8
BF16_SUBLANE_PACK = 16
V7X_VMEM_BYTES = 64 * 1024 * 1024
V7X_VMEM_USABLE_BYTES = 56 * 1024 * 1024

TM_INPROJ = 512
TM_INPROJ_PASS = 256
TK_ATTN = 512
NSUB_ATTN = 2
NQ_ATTN = 2
ATTN_PRODUCE_LEAD = 2
ATTN_BLOCKS_PER_TRIP = 4
TM_OUT = 1024
TM_OUT_PASS = 256
SSD_CHUNKS_PER_STEP = 4


def _vmem_limit(estimate_bytes):
    return int(min(V7X_VMEM_USABLE_BYTES, max(32 * 1024 * 1024, estimate_bytes)))


def _silu(x):
    hx = 0.5 * x
    return hx + hx * jnp.tanh(hx)


def _softplus(x):
    return jnp.maximum(x, 0.0) + jnp.log(1.0 + jnp.exp(-jnp.abs(x)))


def _split_bf16(a, pieces):
    out = []
    r = a
    for i in range(pieces):
        p = r.astype(BF16)
        out.append(p)
        if i + 1 < pieces:
            r = r - p.astype(F32)
    return out


def _dot(a, b):
    return jnp.dot(a, b, preferred_element_type=F32)


def _dot_nt(a, b):
    return lax.dot_general(a, b, (((1,), (1,)), ((), ())), preferred_element_type=F32)


def _dot_tn(a, b):
    return lax.dot_general(a, b, (((0,), (0,)), ((), ())), preferred_element_type=F32)


def _mod_kernel(c_ref, w_ref, b_ref, o_ref):
    c_hi, c_lo = _split_bf16(c_ref[...], 2)
    w_hi, w_lo = _split_bf16(w_ref[...], 2)
    o_ref[...] = _dot(c_hi, w_hi) + _dot(c_hi, w_lo) + _dot(c_lo, w_hi) + b_ref[...]


def _modulation(c, w_ada, b_ada):
    bsz, d = c.shape
    n = w_ada.shape[1]
    rows = V7X_SUBLANES
    c_pad = jnp.zeros((rows, d), F32).at[:bsz].set(c)
    bn = d
    out = pl.pallas_call(
        _mod_kernel,
        grid=(n // bn,),
        in_specs=[pl.BlockSpec((rows, d), lambda j: (0, 0)),
                  pl.BlockSpec((d, bn), lambda j: (0, j)),
                  pl.BlockSpec((1, bn), lambda j: (0, j))],
        out_specs=pl.BlockSpec((rows, bn), lambda j: (0, j)),
        out_shape=jax.ShapeDtypeStruct((rows, n), F32),
        name="modulation",
    )(c_pad, w_ada, b_ada.reshape(1, n))
    return out[:bsz]


def _inproj_kernel(x_ref, scale_ref, shift_ref, w_ref, wqvT_ref, wdt_ref, convw_ref,
                   convb_ref, dtb_ref,
                   zs_ref, xs_ref, bm_ref, cm_ref, qT_ref, k_ref, vT_ref, za_ref, dt_ref, dtT_ref,
                   carry_ref, *, tm, rows_per_pass, d_ssd, d_xbc, d_attn, n_bc):
    @pl.when(pl.program_id(1) == 0)
    def _():
        carry_ref[...] = jnp.zeros_like(carry_ref)

    off_z, off_xbc, off_k, off_za = 0, d_ssd, d_ssd + d_xbc, d_ssd + d_xbc + d_attn
    pad = V7X_SUBLANES
    nh = dt_ref.shape[-1]
    nh_attn = d_attn // ATTN_V_DIM
    cw = convw_ref[...]
    w_hi, w_lo = _split_bf16(wdt_ref[...], 2)
    wlane = lax.broadcasted_iota(jnp.int32, (1, V7X_LANES), 1)
    w_hilo = jnp.where(wlane < nh, w_hi, w_lo)

    tail = carry_ref[...]
    for r0 in range(0, tm, rows_per_pass):
        rows = slice(r0, r0 + rows_per_pass)
        h32 = x_ref[0, rows, :] * (1.0 + scale_ref[0]) + shift_ref[0]
        h = h32.astype(BF16)

        def proj(lo, width):
            return _dot(h, w_ref[:, lo:lo + width])

        u = proj(off_xbc, d_xbc)
        u_ext = jnp.concatenate([tail, u], axis=0)
        tail = u[rows_per_pass - pad:rows_per_pass, :]
        acc = convb_ref[...] + cw[CONV_WIDTH - 1:CONV_WIDTH, :] * u
        for tap in range(CONV_WIDTH - 1):
            shift = CONV_WIDTH - 1 - tap
            acc = acc + (cw[tap:tap + 1, :]
                         * pltpu.roll(u_ext, shift, 0)[pad:pad + rows_per_pass, :])
        xbc = _silu(acc)
        xs_ref[0, rows, :] = xbc[:, :d_ssd].astype(BF16)
        bm_ref[0, rows, :] = xbc[:, d_ssd:d_ssd + n_bc].astype(BF16)
        cm_ref[0, rows, :] = xbc[:, d_ssd + n_bc:d_ssd + 2 * n_bc].astype(BF16)

        zs_ref[0, rows, :] = _silu(proj(off_z, d_ssd)).astype(BF16)
        k_ref[0, rows, :] = proj(off_k, d_attn).astype(BF16)
        za_ref[0, rows, :] = _silu(proj(off_za, d_attn)).astype(BF16)

        qvT = _dot_nt(wqvT_ref[...], h)
        qT = qvT[:d_attn] * (ATTN_QK_DIM ** -0.5 * LOG2_E)
        qT = qT.astype(BF16).reshape(nh_attn, ATTN_V_DIM, rows_per_pass)
        feat = lax.broadcasted_iota(jnp.int32, (1, ATTN_V_DIM, 1), 1)
        zero = jnp.zeros_like(qT)
        qT_ref[0, :, 0, :, rows] = jnp.where(feat < ATTN_QK_DIM, qT, zero)
        qT_ref[0, :, 0, :, tm + r0:tm + r0 + rows_per_pass] = jnp.where(feat >= ATTN_QK_DIM, qT, zero)
        vT_ref[0, :, 0, :, rows] = qvT[d_attn:].astype(BF16).reshape(nh_attn, ATTN_V_DIM,
                                                                      rows_per_pass)

        h_lo = (h32 - h.astype(F32)).astype(BF16)
        r1 = _dot(h, w_hilo)
        r2 = _dot(h_lo, w_hi)
        dt_raw = r1 + pltpu.roll(r1, V7X_LANES - nh, 1) + r2
        dt = _softplus(dt_raw + dtb_ref[...])
        dt_ref[0, rows, :] = dt[:, :nh]
        dtT_ref[0, :, rows] = dt.T[:nh, :]
    carry_ref[...] = tail


def _in_projection(x, scale, shift, w_main, w_qv_t, w_dt, conv_w, conv_b, dt_bias, *, d_ssd, d_xbc,
                   d_attn):
    bsz, seq, d = x.shape
    tm = TM_INPROJ
    nh = w_dt.shape[1]
    nh_attn = d_attn // ATTN_V_DIM
    n_bc = (d_xbc - d_ssd) // 2
    n_main = w_main.shape[1]
    grid = (bsz, seq // tm)

    row = lambda b, l: (b, l, 0)
    head_major = lambda b, l: (b, 0, l, 0, 0)
    const2 = lambda b, l: (0, 0)
    per_b = lambda b, l: (b, 0, 0)
    single = pl.Buffered(1)
    in_specs = [
        pl.BlockSpec((1, tm, d), row),
        pl.BlockSpec((1, 1, d), per_b),
        pl.BlockSpec((1, 1, d), per_b),
        pl.BlockSpec((d, n_main), const2, pipeline_mode=single),
        pl.BlockSpec((2 * d_attn, d), const2, pipeline_mode=single),
        pl.BlockSpec((d, V7X_LANES), const2),
        pl.BlockSpec((CONV_WIDTH, d_xbc), const2),
        pl.BlockSpec((1, d_xbc), const2),
        pl.BlockSpec((1, V7X_LANES), const2),
    ]
    out_specs = [
        pl.BlockSpec((1, tm, d_ssd), row),
        pl.BlockSpec((1, tm, d_ssd), row),
        pl.BlockSpec((1, tm, n_bc), row),
        pl.BlockSpec((1, tm, n_bc), row),
        pl.BlockSpec((1, nh_attn, 1, ATTN_V_DIM, 2 * tm), head_major),
        pl.BlockSpec((1, tm, d_attn), row),
        pl.BlockSpec((1, nh_attn, 1, ATTN_V_DIM, tm), head_major),
        pl.BlockSpec((1, tm, d_attn), row),
        pl.BlockSpec((1, tm, nh), row),
        pl.BlockSpec((1, nh, tm), lambda b, l: (b, 0, l)),
    ]
    out_shape = [
        jax.ShapeDtypeStruct((bsz, seq, d_ssd), BF16),
        jax.ShapeDtypeStruct((bsz, seq, d_ssd), BF16),
        jax.ShapeDtypeStruct((bsz, seq, n_bc), BF16),
        jax.ShapeDtypeStruct((bsz, seq, n_bc), BF16),
        jax.ShapeDtypeStruct((bsz, nh_attn, seq // tm, ATTN_V_DIM, 2 * tm), BF16),
        jax.ShapeDtypeStruct((bsz, seq, d_attn), BF16),
        jax.ShapeDtypeStruct((bsz, nh_attn, seq // tm, ATTN_V_DIM, tm), BF16),
        jax.ShapeDtypeStruct((bsz, seq, d_attn), BF16),
        jax.ShapeDtypeStruct((bsz, seq, nh), F32),
        jax.ShapeDtypeStruct((bsz, nh, seq), F32),
    ]
    pad = V7X_SUBLANES
    vmem = (d * (n_main + 2 * d_attn) * 2 + 2 * tm * d * 4
            + 2 * tm * (2 * d_ssd + 2 * n_bc + 4 * d_attn) * 2
            + (tm + 2 * pad) * d_xbc * 4 + 6 * tm * d_xbc * 4)
    kern = functools.partial(_inproj_kernel, tm=tm, rows_per_pass=TM_INPROJ_PASS, d_ssd=d_ssd,
                             d_xbc=d_xbc, d_attn=d_attn, n_bc=n_bc)
    assert 2 * nh <= V7X_LANES
    w_dt_pad = jnp.zeros((d, V7X_LANES), F32).at[:, :nh].set(w_dt).at[:, nh:2 * nh].set(w_dt)
    dt_bias_pad = jnp.zeros((1, V7X_LANES), F32).at[0, :nh].set(dt_bias)
    return pl.pallas_call(
        kern, grid=grid, in_specs=in_specs, out_specs=out_specs, out_shape=out_shape,
        scratch_shapes=[pltpu.VMEM((pad, d_xbc), F32)],
        compiler_params=pltpu.CompilerParams(
            dimension_semantics=("arbitrary", "arbitrary"), vmem_limit_bytes=_vmem_limit(vmem)),
        name="in_projection",
    )(x, scale, shift, w_main, w_qv_t, w_dt_pad, conv_w, conv_b.reshape(1, -1), dt_bias_pad)


def _ssd_kernel(xs_ref, bm_ref, cm_ref, zs_ref, dt_ref, dtT_ref, alog_ref, alogT_ref, dskip_ref,
                normw_ref, y_ref, state_ref, ybuf_ref, *, nheads, d_ssd, cps):
    t = CHUNK
    n = SSD_STATE
    hp = SSD_HEAD_DIM
    gw = d_ssd // SSD_GROUPS
    heads_per_group = nheads // SSD_GROUPS

    @pl.when(pl.program_id(1) == 0)
    def _():
        state_ref[...] = jnp.zeros_like(state_ref)

    row = lax.broadcasted_iota(jnp.int32, (t, t), 0)
    col = lax.broadcasted_iota(jnp.int32, (t, t), 1)
    causal = col <= row
    tri = jnp.where(causal, 1.0, 0.0).astype(BF16)
    tri_t = jnp.where(row <= col, 1.0, 0.0).astype(BF16)
    eh = lax.broadcasted_iota(jnp.int32, (nheads, d_ssd), 0)
    ej = lax.broadcasted_iota(jnp.int32, (nheads, d_ssd), 1)
    expand_m = jnp.where(ej // hp == eh, 1.0, 0.0).astype(BF16)
    lane = lax.broadcasted_iota(jnp.int32, (1, 2 * hp), 1)
    neg_inf = jnp.float32(-jnp.inf)
    a = -jnp.exp(alog_ref[...])
    a_t = -jnp.exp(alogT_ref[...])

    def expand(v, pieces):
        return sum(_dot(p, expand_m) for p in _split_bf16(v, pieces))

    for ci in range(cps):
        rows = slice(ci * t, (ci + 1) * t)
        dt = dt_ref[0, rows, :]
        dt_t = dtT_ref[0, :, rows]
        acum = sum(_dot(tri, p) for p in _split_bf16(dt * a, 3))
        acum_t = sum(_dot(p, tri_t) for p in _split_bf16(dt_t * a_t, 3))
        last = acum[t - 1:t, :]
        dt_x = expand(dt, 1)
        w_x = expand(jnp.exp(last - acum), 1)
        e_x = expand(jnp.exp(acum), 2)

        xs = xs_ref[0, rows, :].astype(F32)
        xdt = xs * dt_x
        xdt_b = xdt.astype(BF16)
        xw_b = (xdt * w_x).astype(BF16)

        for g in range(SSD_GROUPS):
            b_g = bm_ref[0, rows, g * n:(g + 1) * n]
            c_g = cm_ref[0, rows, g * n:(g + 1) * n]
            gsl = slice(g * gw, (g + 1) * gw)
            cb = _dot_nt(c_g, b_g)
            prev = state_ref[g]
            y_off = _dot(c_g, prev.astype(BF16)) * e_x[:, gsl]
            state_ref[g] = prev * e_x[t - 1:t, gsl] + _dot_tn(b_g, xw_b[:, gsl])
            for pair in range(heads_per_group // 2):
                lo = g * gw + pair * 2 * hp
                x_pair = xdt_b[:, lo:lo + 2 * hp]
                decays = []
                for half in range(2):
                    hd = g * heads_per_group + pair * 2 + half
                    diff = acum[:, hd:hd + 1] - acum_t[hd:hd + 1, :]
                    decays.append((cb * jnp.exp(jnp.where(causal, diff, neg_inf))).astype(BF16))
                zero = jnp.zeros_like(x_pair)
                x_diag = jnp.concatenate([jnp.where(lane < hp, x_pair, zero),
                                          jnp.where(lane >= hp, x_pair, zero)], axis=0)
                y_pair = (y_off[:, pair * 2 * hp:(pair + 1) * 2 * hp]
                          + _dot(jnp.concatenate(decays, axis=1), x_diag))
                ybuf_ref[rows, lo:lo + 2 * hp] = y_pair

    xs = xs_ref[0].astype(F32)
    y = ybuf_ref[...] + dskip_ref[...] * xs
    y = y * zs_ref[0].astype(F32)
    for g in range(SSD_GROUPS):
        gsl = slice(g * gw, (g + 1) * gw)
        y_g = y[:, gsl]
        ms = jnp.mean(y_g * y_g, axis=-1, keepdims=True)
        y_ref[0, :, gsl] = (y_g * lax.rsqrt(ms + EPS) * normw_ref[:, gsl]).astype(BF16)


def _ssd(xs, bm, cm, zs, dt, dt_t, a_log, d_skip, ssd_norm_w):
    bsz, seq, d_ssd = xs.shape
    nheads = dt.shape[-1]
    n_bc = bm.shape[-1]
    cps = SSD_CHUNKS_PER_STEP
    t = cps * CHUNK
    assert seq % t == 0
    grid = (bsz, seq // t)
    row = lambda b, c: (b, c, 0)
    const2 = lambda b, c: (0, 0)
    in_specs = [
        pl.BlockSpec((1, t, d_ssd), row),
        pl.BlockSpec((1, t, n_bc), row),
        pl.BlockSpec((1, t, n_bc), row),
        pl.BlockSpec((1, t, d_ssd), row),
        pl.BlockSpec((1, t, nheads), row),
        pl.BlockSpec((1, nheads, t), lambda b, c: (b, 0, c)),
        pl.BlockSpec((1, nheads), const2),
        pl.BlockSpec((nheads, 1), const2),
        pl.BlockSpec((1, d_ssd), const2),
        pl.BlockSpec((1, d_ssd), const2),
    ]
    kern = functools.partial(_ssd_kernel, nheads=nheads, d_ssd=d_ssd, cps=cps)
    return pl.pallas_call(
        kern, grid=grid, in_specs=in_specs,
        out_specs=pl.BlockSpec((1, t, d_ssd), row),
        out_shape=jax.ShapeDtypeStruct((bsz, seq, d_ssd), BF16),
        scratch_shapes=[pltpu.VMEM((SSD_GROUPS, SSD_STATE, d_ssd // SSD_GROUPS), F32),
                        pltpu.VMEM((t, d_ssd), F32)],
        compiler_params=pltpu.CompilerParams(dimension_semantics=("arbitrary", "arbitrary")),
        name="ssd_scan",
    )(xs, bm, cm, zs, dt, dt_t, a_log.reshape(1, -1), a_log.reshape(-1, 1),
      jnp.repeat(d_skip, SSD_HEAD_DIM).reshape(1, -1), ssd_norm_w.reshape(1, -1))


def _attn_kernel(qT_ref, k_ref, vT_ref, za_ref, lq1_ref, lk1_ref, lq2_ref, lk2_ref, normw_ref,
                 o_ref, s_ref, cmax_ref, m_ref, acc_ref, *, tk, nsub, nq):
    step = pl.program_id(2)
    dk = ATTN_QK_DIM
    dv = 2 * dk
    sw = 2 * tk
    nchunk = 2 * nsub
    tq = nsub * tk
    lead = ATTN_PRODUCE_LEAD
    per_trip = ATTN_BLOCKS_PER_TRIP
    assert lead <= nchunk and per_trip % 2 == 0 and nsub % 2 == 0 and (nsub * nq) % per_trip == 0
    ones_rows = jnp.ones((BF16_SUBLANE_PACK, tk), BF16)
    lam = (jnp.exp(jnp.sum(lq1_ref[...] * lk1_ref[...], axis=-1, keepdims=True))
           - jnp.exp(jnp.sum(lq2_ref[...] * lk2_ref[...], axis=-1, keepdims=True))
           + LAMBDA_INIT_L0)

    def col(c):
        return slice(c * tk, (c + 1) * tk)

    def query_block(g):
        def produce(slot, kv, c):
            start = pl.multiple_of(kv * tk, tk)
            q_cols = qT_ref[0, 0, g * nsub + c // 2, :, (c % 2) * tk:(c % 2 + 1) * tk]
            s = _dot(k_ref[0, pl.ds(start, tk), :], q_cols)
            s_ref[g, slot, :, col(c)] = s
            cmax_ref[g, slot, :, col(c)] = jnp.max(s, axis=0, keepdims=True)

        def consume(slot, kv, c, diagonal=False):
            s = s_ref[g, slot, :, col(c)]
            if diagonal:
                r = lax.broadcasted_iota(jnp.int32, s.shape, 0)
                qpos = lax.broadcasted_iota(jnp.int32, s.shape, 1)
                s = jnp.where(r <= qpos, s, -jnp.inf)
                cmax = jnp.max(s, axis=0, keepdims=True)
            else:
                cmax = cmax_ref[g, slot, :, col(c)]
            m_prev = m_ref[g, :, col(c)]
            m_new = jnp.maximum(m_prev, cmax)
            alpha = jnp.exp2(m_prev - m_new)
            p = jnp.exp2(s - m_new).astype(BF16)
            v_ext = jnp.concatenate([vT_ref[0, 0, kv], ones_rows], axis=0)
            acc_ref[g, :, col(c)] = alpha * acc_ref[g, :, col(c)] + _dot(v_ext, p)
            m_ref[g, :, col(c)] = m_new

        def finalize(sub):
            c1 = sub * sw
            c2 = c1 + tk
            o1 = acc_ref[g, 0:dv, c1:c1 + tk] / acc_ref[g, dv:dv + 1, c1:c1 + tk]
            o2 = acc_ref[g, 0:dv, c2:c2 + tk] / acc_ref[g, dv:dv + 1, c2:c2 + tk]
            o = (o1 - lam * o2).T
            ms = jnp.mean(o * o, axis=-1, keepdims=True)
            o = o * lax.rsqrt(ms + EPS) * normw_ref[...] * (1.0 - LAMBDA_INIT_L0)
            rows = slice(g * tq + sub * tk, g * tq + (sub + 1) * tk)
            o_ref[0, rows, :] = (o * za_ref[0, rows, :].astype(F32)).astype(BF16)

        m_ref[g] = jnp.full(m_ref.shape[1:], -jnp.inf, F32)
        acc_ref[g] = jnp.zeros(acc_ref.shape[1:], F32)

        for c in range(lead):
            produce(0, 0, c)

        def off_diagonal(kv, nblocks):
            for idx in range(nblocks * nchunk):
                ahead = idx + lead
                produce((ahead // nchunk) % 2, kv + ahead // nchunk, ahead % nchunk)
                consume((idx // nchunk) % 2, kv + idx // nchunk, idx % nchunk)

        trips = (nsub * nq // per_trip) * step + (nsub * g) // per_trip
        left = (nsub * g) % per_trip

        def trip(i, carry):
            off_diagonal(i * per_trip, per_trip)
            return carry

        lax.fori_loop(0, trips, trip, 0)
        if left:
            off_diagonal(trips * per_trip, left)

        kv0 = trips * per_trip + left
        units = [(d, c) for d in range(nsub) for c in range(nchunk) if c // 2 >= d]
        assert units[:lead] == [(0, c) for c in range(lead)]
        for i, (d, c) in enumerate(units):
            if i + lead < len(units):
                nd, nc = units[i + lead]
                produce(nd % 2, kv0 + nd, nc)
            consume(d % 2, kv0 + d, c, diagonal=(c // 2 == d))
            if c % 2 == 1 and c // 2 == d:
                finalize(d)

    for g in range(nq):
        query_block(g)


def _attention(q_t, k, v_t, za, lq1, lk1, lq2, lk2, attn_norm_w):
    bsz, seq, d_attn = k.shape
    hw = ATTN_V_DIM
    nh = d_attn // hw
    tk = TK_ATTN
    nsub = NSUB_ATTN
    nq = NQ_ATTN
    tq = nsub * tk
    rows = nq * tq
    nblk = seq // tk
    assert nsub % 2 == 0 and seq % rows == 0
    assert q_t.shape == (bsz, nh, nblk, hw, 2 * tk) and v_t.shape == (bsz, nh, nblk, hw, tk)
    grid = (bsz, nh, seq // rows)
    qrow = lambda b, h, i: (b, i, h)
    const2 = lambda b, h, i: (0, 0)
    vec = pl.BlockSpec((1, ATTN_QK_DIM), const2)
    in_specs = [
        pl.BlockSpec((1, 1, nq * nsub, hw, 2 * tk), lambda b, h, i: (b, h, i, 0, 0)),
        pl.BlockSpec((1, seq, hw), lambda b, h, i: (b, 0, h)),
        pl.BlockSpec((1, 1, nblk, hw, tk), lambda b, h, i: (b, h, 0, 0, 0)),
        pl.BlockSpec((1, rows, hw), qrow),
        vec, vec, vec, vec,
        pl.BlockSpec((1, hw), const2),
    ]
    width = 2 * tq
    vmem = (2 * 2 * seq * hw * 2 + nq * 2 * tk * width * 4
            + nq * (2 * hw + BF16_SUBLANE_PACK) * width * 4 + 4 * tk * width * 4)
    return pl.pallas_call(
        functools.partial(_attn_kernel, tk=tk, nsub=nsub, nq=nq), grid=grid, in_specs=in_specs,
        out_specs=pl.BlockSpec((1, rows, hw), qrow),
        out_shape=jax.ShapeDtypeStruct((bsz, seq, d_attn), BF16),
        scratch_shapes=[pltpu.VMEM((nq, 2, tk, width), F32),
                        pltpu.VMEM((nq, 2, 1, width), F32), pltpu.VMEM((nq, 1, width), F32),
                        pltpu.VMEM((nq, hw + BF16_SUBLANE_PACK, width), F32)],
        compiler_params=pltpu.CompilerParams(
            dimension_semantics=("arbitrary", "arbitrary", "arbitrary"),
            vmem_limit_bytes=_vmem_limit(vmem)),
        name="diff_attention",
    )(q_t, k, v_t, za, lq1.reshape(1, -1), lk1.reshape(1, -1), lq2.reshape(1, -1),
      lk2.reshape(1, -1), attn_norm_w.reshape(1, -1))


def _outproj_kernel(ys_ref, ya_ref, w_ref, x_ref, gate_ref, g_ref, b_ref, o_ref, *, d_ssd, alpha,
                    tm, rows_per_pass):
    for r0 in range(0, tm, rows_per_pass):
        rows = slice(r0, r0 + rows_per_pass)
        mixed = (_dot(ys_ref[0, rows, :], w_ref[0:d_ssd, :])
                 + _dot(ya_ref[0, rows, :], w_ref[d_ssd:, :]))
        xf = alpha * x_ref[0, rows, :] + gate_ref[0] * mixed
        mu = jnp.mean(xf, axis=-1, keepdims=True)
        xc = xf - mu
        var = jnp.mean(xc * xc, axis=-1, keepdims=True)
        o_ref[0, rows, :] = xc * lax.rsqrt(var + EPS) * g_ref[...] + b_ref[...]


def _out_projection(y_ssd, y_attn, w_out, x, gate, ln_g, ln_b, alpha):
    bsz, seq, d = x.shape
    d_ssd = y_ssd.shape[-1]
    d_attn = y_attn.shape[-1]
    tm = TM_OUT
    row = lambda b, l: (b, l, 0)
    const2 = lambda b, l: (0, 0)
    in_specs = [
        pl.BlockSpec((1, tm, d_ssd), row),
        pl.BlockSpec((1, tm, d_attn), row),
        pl.BlockSpec((d_ssd + d_attn, d), const2, pipeline_mode=pl.Buffered(1)),
        pl.BlockSpec((1, tm, d), row),
        pl.BlockSpec((1, 1, d), lambda b, l: (b, 0, 0)),
        pl.BlockSpec((1, d), const2),
        pl.BlockSpec((1, d), const2),
    ]
    vmem = (d_ssd + d_attn) * d * 2 + 2 * tm * (d_ssd + d_attn) * 2 + 4 * tm * d * 4 + 4 * tm * d * 4
    return pl.pallas_call(
        functools.partial(_outproj_kernel, d_ssd=d_ssd, alpha=alpha, tm=tm,
                          rows_per_pass=TM_OUT_PASS),
        grid=(bsz, seq // tm), in_specs=in_specs,
        out_specs=pl.BlockSpec((1, tm, d), row),
        out_shape=jax.ShapeDtypeStruct((bsz, seq, d), F32),
        compiler_params=pltpu.CompilerParams(
            dimension_semantics=("arbitrary", "arbitrary"), vmem_limit_bytes=_vmem_limit(vmem)),
        name="out_projection",
    )(y_ssd, y_attn, w_out, x, gate, ln_g.reshape(1, -1), ln_b.reshape(1, -1))


def _layer(x, c, w_ada, b_ada, w_in, conv_w, conv_b, dt_bias, a_log, d_skip, ssd_norm_w,
           lambda_q1, lambda_k1, lambda_q2, lambda_k2, attn_norm_w, w_out, ln_g, ln_b, depth):
    bsz, seq, d = x.shape
    nheads = a_log.shape[0]
    d_ssd = nheads * SSD_HEAD_DIM
    d_xbc = conv_w.shape[1]
    d_attn = w_out.shape[0] - d_ssd
    assert d_xbc == d_ssd + 2 * SSD_GROUPS * SSD_STATE
    assert w_in.shape[1] == d_ssd + d_xbc + nheads + 4 * d_attn
    assert seq % TM_INPROJ == 0 and seq % TM_OUT == 0 and seq % CHUNK == 0
    assert TM_INPROJ == TK_ATTN, "the in-projection writes q^T / v^T in attention-sized time blocks"

    mod = _modulation(c, w_ada, b_ada)
    shift = mod[:, None, 0:d]
    scale = mod[:, None, d:2 * d]
    gate = mod[:, None, 2 * d:3 * d]

    off_dt = d_ssd + d_xbc
    off_q = off_dt + nheads
    off_k, off_v, off_za = off_q + d_attn, off_q + 2 * d_attn, off_q + 3 * d_attn
    w_main = jnp.concatenate([w_in[:, :off_dt], w_in[:, off_k:off_v], w_in[:, off_za:]],
                             axis=1).astype(BF16)
    w_qv_t = jnp.concatenate([w_in[:, off_q:off_k], w_in[:, off_v:off_za]], axis=1).T.astype(BF16)
    w_dt = w_in[:, off_dt:off_q]

    zs, xs, bm, cm, q_t, k, v_t, za, dt, dt_t = _in_projection(
        x, scale, shift, w_main, w_qv_t, w_dt, conv_w, conv_b, dt_bias,
        d_ssd=d_ssd, d_xbc=d_xbc, d_attn=d_attn)
    y_ssd = _ssd(xs, bm, cm, zs, dt, dt_t, a_log, d_skip, ssd_norm_w)
    y_attn = _attention(q_t, k, v_t, za, lambda_q1, lambda_k1, lambda_q2, lambda_k2, attn_norm_w)
    alpha = (2.0 * depth) ** 0.25
    return _out_projection(y_ssd, y_attn, w_out.astype(BF16), x, gate, ln_g, ln_b, alpha)


def kernel(x, c, w_ada, b_ada, w_in, conv_w, conv_b, dt_bias, a_log, d_skip, ssd_norm_w, lambda_q1,
           lambda_k1, lambda_q2, lambda_k2, attn_norm_w, w_out, ln_g, ln_b):
    depth = w_in.shape[0]
    assert depth == 1, "lambda_init and the layer loop are specialised to a single layer"
    return _layer(x, c, w_ada[0], b_ada[0], w_in[0], conv_w[0], conv_b[0], dt_bias[0], a_log[0],
                  d_skip[0], ssd_norm_w[0], lambda_q1[0], lambda_k1[0], lambda_q2[0], lambda_k2[0],
                  attn_norm_w[0], w_out[0], ln_g[0], ln_b[0], depth)
```

```python
import functools
import math

import jax
import jax.numpy as jnp
from jax import lax
from jax.experimental import pallas as pl
from jax.experimental.pallas import tpu as pltpu

F32 = jnp.float32
BF16 = jnp.bfloat16

SSD_HEAD_DIM = 64
SSD_GROUPS = 2
SSD_STATE = 128
CONV_WIDTH = 4
CHUNK = 128
ATTN_QK_DIM = 64
ATTN_V_DIM = 2 * ATTN_QK_DIM
EPS = 1e-5
LAMBDA_INIT_L0 = 0.8 - 0.6 * math.exp(-0.3 * 0)
LOG2_E = math.log2(math.e)

V7X_LANES = 128
V7X_SUBLANES = 8
BF16_SUBLANE_PACK = 16
V7X_VMEM_BYTES = 64 * 1024 * 1024
V7X_VMEM_USABLE_BYTES = 56 * 1024 * 1024

TM_INPROJ = 512
TM_INPROJ_PASS = 256
TK_ATTN = 512
NSUB_ATTN = 2
NQ_ATTN = 4
ATTN_SCRATCH_SETS = 2
ATTN_PRODUCE_LEAD = 2
ATTN_BLOCKS_PER_TRIP = 4
TM_OUT = 1024
TM_OUT_PASS = 256
SSD_CHUNKS_PER_STEP = 4
TRANSPOSE_BLOCK_COLS = 512


def _vmem_limit(estimate_bytes):
    return int(min(V7X_VMEM_USABLE_BYTES, max(32 * 1024 * 1024, estimate_bytes)))


def _silu(x):
    hx = 0.5 * x
    return hx + hx * jnp.tanh(hx)


def _softplus(x):
    return jnp.maximum(x, 0.0) + jnp.log(1.0 + jnp.exp(-jnp.abs(x)))


def _split_bf16(a, pieces):
    out = []
    r = a
    for i in range(pieces):
        p = r.astype(BF16)
        out.append(p)
        if i + 1 < pieces:
            r = r - p.astype(F32)
    return out


def _dot(a, b):
    return jnp.dot(a, b, preferred_element_type=F32)


def _dot_nt(a, b):
    return lax.dot_general(a, b, (((1,), (1,)), ((), ())), preferred_element_type=F32)


def _dot_tn(a, b):
    return lax.dot_general(a, b, (((0,), (0,)), ((), ())), preferred_element_type=F32)


def _mod_kernel(c_ref, w_ref, b_ref, o_ref):
    c_hi, c_lo = _split_bf16(c_ref[...], 2)
    w_hi, w_lo = _split_bf16(w_ref[...], 2)
    o_ref[...] = _dot(c_hi, w_hi) + _dot(c_hi, w_lo) + _dot(c_lo, w_hi) + b_ref[...]


def _modulation(c, w_ada, b_ada):
    bsz, d = c.shape
    n = w_ada.shape[1]
    rows = V7X_SUBLANES
    c_pad = jnp.zeros((rows, d), F32).at[:bsz].set(c)
    bn = d
    out = pl.pallas_call(
        _mod_kernel,
        grid=(n // bn,),
        in_specs=[pl.BlockSpec((rows, d), lambda j: (0, 0)),
                  pl.BlockSpec((d, bn), lambda j: (0, j)),
                  pl.BlockSpec((1, bn), lambda j: (0, j))],
        out_specs=pl.BlockSpec((rows, bn), lambda j: (0, j)),
        out_shape=jax.ShapeDtypeStruct((rows, n), F32),
        name="modulation",
    )(c_pad, w_ada, b_ada.reshape(1, n))
    return out[:bsz]


def _transpose_kernel(w_ref, o_ref):
    o_ref[...] = w_ref[...].T.astype(BF16)


def _transpose_to_bf16(w):
    rows, cols = w.shape
    bc = TRANSPOSE_BLOCK_COLS
    assert cols % bc == 0
    return pl.pallas_call(
        _transpose_kernel, grid=(cols // bc,),
        in_specs=[pl.BlockSpec((rows, bc), lambda j: (0, j))],
        out_specs=pl.BlockSpec((bc, rows), lambda j: (j, 0)),
        out_shape=jax.ShapeDtypeStruct((cols, rows), BF16),
        name="weight_transpose",
    )(w)


def _inproj_kernel(x_ref, scale_ref, shift_ref, w_ref, wqvT_ref, wdt_ref, convw_ref,
                   convb_ref, dtb_ref,
                   zs_ref, xs_ref, bm_ref, cm_ref, qT_ref, k_ref, vT_ref, za_ref, dt_ref, dtT_ref,
                   carry_ref, *, tm, rows_per_pass, d_ssd, d_xbc, d_attn, n_bc):
    @pl.when(pl.program_id(1) == 0)
    def _():
        carry_ref[...] = jnp.zeros_like(carry_ref)

    off_z, off_xbc, off_k, off_za = 0, d_ssd, d_ssd + d_xbc, d_ssd + d_xbc + d_attn
    pad = V7X_SUBLANES
    nh = dt_ref.shape[-1]
    nh_attn = d_attn // ATTN_V_DIM
    cw = convw_ref[...]
    w_hi, w_lo = _split_bf16(wdt_ref[...], 2)
    wlane = lax.broadcasted_iota(jnp.int32, (1, V7X_LANES), 1)
    w_hilo = jnp.where(wlane < nh, w_hi, w_lo)

    tail = carry_ref[...]
    for r0 in range(0, tm, rows_per_pass):
        rows = slice(r0, r0 + rows_per_pass)
        h32 = x_ref[0, rows, :] * (1.0 + scale_ref[0]) + shift_ref[0]
        h = h32.astype(BF16)

        def proj(lo, width):
            return _dot(h, w_ref[:, lo:lo + width])

        u = proj(off_xbc, d_xbc)
        u_ext = jnp.concatenate([tail, u], axis=0)
        tail = u[rows_per_pass - pad:rows_per_pass, :]
        acc = convb_ref[...] + cw[CONV_WIDTH - 1:CONV_WIDTH, :] * u
        for tap in range(CONV_WIDTH - 1):
            shift = CONV_WIDTH - 1 - tap
            acc = acc + (cw[tap:tap + 1, :]
                         * pltpu.roll(u_ext, shift, 0)[pad:pad + rows_per_pass, :])
        xbc = _silu(acc)
        xs_ref[0, rows, :] = xbc[:, :d_ssd].astype(BF16)
        bm_ref[0, rows, :] = xbc[:, d_ssd:d_ssd + n_bc].astype(BF16)
        cm_ref[0, rows, :] = xbc[:, d_ssd + n_bc:d_ssd + 2 * n_bc].astype(BF16)

        zs_ref[0, rows, :] = _silu(proj(off_z, d_ssd)).astype(BF16)
        k_ref[0, rows, :] = proj(off_k, d_attn).astype(BF16)
        za_ref[0, rows, :] = _silu(proj(off_za, d_attn)).astype(BF16)

        qvT = _dot_nt(wqvT_ref[...], h)
        qT = qvT[:d_attn] * (ATTN_QK_DIM ** -0.5 * LOG2_E)
        qT = qT.astype(BF16).reshape(nh_attn, ATTN_V_DIM, rows_per_pass)
        feat = lax.broadcasted_iota(jnp.int32, (1, ATTN_V_DIM, 1), 1)
        zero = jnp.zeros_like(qT)
        qT_ref[0, :, 0, :, rows] = jnp.where(feat < ATTN_QK_DIM, qT, zero)
        qT_ref[0, :, 0, :, tm + r0:tm + r0 + rows_per_pass] = jnp.where(feat >= ATTN_QK_DIM, qT, zero)
        vT_ref[0, :, 0, :, rows] = qvT[d_attn:].astype(BF16).reshape(nh_attn, ATTN_V_DIM,
                                                                      rows_per_pass)

        h_lo = (h32 - h.astype(F32)).astype(BF16)
        r1 = _dot(h, w_hilo)
        r2 = _dot(h_lo, w_hi)
        dt_raw = r1 + pltpu.roll(r1, V7X_LANES - nh, 1) + r2
        dt = _softplus(dt_raw + dtb_ref[...])
        dt_ref[0, rows, :] = dt[:, :nh]
        dtT_ref[0, :, rows] = dt.T[:nh, :]
    carry_ref[...] = tail


def _in_projection(x, scale, shift, w_main, w_qv_t, w_dt, conv_w, conv_b, dt_bias, *, d_ssd, d_xbc,
                   d_attn):
    bsz, seq, d = x.shape
    tm = TM_INPROJ
    nh = w_dt.shape[1]
    nh_attn = d_attn // ATTN_V_DIM
    n_bc = (d_xbc - d_ssd) // 2
    n_main = w_main.shape[1]
    grid = (bsz, seq // tm)

    row = lambda b, l: (b, l, 0)
    head_major = lambda b, l: (b, 0, l, 0, 0)
    const2 = lambda b, l: (0, 0)
    per_b = lambda b, l: (b, 0, 0)
    single = pl.Buffered(1)
    in_specs = [
        pl.BlockSpec((1, tm, d), row),
        pl.BlockSpec((1, 1, d), per_b),
        pl.BlockSpec((1, 1, d), per_b),
        pl.BlockSpec((d, n_main), const2, pipeline_mode=single),
        pl.BlockSpec((2 * d_attn, d), const2, pipeline_mode=single),
        pl.BlockSpec((d, V7X_LANES), const2),
        pl.BlockSpec((CONV_WIDTH, d_xbc), const2),
        pl.BlockSpec((1, d_xbc), const2),
        pl.BlockSpec((1, V7X_LANES), const2),
    ]
    out_specs = [
        pl.BlockSpec((1, tm, d_ssd), row),
        pl.BlockSpec((1, tm, d_ssd), row),
        pl.BlockSpec((1, tm, n_bc), row),
        pl.BlockSpec((1, tm, n_bc), row),
        pl.BlockSpec((1, nh_attn, 1, ATTN_V_DIM, 2 * tm), head_major),
        pl.BlockSpec((1, tm, d_attn), row),
        pl.BlockSpec((1, nh_attn, 1, ATTN_V_DIM, tm), head_major),
        pl.BlockSpec((1, tm, d_attn), row),
        pl.BlockSpec((1, tm, nh), row),
        pl.BlockSpec((1, nh, tm), lambda b, l: (b, 0, l)),
    ]
    out_shape = [
        jax.ShapeDtypeStruct((bsz, seq, d_ssd), BF16),
        jax.ShapeDtypeStruct((bsz, seq, d_ssd), BF16),
        jax.ShapeDtypeStruct((bsz, seq, n_bc), BF16),
        jax.ShapeDtypeStruct((bsz, seq, n_bc), BF16),
        jax.ShapeDtypeStruct((bsz, nh_attn, seq // tm, ATTN_V_DIM, 2 * tm), BF16),
        jax.ShapeDtypeStruct((bsz, seq, d_attn), BF16),
        jax.ShapeDtypeStruct((bsz, nh_attn, seq // tm, ATTN_V_DIM, tm), BF16),
        jax.ShapeDtypeStruct((bsz, seq, d_attn), BF16),
        jax.ShapeDtypeStruct((bsz, seq, nh), F32),
        jax.ShapeDtypeStruct((bsz, nh, seq), F32),
    ]
    pad = V7X_SUBLANES
    vmem = (d * (n_main + 2 * d_attn) * 2 + 2 * tm * d * 4
            + 2 * tm * (2 * d_ssd + 2 * n_bc + 4 * d_attn) * 2
            + (tm + 2 * pad) * d_xbc * 4 + 6 * tm * d_xbc * 4)
    kern = functools.partial(_inproj_kernel, tm=tm, rows_per_pass=TM_INPROJ_PASS, d_ssd=d_ssd,
                             d_xbc=d_xbc, d_attn=d_attn, n_bc=n_bc)
    assert 2 * nh <= V7X_LANES
    w_dt_pad = jnp.zeros((d, V7X_LANES), F32).at[:, :nh].set(w_dt).at[:, nh:2 * nh].set(w_dt)
    dt_bias_pad = jnp.zeros((1, V7X_LANES), F32).at[0, :nh].set(dt_bias)
    return pl.pallas_call(
        kern, grid=grid, in_specs=in_specs, out_specs=out_specs, out_shape=out_shape,
        scratch_shapes=[pltpu.VMEM((pad, d_xbc), F32)],
        compiler_params=pltpu.CompilerParams(
            dimension_semantics=("arbitrary", "arbitrary"), vmem_limit_bytes=_vmem_limit(vmem)),
        name="in_projection",
    )(x, scale, shift, w_main, w_qv_t, w_dt_pad, conv_w, conv_b.reshape(1, -1), dt_bias_pad)


def _ssd_kernel(xs_ref, bm_ref, cm_ref, zs_ref, dt_ref, dtT_ref, alog_ref, alogT_ref, dskip_ref,
                normw_ref, y_ref, state_ref, ybuf_ref, *, nheads, d_ssd, cps):
    t = CHUNK
    n = SSD_STATE
    hp = SSD_HEAD_DIM
    gw = d_ssd // SSD_GROUPS
    heads_per_group = nheads // SSD_GROUPS

    @pl.when(pl.program_id(1) == 0)
    def _():
        state_ref[...] = jnp.zeros_like(state_ref)

    row = lax.broadcasted_iota(jnp.int32, (t, t), 0)
    col = lax.broadcasted_iota(jnp.int32, (t, t), 1)
    causal = col <= row
    tri = jnp.where(causal, 1.0, 0.0).astype(BF16)
    tri_t = jnp.where(row <= col, 1.0, 0.0).astype(BF16)
    eh = lax.broadcasted_iota(jnp.int32, (nheads, d_ssd), 0)
    ej = lax.broadcasted_iota(jnp.int32, (nheads, d_ssd), 1)
    expand_m = jnp.where(ej // hp == eh, 1.0, 0.0).astype(BF16)
    lane = lax.broadcasted_iota(jnp.int32, (1, 2 * hp), 1)
    neg_inf = jnp.float32(-jnp.inf)
    a = -jnp.exp(alog_ref[...])
    a_t = -jnp.exp(alogT_ref[...])

    def expand(v, pieces):
        return sum(_dot(p, expand_m) for p in _split_bf16(v, pieces))

    for ci in range(cps):
        rows = slice(ci * t, (ci + 1) * t)
        dt = dt_ref[0, rows, :]
        dt_t = dtT_ref[0, :, rows]
        acum = sum(_dot(tri, p) for p in _split_bf16(dt * a, 3))
        acum_t = sum(_dot(p, tri_t) for p in _split_bf16(dt_t * a_t, 3))
        last = acum[t - 1:t, :]
        dt_x = expand(dt, 1)
        w_x = expand(jnp.exp(last - acum), 1)
        e_x = expand(jnp.exp(acum), 2)

        xs = xs_ref[0, rows, :].astype(F32)
        xdt = xs * dt_x
        xdt_b = xdt.astype(BF16)
        xw_b = (xdt * w_x).astype(BF16)

        for g in range(SSD_GROUPS):
            b_g = bm_ref[0, rows, g * n:(g + 1) * n]
            c_g = cm_ref[0, rows, g * n:(g + 1) * n]
            gsl = slice(g * gw, (g + 1) * gw)
            cb = _dot_nt(c_g, b_g)
            prev = state_ref[g]
            y_off = _dot(c_g, prev.astype(BF16)) * e_x[:, gsl]
            state_ref[g] = prev * e_x[t - 1:t, gsl] + _dot_tn(b_g, xw_b[:, gsl])
            for pair in range(heads_per_group // 2):
                lo = g * gw + pair * 2 * hp
                x_pair = xdt_b[:, lo:lo + 2 * hp]
                decays = []
                for half in range(2):
                    hd = g * heads_per_group + pair * 2 + half
                    diff = acum[:, hd:hd + 1] - acum_t[hd:hd + 1, :]
                    decays.append((cb * jnp.exp(jnp.where(causal, diff, neg_inf))).astype(BF16))
                zero = jnp.zeros_like(x_pair)
                x_diag = jnp.concatenate([jnp.where(lane < hp, x_pair, zero),
                                          jnp.where(lane >= hp, x_pair, zero)], axis=0)
                y_pair = (y_off[:, pair * 2 * hp:(pair + 1) * 2 * hp]
                          + _dot(jnp.concatenate(decays, axis=1), x_diag))
                ybuf_ref[rows, lo:lo + 2 * hp] = y_pair

    xs = xs_ref[0].astype(F32)
    y = ybuf_ref[...] + dskip_ref[...] * xs
    y = y * zs_ref[0].astype(F32)
    for g in range(SSD_GROUPS):
        gsl = slice(g * gw, (g + 1) * gw)
        y_g = y[:, gsl]
        ms = jnp.mean(y_g * y_g, axis=-1, keepdims=True)
        y_ref[0, :, gsl] = (y_g * lax.rsqrt(ms + EPS) * normw_ref[:, gsl]).astype(BF16)


def _ssd(xs, bm, cm, zs, dt, dt_t, a_log, d_skip, ssd_norm_w):
    bsz, seq, d_ssd = xs.shape
    nheads = dt.shape[-1]
    n_bc = bm.shape[-1]
    cps = SSD_CHUNKS_PER_STEP
    t = cps * CHUNK
    assert seq % t == 0
    grid = (bsz, seq // t)
    row = lambda b, c: (b, c, 0)
    const2 = lambda b, c: (0, 0)
    in_specs = [
        pl.BlockSpec((1, t, d_ssd), row),
        pl.BlockSpec((1, t, n_bc), row),
        pl.BlockSpec((1, t, n_bc), row),
        pl.BlockSpec((1, t, d_ssd), row),
        pl.BlockSpec((1, t, nheads), row),
        pl.BlockSpec((1, nheads, t), lambda b, c: (b, 0, c)),
        pl.BlockSpec((1, nheads), const2),
        pl.BlockSpec((nheads, 1), const2),
        pl.BlockSpec((1, d_ssd), const2),
        pl.BlockSpec((1, d_ssd), const2),
    ]
    kern = functools.partial(_ssd_kernel, nheads=nheads, d_ssd=d_ssd, cps=cps)
    return pl.pallas_call(
        kern, grid=grid, in_specs=in_specs,
        out_specs=pl.BlockSpec((1, t, d_ssd), row),
        out_shape=jax.ShapeDtypeStruct((bsz, seq, d_ssd), BF16),
        scratch_shapes=[pltpu.VMEM((SSD_GROUPS, SSD_STATE, d_ssd // SSD_GROUPS), F32),
                        pltpu.VMEM((t, d_ssd), F32)],
        compiler_params=pltpu.CompilerParams(dimension_semantics=("arbitrary", "arbitrary")),
        name="ssd_scan",
    )(xs, bm, cm, zs, dt, dt_t, a_log.reshape(1, -1), a_log.reshape(-1, 1),
      jnp.repeat(d_skip, SSD_HEAD_DIM).reshape(1, -1), ssd_norm_w.reshape(1, -1))


def _attn_kernel(qT_ref, k_ref, vT_ref, za_ref, lq1_ref, lk1_ref, lq2_ref, lk2_ref, normw_ref,
                 o_ref, s_ref, cmax_ref, m_ref, acc_ref, *, tk, nsub, nq):
    step = pl.program_id(2)
    dk = ATTN_QK_DIM
    dv = 2 * dk
    sw = 2 * tk
    nchunk = 2 * nsub
    tq = nsub * tk
    lead = ATTN_PRODUCE_LEAD
    per_trip = ATTN_BLOCKS_PER_TRIP
    assert lead <= nchunk and per_trip % 2 == 0 and nsub % 2 == 0 and (nsub * nq) % per_trip == 0
    ones_rows = jnp.ones((BF16_SUBLANE_PACK, tk), BF16)
    lam = (jnp.exp(jnp.sum(lq1_ref[...] * lk1_ref[...], axis=-1, keepdims=True))
           - jnp.exp(jnp.sum(lq2_ref[...] * lk2_ref[...], axis=-1, keepdims=True))
           + LAMBDA_INIT_L0)

    def col(c):
        return slice(c * tk, (c + 1) * tk)

    def query_block(g):
        sc = g % ATTN_SCRATCH_SETS

        def produce(slot, kv, c):
            start = pl.multiple_of(kv * tk, tk)
            q_cols = qT_ref[0, 0, g * nsub + c // 2, :, (c % 2) * tk:(c % 2 + 1) * tk]
            s = _dot(k_ref[0, pl.ds(start, tk), :], q_cols)
            s_ref[sc, slot, :, col(c)] = s
            cmax_ref[sc, slot, :, col(c)] = jnp.max(s, axis=0, keepdims=True)

        def consume(slot, kv, c, diagonal=False):
            s = s_ref[sc, slot, :, col(c)]
            if diagonal:
                r = lax.broadcasted_iota(jnp.int32, s.shape, 0)
                qpos = lax.broadcasted_iota(jnp.int32, s.shape, 1)
                s = jnp.where(r <= qpos, s, -jnp.inf)
                cmax = jnp.max(s, axis=0, keepdims=True)
            else:
                cmax = cmax_ref[sc, slot, :, col(c)]
            m_prev = m_ref[sc, :, col(c)]
            m_new = jnp.maximum(m_prev, cmax)
            alpha = jnp.exp2(m_prev - m_new)
            p = jnp.exp2(s - m_new).astype(BF16)
            v_ext = jnp.concatenate([vT_ref[0, 0, kv], ones_rows], axis=0)
            acc_ref[sc, :, col(c)] = alpha * acc_ref[sc, :, col(c)] + _dot(v_ext, p)
            m_ref[sc, :, col(c)] = m_new

        def finalize(sub):
            c1 = sub * sw
            c2 = c1 + tk
            o1 = acc_ref[sc, 0:dv, c1:c1 + tk] / acc_ref[sc, dv:dv + 1, c1:c1 + tk]
            o2 = acc_ref[sc, 0:dv, c2:c2 + tk] / acc_ref[sc, dv:dv + 1, c2:c2 + tk]
            o = (o1 - lam * o2).T
            ms = jnp.mean(o * o, axis=-1, keepdims=True)
            o = o * lax.rsqrt(ms + EPS) * normw_ref[...] * (1.0 - LAMBDA_INIT_L0)
            rows = slice(g * tq + sub * tk, g * tq + (sub + 1) * tk)
            o_ref[0, rows, :] = (o * za_ref[0, rows, :].astype(F32)).astype(BF16)

        m_ref[sc] = jnp.full(m_ref.shape[1:], -jnp.inf, F32)
        acc_ref[sc] = jnp.zeros(acc_ref.shape[1:], F32)

        for c in range(lead):
            produce(0, 0, c)

        def off_diagonal(kv, nblocks):
            for idx in range(nblocks * nchunk):
                ahead = idx + lead
                produce((ahead // nchunk) % 2, kv + ahead // nchunk, ahead % nchunk)
                consume((idx // nchunk) % 2, kv + idx // nchunk, idx % nchunk)

        trips = (nsub * nq // per_trip) * step + (nsub * g) // per_trip
        left = (nsub * g) % per_trip

        def trip(i, carry):
            off_diagonal(i * per_trip, per_trip)
            return carry

        lax.fori_loop(0, trips, trip, 0)
        if left:
            off_diagonal(trips * per_trip, left)

        kv0 = trips * per_trip + left
        units = [(d, c) for d in range(nsub) for c in range(nchunk) if c // 2 >= d]
        assert units[:lead] == [(0, c) for c in range(lead)]
        for i, (d, c) in enumerate(units):
            if i + lead < len(units):
                nd, nc = units[i + lead]
                produce(nd % 2, kv0 + nd, nc)
            consume(d % 2, kv0 + d, c, diagonal=(c // 2 == d))
            if c % 2 == 1 and c // 2 == d:
                finalize(d)

    for g in range(nq):
        query_block(g)


def _attention(q_t, k, v_t, za, lq1, lk1, lq2, lk2, attn_norm_w):
    bsz, seq, d_attn = k.shape
    hw = ATTN_V_DIM
    nh = d_attn // hw
    tk = TK_ATTN
    nsub = NSUB_ATTN
    nq = NQ_ATTN
    tq = nsub * tk
    rows = nq * tq
    nblk = seq // tk
    assert nsub % 2 == 0 and seq % rows == 0
    assert q_t.shape == (bsz, nh, nblk, hw, 2 * tk) and v_t.shape == (bsz, nh, nblk, hw, tk)
    grid = (bsz, nh, seq // rows)
    qrow = lambda b, h, i: (b, i, h)
    const2 = lambda b, h, i: (0, 0)
    vec = pl.BlockSpec((1, ATTN_QK_DIM), const2)
    in_specs = [
        pl.BlockSpec((1, 1, nq * nsub, hw, 2 * tk), lambda b, h, i: (b, h, i, 0, 0)),
        pl.BlockSpec((1, seq, hw), lambda b, h, i: (b, 0, h)),
        pl.BlockSpec((1, 1, nblk, hw, tk), lambda b, h, i: (b, h, 0, 0, 0)),
        pl.BlockSpec((1, rows, hw), qrow),
        vec, vec, vec, vec,
        pl.BlockSpec((1, hw), const2),
    ]
    width = 2 * tq
    nscr = ATTN_SCRATCH_SETS
    assert nscr >= min(nq, 2)
    vmem = (2 * 2 * seq * hw * 2 + nscr * 2 * tk * width * 4
            + nscr * (2 * hw + BF16_SUBLANE_PACK) * width * 4 + 4 * tk * width * 4)
    return pl.pallas_call(
        functools.partial(_attn_kernel, tk=tk, nsub=nsub, nq=nq), grid=grid, in_specs=in_specs,
        out_specs=pl.BlockSpec((1, rows, hw), qrow),
        out_shape=jax.ShapeDtypeStruct((bsz, seq, d_attn), BF16),
        scratch_shapes=[pltpu.VMEM((nscr, 2, tk, width), F32),
                        pltpu.VMEM((nscr, 2, 1, width), F32), pltpu.VMEM((nscr, 1, width), F32),
                        pltpu.VMEM((nscr, hw + BF16_SUBLANE_PACK, width), F32)],
        compiler_params=pltpu.CompilerParams(
            dimension_semantics=("arbitrary", "arbitrary", "arbitrary"),
            vmem_limit_bytes=_vmem_limit(vmem)),
        name="diff_attention",
    )(q_t, k, v_t, za, lq1.reshape(1, -1), lk1.reshape(1, -1), lq2.reshape(1, -1),
      lk2.reshape(1, -1), attn_norm_w.reshape(1, -1))


def _outproj_kernel(ys_ref, ya_ref, w_ref, x_ref, gate_ref, g_ref, b_ref, o_ref, *, d_ssd, alpha,
                    tm, rows_per_pass):
    for r0 in range(0, tm, rows_per_pass):
        rows = slice(r0, r0 + rows_per_pass)
        mixed = (_dot(ys_ref[0, rows, :], w_ref[0:d_ssd, :])
                 + _dot(ya_ref[0, rows, :], w_ref[d_ssd:, :]))
        xf = alpha * x_ref[0, rows, :] + gate_ref[0] * mixed
        mu = jnp.mean(xf, axis=-1, keepdims=True)
        xc = xf - mu
        var = jnp.mean(xc * xc, axis=-1, keepdims=True)
        o_ref[0, rows, :] = xc * lax.rsqrt(var + EPS) * g_ref[...] + b_ref[...]


def _out_projection(y_ssd, y_attn, w_out, x, gate, ln_g, ln_b, alpha):
    bsz, seq, d = x.shape
    d_ssd = y_ssd.shape[-1]
    d_attn = y_attn.shape[-1]
    tm = TM_OUT
    row = lambda b, l: (b, l, 0)
    const2 = lambda b, l: (0, 0)
    in_specs = [
        pl.BlockSpec((1, tm, d_ssd), row),
        pl.BlockSpec((1, tm, d_attn), row),
        pl.BlockSpec((d_ssd + d_attn, d), const2, pipeline_mode=pl.Buffered(1)),
        pl.BlockSpec((1, tm, d), row),
        pl.BlockSpec((1, 1, d), lambda b, l: (b, 0, 0)),
        pl.BlockSpec((1, d), const2),
        pl.BlockSpec((1, d), const2),
    ]
    vmem = (d_ssd + d_attn) * d * 2 + 2 * tm * (d_ssd + d_attn) * 2 + 4 * tm * d * 4 + 4 * tm * d * 4
    return pl.pallas_call(
        functools.partial(_outproj_kernel, d_ssd=d_ssd, alpha=alpha, tm=tm,
                          rows_per_pass=TM_OUT_PASS),
        grid=(bsz, seq // tm), in_specs=in_specs,
        out_specs=pl.BlockSpec((1, tm, d), row),
        out_shape=jax.ShapeDtypeStruct((bsz, seq, d), F32),
        compiler_params=pltpu.CompilerParams(
            dimension_semantics=("arbitrary", "arbitrary"), vmem_limit_bytes=_vmem_limit(vmem)),
        name="out_projection",
    )(y_ssd, y_attn, w_out, x, gate, ln_g.reshape(1, -1), ln_b.reshape(1, -1))


def _layer(x, c, w_ada, b_ada, w_in, conv_w, conv_b, dt_bias, a_log, d_skip, ssd_norm_w,
           lambda_q1, lambda_k1, lambda_q2, lambda_k2, attn_norm_w, w_out, ln_g, ln_b, depth):
    bsz, seq, d = x.shape
    nheads = a_log.shape[0]
    d_ssd = nheads * SSD_HEAD_DIM
    d_xbc = conv_w.shape[1]
    d_attn = w_out.shape[0] - d_ssd
    assert d_xbc == d_ssd + 2 * SSD_GROUPS * SSD_STATE
    assert w_in.shape[1] == d_ssd + d_xbc + nheads + 4 * d_attn
    assert seq % TM_INPROJ == 0 and seq % TM_OUT == 0 and seq % CHUNK == 0
    assert TM_INPROJ == TK_ATTN, "the in-projection writes q^T / v^T in attention-sized time blocks"

    mod = _modulation(c, w_ada, b_ada)
    shift = mod[:, None, 0:d]
    scale = mod[:, None, d:2 * d]
    gate = mod[:, None, 2 * d:3 * d]

    off_dt = d_ssd + d_xbc
    off_q = off_dt + nheads
    off_k, off_v, off_za = off_q + d_attn, off_q + 2 * d_attn, off_q + 3 * d_attn
    w_main = jnp.concatenate([w_in[:, :off_dt], w_in[:, off_k:off_v], w_in[:, off_za:]],
                             axis=1).astype(BF16)
    w_qv_t = _transpose_to_bf16(jnp.concatenate([w_in[:, off_q:off_k], w_in[:, off_v:off_za]], axis=1))
    w_dt = w_in[:, off_dt:off_q]

    zs, xs, bm, cm, q_t, k, v_t, za, dt, dt_t = _in_projection(
        x, scale, shift, w_main, w_qv_t, w_dt, conv_w, conv_b, dt_bias,
        d_ssd=d_ssd, d_xbc=d_xbc, d_attn=d_attn)
    y_ssd = _ssd(xs, bm, cm, zs, dt, dt_t, a_log, d_skip, ssd_norm_w)
    y_attn = _attention(q_t, k, v_t, za, lambda_q1, lambda_k1, lambda_q2, lambda_k2, attn_norm_w)
    alpha = (2.0 * depth) ** 0.25
    return _out_projection(y_ssd, y_attn, w_out.astype(BF16), x, gate, ln_g, ln_b, alpha)


def kernel(x, c, w_ada, b_ada, w_in, conv_w, conv_b, dt_bias, a_log, d_skip, ssd_norm_w, lambda_q1,
           lambda_k1, lambda_q2, lambda_k2, attn_norm_w, w_out, ln_g, ln_b):
    depth = w_in.shape[0]
    assert depth == 1, "lambda_init and the layer loop are specialised to a single layer"
    return _layer(x, c, w_ada[0], b_ada[0], w_in[0], conv_w[0], conv_b[0], dt_bias[0], a_log[0],
                  d_skip[0], ssd_norm_w[0], lambda_q1[0], lambda_k1[0], lambda_q2[0], lambda_k2[0],
                  attn_norm_w[0], w_out[0], ln_g[0], ln_b[0], depth)
```

```python
import functools
import math

import jax
import jax.numpy as jnp
from jax import lax
from jax.experimental import pallas as pl
from jax.experimental.pallas import tpu as pltpu

F32 = jnp.float32
BF16 = jnp.bfloat16

SSD_HEAD_DIM = 64
SSD_GROUPS = 2
SSD_STATE = 128
CONV_WIDTH = 4
CHUNK = 128
ATTN_QK_DIM = 64
ATTN_V_DIM = 2 * ATTN_QK_DIM
EPS = 1e-5
LAMBDA_INIT_L0 = 0.8 - 0.6 * math.exp(-0.3 * 0)
LOG2_E = math.log2(math.e)

V7X_LANES = 128
V7X_SUBLANES = 8
BF16_SUBLANE_PACK = 16
V7X_VMEM_BYTES = 64 * 1024 * 1024
V7X_VMEM_USABLE_BYTES = 56 * 1024 * 1024

TM_INPROJ = 512
TM_INPROJ_PASS = 256
TK_ATTN = 512
NSUB_ATTN = 2
NQ_ATTN = 4
ATTN_SCRATCH_SETS = 2
ATTN_PRODUCE_LEAD = 2
ATTN_BLOCKS_PER_TRIP = 4
TM_OUT = 1024
TM_OUT_PASS = 256
SSD_CHUNKS_PER_STEP = 4


def _vmem_limit(estimate_bytes):
    return int(min(V7X_VMEM_USABLE_BYTES, max(32 * 1024 * 1024, estimate_bytes)))


def _silu(x):
    hx = 0.5 * x
    return hx + hx * jnp.tanh(hx)


def _softplus(x):
    return jnp.maximum(x, 0.0) + jnp.log(1.0 + jnp.exp(-jnp.abs(x)))


def _split_bf16(a, pieces):
    out = []
    r = a
    for i in range(pieces):
        p = r.astype(BF16)
        out.append(p)
        if i + 1 < pieces:
            r = r - p.astype(F32)
    return out


def _dot(a, b):
    return jnp.dot(a, b, preferred_element_type=F32)


def _dot_nt(a, b):
    return lax.dot_general(a, b, (((1,), (1,)), ((), ())), preferred_element_type=F32)


def _dot_tn(a, b):
    return lax.dot_general(a, b, (((0,), (0,)), ((), ())), preferred_element_type=F32)


def _mod_kernel(c_ref, w_ref, b_ref, o_ref):
    c_hi, c_lo = _split_bf16(c_ref[...], 2)
    w_hi, w_lo = _split_bf16(w_ref[...], 2)
    o_ref[...] = _dot(c_hi, w_hi) + _dot(c_hi, w_lo) + _dot(c_lo, w_hi) + b_ref[...]


def _modulation(c, w_ada, b_ada):
    bsz, d = c.shape
    n = w_ada.shape[1]
    rows = V7X_SUBLANES
    c_pad = jnp.zeros((rows, d), F32).at[:bsz].set(c)
    bn = d
    out = pl.pallas_call(
        _mod_kernel,
        grid=(n // bn,),
        in_specs=[pl.BlockSpec((rows, d), lambda j: (0, 0)),
                  pl.BlockSpec((d, bn), lambda j: (0, j)),
                  pl.BlockSpec((1, bn), lambda j: (0, j))],
        out_specs=pl.BlockSpec((rows, bn), lambda j: (0, j)),
        out_shape=jax.ShapeDtypeStruct((rows, n), F32),
        name="modulation",
    )(c_pad, w_ada, b_ada.reshape(1, n))
    return out[:bsz]


def _inproj_kernel(x_ref, scale_ref, shift_ref, w_ref, wqvT_ref, wdt_ref, convw_ref,
                   convb_ref, dtb_ref,
                   zs_ref, xs_ref, bm_ref, cm_ref, qT_ref, k_ref, vT_ref, za_ref, dt_ref, dtT_ref,
                   carry_ref, *, tm, rows_per_pass, d_ssd, d_xbc, d_attn, n_bc):
    @pl.when(pl.program_id(1) == 0)
    def _():
        carry_ref[...] = jnp.zeros_like(carry_ref)

    off_z, off_xbc, off_k, off_za = 0, d_ssd, d_ssd + d_xbc, d_ssd + d_xbc + d_attn
    pad = V7X_SUBLANES
    nh = dt_ref.shape[-1]
    nh_attn = d_attn // ATTN_V_DIM
    cw = convw_ref[...]
    w_hi, w_lo = _split_bf16(wdt_ref[...], 2)
    wrow = lax.broadcasted_iota(jnp.int32, (V7X_LANES, 1), 0)
    w_hilo = jnp.where(wrow < nh, w_hi, w_lo)

    tail = carry_ref[...]
    for r0 in range(0, tm, rows_per_pass):
        rows = slice(r0, r0 + rows_per_pass)
        h32 = x_ref[0, rows, :] * (1.0 + scale_ref[0]) + shift_ref[0]
        h = h32.astype(BF16)

        def proj(lo, width):
            return _dot(h, w_ref[:, lo:lo + width])

        u = proj(off_xbc, d_xbc)
        u_ext = jnp.concatenate([tail, u], axis=0)
        tail = u[rows_per_pass - pad:rows_per_pass, :]
        acc = convb_ref[...] + cw[CONV_WIDTH - 1:CONV_WIDTH, :] * u
        for tap in range(CONV_WIDTH - 1):
            shift = CONV_WIDTH - 1 - tap
            acc = acc + (cw[tap:tap + 1, :]
                         * pltpu.roll(u_ext, shift, 0)[pad:pad + rows_per_pass, :])
        xbc = _silu(acc)
        xs_ref[0, rows, :] = xbc[:, :d_ssd].astype(BF16)
        bm_ref[0, rows, :] = xbc[:, d_ssd:d_ssd + n_bc].astype(BF16)
        cm_ref[0, rows, :] = xbc[:, d_ssd + n_bc:d_ssd + 2 * n_bc].astype(BF16)

        zs_ref[0, rows, :] = _silu(proj(off_z, d_ssd)).astype(BF16)
        k_ref[0, rows, :] = proj(off_k, d_attn).astype(BF16)
        za_ref[0, rows, :] = _silu(proj(off_za, d_attn)).astype(BF16)

        qvT = _dot_nt(wqvT_ref[...], h)
        qT = qvT[:d_attn] * (ATTN_QK_DIM ** -0.5 * LOG2_E)
        qT = qT.astype(BF16).reshape(nh_attn, ATTN_V_DIM, rows_per_pass)
        feat = lax.broadcasted_iota(jnp.int32, (1, ATTN_V_DIM, 1), 1)
        zero = jnp.zeros_like(qT)
        qT_ref[0, :, 0, :, rows] = jnp.where(feat < ATTN_QK_DIM, qT, zero)
        qT_ref[0, :, 0, :, tm + r0:tm + r0 + rows_per_pass] = jnp.where(feat >= ATTN_QK_DIM, qT, zero)
        vT_ref[0, :, 0, :, rows] = qvT[d_attn:].astype(BF16).reshape(nh_attn, ATTN_V_DIM,
                                                                      rows_per_pass)

        h_lo = (h32 - h.astype(F32)).astype(BF16)
        r1 = _dot_nt(h, w_hilo)
        r2 = _dot_nt(h_lo, w_hi)
        dt_raw = r1 + pltpu.roll(r1, V7X_LANES - nh, 1) + r2
        dt = _softplus(dt_raw + dtb_ref[...])
        dt_ref[0, rows, :] = dt[:, :nh]
        dtT_ref[0, :, rows] = dt.T[:nh, :]
    carry_ref[...] = tail


def _in_projection(x, scale, shift, w_main, w_qv_t, w_dt, conv_w, conv_b, dt_bias, *, d_ssd, d_xbc,
                   d_attn):
    bsz, seq, d = x.shape
    tm = TM_INPROJ
    nh = w_dt.shape[0]
    nh_attn = d_attn // ATTN_V_DIM
    n_bc = (d_xbc - d_ssd) // 2
    n_main = w_main.shape[1]
    grid = (bsz, seq // tm)

    row = lambda b, l: (b, l, 0)
    head_major = lambda b, l: (b, 0, l, 0, 0)
    const2 = lambda b, l: (0, 0)
    per_b = lambda b, l: (b, 0, 0)
    single = pl.Buffered(1)
    in_specs = [
        pl.BlockSpec((1, tm, d), row),
        pl.BlockSpec((1, 1, d), per_b),
        pl.BlockSpec((1, 1, d), per_b),
        pl.BlockSpec((d, n_main), const2, pipeline_mode=single),
        pl.BlockSpec((2 * d_attn, d), const2, pipeline_mode=single),
        pl.BlockSpec((V7X_LANES, d), const2),
        pl.BlockSpec((CONV_WIDTH, d_xbc), const2),
        pl.BlockSpec((1, d_xbc), const2),
        pl.BlockSpec((1, V7X_LANES), const2),
    ]
    out_specs = [
        pl.BlockSpec((1, tm, d_ssd), row),
        pl.BlockSpec((1, tm, d_ssd), row),
        pl.BlockSpec((1, tm, n_bc), row),
        pl.BlockSpec((1, tm, n_bc), row),
        pl.BlockSpec((1, nh_attn, 1, ATTN_V_DIM, 2 * tm), head_major),
        pl.BlockSpec((1, tm, d_attn), row),
        pl.BlockSpec((1, nh_attn, 1, ATTN_V_DIM, tm), head_major),
        pl.BlockSpec((1, tm, d_attn), row),
        pl.BlockSpec((1, tm, nh), row),
        pl.BlockSpec((1, nh, tm), lambda b, l: (b, 0, l)),
    ]
    out_shape = [
        jax.ShapeDtypeStruct((bsz, seq, d_ssd), BF16),
        jax.ShapeDtypeStruct((bsz, seq, d_ssd), BF16),
        jax.ShapeDtypeStruct((bsz, seq, n_bc), BF16),
        jax.ShapeDtypeStruct((bsz, seq, n_bc), BF16),
        jax.ShapeDtypeStruct((bsz, nh_attn, seq // tm, ATTN_V_DIM, 2 * tm), BF16),
        jax.ShapeDtypeStruct((bsz, seq, d_attn), BF16),
        jax.ShapeDtypeStruct((bsz, nh_attn, seq // tm, ATTN_V_DIM, tm), BF16),
        jax.ShapeDtypeStruct((bsz, seq, d_attn), BF16),
        jax.ShapeDtypeStruct((bsz, seq, nh), F32),
        jax.ShapeDtypeStruct((bsz, nh, seq), F32),
    ]
    pad = V7X_SUBLANES
    vmem = (d * (n_main + 2 * d_attn) * 2 + 2 * tm * d * 4
            + 2 * tm * (2 * d_ssd + 2 * n_bc + 4 * d_attn) * 2
            + (tm + 2 * pad) * d_xbc * 4 + 6 * tm * d_xbc * 4)
    kern = functools.partial(_inproj_kernel, tm=tm, rows_per_pass=TM_INPROJ_PASS, d_ssd=d_ssd,
                             d_xbc=d_xbc, d_attn=d_attn, n_bc=n_bc)
    assert 2 * nh <= V7X_LANES
    w_dt_pad = jnp.concatenate([w_dt, w_dt, jnp.zeros((V7X_LANES - 2 * nh, d), F32)], axis=0)
    dt_bias_pad = jnp.concatenate([dt_bias, jnp.zeros((V7X_LANES - nh,), F32)]).reshape(1, -1)
    return pl.pallas_call(
        kern, grid=grid, in_specs=in_specs, out_specs=out_specs, out_shape=out_shape,
        scratch_shapes=[pltpu.VMEM((pad, d_xbc), F32)],
        compiler_params=pltpu.CompilerParams(
            dimension_semantics=("arbitrary", "arbitrary"), vmem_limit_bytes=_vmem_limit(vmem)),
        name="in_projection",
    )(x, scale, shift, w_main, w_qv_t, w_dt_pad, conv_w, conv_b.reshape(1, -1), dt_bias_pad)


def _ssd_kernel(xs_ref, bm_ref, cm_ref, zs_ref, dt_ref, dtT_ref, alog_ref, alogT_ref, dskip_ref,
                normw_ref, y_ref, state_ref, ybuf_ref, *, nheads, d_ssd, cps):
    t = CHUNK
    n = SSD_STATE
    hp = SSD_HEAD_DIM
    gw = d_ssd // SSD_GROUPS
    heads_per_group = nheads // SSD_GROUPS

    @pl.when(pl.program_id(1) == 0)
    def _():
        state_ref[...] = jnp.zeros_like(state_ref)

    row = lax.broadcasted_iota(jnp.int32, (t, t), 0)
    col = lax.broadcasted_iota(jnp.int32, (t, t), 1)
    causal = col <= row
    tri = jnp.where(causal, 1.0, 0.0).astype(BF16)
    tri_t = jnp.where(row <= col, 1.0, 0.0).astype(BF16)
    eh = lax.broadcasted_iota(jnp.int32, (nheads, d_ssd), 0)
    ej = lax.broadcasted_iota(jnp.int32, (nheads, d_ssd), 1)
    expand_m = jnp.where(ej // hp == eh, 1.0, 0.0).astype(BF16)
    lane = lax.broadcasted_iota(jnp.int32, (1, 2 * hp), 1)
    neg_inf = jnp.float32(-jnp.inf)
    a = -jnp.exp(alog_ref[...])
    a_t = -jnp.exp(alogT_ref[...])

    def expand(v, pieces):
        return sum(_dot(p, expand_m) for p in _split_bf16(v, pieces))

    for ci in range(cps):
        rows = slice(ci * t, (ci + 1) * t)
        dt = dt_ref[0, rows, :]
        dt_t = dtT_ref[0, :, rows]
        acum = sum(_dot(tri, p) for p in _split_bf16(dt * a, 3))
        acum_t = sum(_dot(p, tri_t) for p in _split_bf16(dt_t * a_t, 3))
        last = acum[t - 1:t, :]
        dt_x = expand(dt, 1)
        w_x = expand(jnp.exp(last - acum), 1)
        e_x = expand(jnp.exp(acum), 2)

        xs = xs_ref[0, rows, :].astype(F32)
        xdt = xs * dt_x
        xdt_b = xdt.astype(BF16)
        xw_b = (xdt * w_x).astype(BF16)

        for g in range(SSD_GROUPS):
            b_g = bm_ref[0, rows, g * n:(g + 1) * n]
            c_g = cm_ref[0, rows, g * n:(g + 1) * n]
            gsl = slice(g * gw, (g + 1) * gw)
            cb = _dot_nt(c_g, b_g)
            prev = state_ref[g]
            y_off = _dot(c_g, prev.astype(BF16)) * e_x[:, gsl]
            state_ref[g] = prev * e_x[t - 1:t, gsl] + _dot_tn(b_g, xw_b[:, gsl])
            for pair in range(heads_per_group // 2):
                lo = g * gw + pair * 2 * hp
                x_pair = xdt_b[:, lo:lo + 2 * hp]
                decays = []
                for half in range(2):
                    hd = g * heads_per_group + pair * 2 + half
                    diff = acum[:, hd:hd + 1] - acum_t[hd:hd + 1, :]
                    decays.append((cb * jnp.exp(jnp.where(causal, diff, neg_inf))).astype(BF16))
                zero = jnp.zeros_like(x_pair)
                x_diag = jnp.concatenate([jnp.where(lane < hp, x_pair, zero),
                                          jnp.where(lane >= hp, x_pair, zero)], axis=0)
                y_pair = (y_off[:, pair * 2 * hp:(pair + 1) * 2 * hp]
                          + _dot(jnp.concatenate(decays, axis=1), x_diag))
                ybuf_ref[rows, lo:lo + 2 * hp] = y_pair

    xs = xs_ref[0].astype(F32)
    y = ybuf_ref[...] + dskip_ref[...] * xs
    y = y * zs_ref[0].astype(F32)
    for g in range(SSD_GROUPS):
        gsl = slice(g * gw, (g + 1) * gw)
        y_g = y[:, gsl]
        ms = jnp.mean(y_g * y_g, axis=-1, keepdims=True)
        y_ref[0, :, gsl] = (y_g * lax.rsqrt(ms + EPS) * normw_ref[:, gsl]).astype(BF16)


def _ssd(xs, bm, cm, zs, dt, dt_t, a_log, d_skip, ssd_norm_w):
    bsz, seq, d_ssd = xs.shape
    nheads = dt.shape[-1]
    n_bc = bm.shape[-1]
    cps = SSD_CHUNKS_PER_STEP
    t = cps * CHUNK
    assert seq % t == 0
    grid = (bsz, seq // t)
    row = lambda b, c: (b, c, 0)
    const2 = lambda b, c: (0, 0)
    in_specs = [
        pl.BlockSpec((1, t, d_ssd), row),
        pl.BlockSpec((1, t, n_bc), row),
        pl.BlockSpec((1, t, n_bc), row),
        pl.BlockSpec((1, t, d_ssd), row),
        pl.BlockSpec((1, t, nheads), row),
        pl.BlockSpec((1, nheads, t), lambda b, c: (b, 0, c)),
        pl.BlockSpec((1, nheads), const2),
        pl.BlockSpec((nheads, 1), const2),
        pl.BlockSpec((1, d_ssd), const2),
        pl.BlockSpec((1, d_ssd), const2),
    ]
    kern = functools.partial(_ssd_kernel, nheads=nheads, d_ssd=d_ssd, cps=cps)
    return pl.pallas_call(
        kern, grid=grid, in_specs=in_specs,
        out_specs=pl.BlockSpec((1, t, d_ssd), row),
        out_shape=jax.ShapeDtypeStruct((bsz, seq, d_ssd), BF16),
        scratch_shapes=[pltpu.VMEM((SSD_GROUPS, SSD_STATE, d_ssd // SSD_GROUPS), F32),
                        pltpu.VMEM((t, d_ssd), F32)],
        compiler_params=pltpu.CompilerParams(dimension_semantics=("arbitrary", "arbitrary")),
        name="ssd_scan",
    )(xs, bm, cm, zs, dt, dt_t, a_log.reshape(1, -1), a_log.reshape(-1, 1),
      jnp.repeat(d_skip, SSD_HEAD_DIM).reshape(1, -1), ssd_norm_w.reshape(1, -1))


def _attn_kernel(qT_ref, k_ref, vT_ref, za_ref, lq1_ref, lk1_ref, lq2_ref, lk2_ref, normw_ref,
                 o_ref, s_ref, cmax_ref, m_ref, acc_ref, *, tk, nsub, nq):
    step = pl.program_id(2)
    dk = ATTN_QK_DIM
    dv = 2 * dk
    sw = 2 * tk
    nchunk = 2 * nsub
    tq = nsub * tk
    lead = ATTN_PRODUCE_LEAD
    per_trip = ATTN_BLOCKS_PER_TRIP
    assert lead <= nchunk and per_trip % 2 == 0 and nsub % 2 == 0 and (nsub * nq) % per_trip == 0
    ones_rows = jnp.ones((BF16_SUBLANE_PACK, tk), BF16)
    lam = (jnp.exp(jnp.sum(lq1_ref[...] * lk1_ref[...], axis=-1, keepdims=True))
           - jnp.exp(jnp.sum(lq2_ref[...] * lk2_ref[...], axis=-1, keepdims=True))
           + LAMBDA_INIT_L0)

    def col(c):
        return slice(c * tk, (c + 1) * tk)

    def query_block(g):
        sc = g % ATTN_SCRATCH_SETS

        def produce(slot, kv, c):
            start = pl.multiple_of(kv * tk, tk)
            q_cols = qT_ref[0, 0, g * nsub + c // 2, :, (c % 2) * tk:(c % 2 + 1) * tk]
            s = _dot(k_ref[0, pl.ds(start, tk), :], q_cols)
            s_ref[sc, slot, :, col(c)] = s
            cmax_ref[sc, slot, :, col(c)] = jnp.max(s, axis=0, keepdims=True)

        def consume(slot, kv, c, diagonal=False):
            s = s_ref[sc, slot, :, col(c)]
            if diagonal:
                r = lax.broadcasted_iota(jnp.int32, s.shape, 0)
                qpos = lax.broadcasted_iota(jnp.int32, s.shape, 1)
                s = jnp.where(r <= qpos, s, -jnp.inf)
                cmax = jnp.max(s, axis=0, keepdims=True)
            else:
                cmax = cmax_ref[sc, slot, :, col(c)]
            m_prev = m_ref[sc, :, col(c)]
            m_new = jnp.maximum(m_prev, cmax)
            alpha = jnp.exp2(m_prev - m_new)
            p = jnp.exp2(s - m_new).astype(BF16)
            v_ext = jnp.concatenate([vT_ref[0, 0, kv], ones_rows], axis=0)
            acc_ref[sc, :, col(c)] = alpha * acc_ref[sc, :, col(c)] + _dot(v_ext, p)
            m_ref[sc, :, col(c)] = m_new

        def finalize(sub):
            c1 = sub * sw
            c2 = c1 + tk
            o1 = acc_ref[sc, 0:dv, c1:c1 + tk] / acc_ref[sc, dv:dv + 1, c1:c1 + tk]
            o2 = acc_ref[sc, 0:dv, c2:c2 + tk] / acc_ref[sc, dv:dv + 1, c2:c2 + tk]
            o = (o1 - lam * o2).T
            ms = jnp.mean(o * o, axis=-1, keepdims=True)
            o = o * lax.rsqrt(ms + EPS) * normw_ref[...] * (1.0 - LAMBDA_INIT_L0)
            rows = slice(g * tq + sub * tk, g * tq + (sub + 1) * tk)
            o_ref[0, rows, :] = (o * za_ref[0, rows, :].astype(F32)).astype(BF16)

        m_ref[sc] = jnp.full(m_ref.shape[1:], -jnp.inf, F32)
        acc_ref[sc] = jnp.zeros(acc_ref.shape[1:], F32)

        for c in range(lead):
            produce(0, 0, c)

        def off_diagonal(kv, nblocks):
            for idx in range(nblocks * nchunk):
                ahead = idx + lead
                produce((ahead // nchunk) % 2, kv + ahead // nchunk, ahead % nchunk)
                consume((idx // nchunk) % 2, kv + idx // nchunk, idx % nchunk)

        trips = (nsub * nq // per_trip) * step + (nsub * g) // per_trip
        left = (nsub * g) % per_trip

        def trip(i, carry):
            off_diagonal(i * per_trip, per_trip)
            return carry

        lax.fori_loop(0, trips, trip, 0)
        if left:
            off_diagonal(trips * per_trip, left)

        kv0 = trips * per_trip + left
        units = [(d, c) for d in range(nsub) for c in range(nchunk) if c // 2 >= d]
        assert units[:lead] == [(0, c) for c in range(lead)]
        for i, (d, c) in enumerate(units):
            if i + lead < len(units):
                nd, nc = units[i + lead]
                produce(nd % 2, kv0 + nd, nc)
            consume(d % 2, kv0 + d, c, diagonal=(c // 2 == d))
            if c % 2 == 1 and c // 2 == d:
                finalize(d)

    for g in range(nq):
        query_block(g)


def _attention(q_t, k, v_t, za, lq1, lk1, lq2, lk2, attn_norm_w):
    bsz, seq, d_attn = k.shape
    hw = ATTN_V_DIM
    nh = d_attn // hw
    tk = TK_ATTN
    nsub = NSUB_ATTN
    nq = NQ_ATTN
    tq = nsub * tk
    rows = nq * tq
    nblk = seq // tk
    assert nsub % 2 == 0 and seq % rows == 0
    assert q_t.shape == (bsz, nh, nblk, hw, 2 * tk) and v_t.shape == (bsz, nh, nblk, hw, tk)
    grid = (bsz, nh, seq // rows)
    qrow = lambda b, h, i: (b, i, h)
    const2 = lambda b, h, i: (0, 0)
    vec = pl.BlockSpec((1, ATTN_QK_DIM), const2)
    in_specs = [
        pl.BlockSpec((1, 1, nq * nsub, hw, 2 * tk), lambda b, h, i: (b, h, i, 0, 0)),
        pl.BlockSpec((1, seq, hw), lambda b, h, i: (b, 0, h)),
        pl.BlockSpec((1, 1, nblk, hw, tk), lambda b, h, i: (b, h, 0, 0, 0)),
        pl.BlockSpec((1, rows, hw), qrow),
        vec, vec, vec, vec,
        pl.BlockSpec((1, hw), const2),
    ]
    width = 2 * tq
    nscr = ATTN_SCRATCH_SETS
    assert nscr >= min(nq, 2)
    vmem = (2 * 2 * seq * hw * 2 + nscr * 2 * tk * width * 4
            + nscr * (2 * hw + BF16_SUBLANE_PACK) * width * 4 + 4 * tk * width * 4)
    return pl.pallas_call(
        functools.partial(_attn_kernel, tk=tk, nsub=nsub, nq=nq), grid=grid, in_specs=in_specs,
        out_specs=pl.BlockSpec((1, rows, hw), qrow),
        out_shape=jax.ShapeDtypeStruct((bsz, seq, d_attn), BF16),
        scratch_shapes=[pltpu.VMEM((nscr, 2, tk, width), F32),
                        pltpu.VMEM((nscr, 2, 1, width), F32), pltpu.VMEM((nscr, 1, width), F32),
                        pltpu.VMEM((nscr, hw + BF16_SUBLANE_PACK, width), F32)],
        compiler_params=pltpu.CompilerParams(
            dimension_semantics=("arbitrary", "arbitrary", "arbitrary"),
            vmem_limit_bytes=_vmem_limit(vmem)),
        name="diff_attention",
    )(q_t, k, v_t, za, lq1.reshape(1, -1), lk1.reshape(1, -1), lq2.reshape(1, -1),
      lk2.reshape(1, -1), attn_norm_w.reshape(1, -1))


def _outproj_kernel(ys_ref, ya_ref, w_ref, x_ref, gate_ref, g_ref, b_ref, o_ref, *, d_ssd, alpha,
                    tm, rows_per_pass):
    for r0 in range(0, tm, rows_per_pass):
        rows = slice(r0, r0 + rows_per_pass)
        mixed = (_dot(ys_ref[0, rows, :], w_ref[0:d_ssd, :])
                 + _dot(ya_ref[0, rows, :], w_ref[d_ssd:, :]))
        xf = alpha * x_ref[0, rows, :] + gate_ref[0] * mixed
        mu = jnp.mean(xf, axis=-1, keepdims=True)
        xc = xf - mu
        var = jnp.mean(xc * xc, axis=-1, keepdims=True)
        o_ref[0, rows, :] = xc * lax.rsqrt(var + EPS) * g_ref[...] + b_ref[...]


def _out_projection(y_ssd, y_attn, w_out, x, gate, ln_g, ln_b, alpha):
    bsz, seq, d = x.shape
    d_ssd = y_ssd.shape[-1]
    d_attn = y_attn.shape[-1]
    tm = TM_OUT
    row = lambda b, l: (b, l, 0)
    const2 = lambda b, l: (0, 0)
    in_specs = [
        pl.BlockSpec((1, tm, d_ssd), row),
        pl.BlockSpec((1, tm, d_attn), row),
        pl.BlockSpec((d_ssd + d_attn, d), const2, pipeline_mode=pl.Buffered(1)),
        pl.BlockSpec((1, tm, d), row),
        pl.BlockSpec((1, 1, d), lambda b, l: (b, 0, 0)),
        pl.BlockSpec((1, d), const2),
        pl.BlockSpec((1, d), const2),
    ]
    vmem = (d_ssd + d_attn) * d * 2 + 2 * tm * (d_ssd + d_attn) * 2 + 4 * tm * d * 4 + 4 * tm * d * 4
    return pl.pallas_call(
        functools.partial(_outproj_kernel, d_ssd=d_ssd, alpha=alpha, tm=tm,
                          rows_per_pass=TM_OUT_PASS),
        grid=(bsz, seq // tm), in_specs=in_specs,
        out_specs=pl.BlockSpec((1, tm, d), row),
        out_shape=jax.ShapeDtypeStruct((bsz, seq, d), F32),
        compiler_params=pltpu.CompilerParams(
            dimension_semantics=("arbitrary", "arbitrary"), vmem_limit_bytes=_vmem_limit(vmem)),
        name="out_projection",
    )(y_ssd, y_attn, w_out, x, gate, ln_g.reshape(1, -1), ln_b.reshape(1, -1))


def _layer(x, c, w_ada, b_ada, w_in, conv_w, conv_b, dt_bias, a_log, d_skip, ssd_norm_w,
           lambda_q1, lambda_k1, lambda_q2, lambda_k2, attn_norm_w, w_out, ln_g, ln_b, depth):
    bsz, seq, d = x.shape
    nheads = a_log.shape[0]
    d_ssd = nheads * SSD_HEAD_DIM
    d_xbc = conv_w.shape[1]
    d_attn = w_out.shape[0] - d_ssd
    assert d_xbc == d_ssd + 2 * SSD_GROUPS * SSD_STATE
    assert w_in.shape[1] == d_ssd + d_xbc + nheads + 4 * d_attn
    assert seq % TM_INPROJ == 0 and seq % TM_OUT == 0 and seq % CHUNK == 0
    assert TM_INPROJ == TK_ATTN, "the in-projection writes q^T / v^T in attention-sized time blocks"

    mod = _modulation(c, w_ada, b_ada)
    shift = mod[:, None, 0:d]
    scale = mod[:, None, d:2 * d]
    gate = mod[:, None, 2 * d:3 * d]

    off_dt = d_ssd + d_xbc
    off_q = off_dt + nheads
    off_k, off_v, off_za = off_q + d_attn, off_q + 2 * d_attn, off_q + 3 * d_attn
    w_t = w_in.T
    w_main = jnp.concatenate([w_t[:off_dt], w_t[off_k:off_v], w_t[off_za:]], axis=0).astype(BF16).T
    w_qv_t = jnp.concatenate([w_t[off_q:off_k], w_t[off_v:off_za]], axis=0).astype(BF16)
    w_dt = w_t[off_dt:off_q]

    zs, xs, bm, cm, q_t, k, v_t, za, dt, dt_t = _in_projection(
        x, scale, shift, w_main, w_qv_t, w_dt, conv_w, conv_b, dt_bias,
        d_ssd=d_ssd, d_xbc=d_xbc, d_attn=d_attn)
    y_ssd = _ssd(xs, bm, cm, zs, dt, dt_t, a_log, d_skip, ssd_norm_w)
    y_attn = _attention(q_t, k, v_t, za, lambda_q1, lambda_k1, lambda_q2, lambda_k2, attn_norm_w)
    alpha = (2.0 * depth) ** 0.25
    return _out_projection(y_ssd, y_attn, w_out.astype(BF16), x, gate, ln_g, ln_b, alpha)


def kernel(x, c, w_ada, b_ada, w_in, conv_w, conv_b, dt_bias, a_log, d_skip, ssd_norm_w, lambda_q1,
           lambda_k1, lambda_q2, lambda_k2, attn_norm_w, w_out, ln_g, ln_b):
    depth = w_in.shape[0]
    assert depth == 1, "lambda_init and the layer loop are specialised to a single layer"
    return _layer(x, c, w_ada[0], b_ada[0], w_in[0], conv_w[0], conv_b[0], dt_bias[0], a_log[0],
                  d_skip[0], ssd_norm_w[0], lambda_q1[0], lambda_k1[0], lambda_q2[0], lambda_k2[0],
                  attn_norm_w[0], w_out[0], ln_g[0], ln_b[0], depth)
```

```python
import functools
import math

import jax
import jax.numpy as jnp
from jax import lax
from jax.experimental import pallas as pl
from jax.experimental.pallas import tpu as pltpu

F32 = jnp.float32
BF16 = jnp.bfloat16

SSD_HEAD_DIM = 64
SSD_GROUPS = 2
SSD_STATE = 128
CONV_WIDTH = 4
CHUNK = 128
ATTN_QK_DIM = 64
ATTN_V_DIM = 2 * ATTN_QK_DIM
EPS = 1e-5
LAMBDA_INIT_L0 = 0.8 - 0.6 * math.exp(-0.3 * 0)
LOG2_E = math.log2(math.e)

V7X_LANES = 128
V7X_SUBLANES = 8
BF16_SUBLANE_PACK = 16
V7X_VMEM_BYTES = 64 * 1024 * 1024
V7X_VMEM_USABLE_BYTES = 56 * 1024 * 1024

TM_INPROJ = 512
TM_INPROJ_PASS = 256
TK_ATTN = 512
NSUB_ATTN = 2
NQ_ATTN = 4
ATTN_SCRATCH_SETS = 2
ATTN_PRODUCE_LEAD = 2
ATTN_BLOCKS_PER_TRIP = 4
TM_OUT = 1024
TM_OUT_PASS = 256
SSD_CHUNKS_PER_STEP = 4


def _vmem_limit(estimate_bytes):
    return int(min(V7X_VMEM_USABLE_BYTES, max(32 * 1024 * 1024, estimate_bytes)))


def _silu(x):
    hx = 0.5 * x
    return hx + hx * jnp.tanh(hx)


def _softplus(x):
    return jnp.maximum(x, 0.0) + jnp.log(1.0 + jnp.exp(-jnp.abs(x)))


def _split_bf16(a, pieces):
    out = []
    r = a
    for i in range(pieces):
        p = r.astype(BF16)
        out.append(p)
        if i + 1 < pieces:
            r = r - p.astype(F32)
    return out


def _dot(a, b):
    return jnp.dot(a, b, preferred_element_type=F32)


def _dot_nt(a, b):
    return lax.dot_general(a, b, (((1,), (1,)), ((), ())), preferred_element_type=F32)


def _dot_tn(a, b):
    return lax.dot_general(a, b, (((0,), (0,)), ((), ())), preferred_element_type=F32)


def _mod_kernel(c_ref, w_ref, b_ref, o_ref):
    c_hi, c_lo = _split_bf16(c_ref[...], 2)
    w_hi, w_lo = _split_bf16(w_ref[...], 2)
    o_ref[...] = _dot(c_hi, w_hi) + _dot(c_hi, w_lo) + _dot(c_lo, w_hi) + b_ref[...]


def _modulation(c, w_ada, b_ada):
    bsz, d = c.shape
    n = w_ada.shape[1]
    rows = V7X_SUBLANES
    c_pad = jnp.zeros((rows, d), F32).at[:bsz].set(c)
    bn = d
    out = pl.pallas_call(
        _mod_kernel,
        grid=(n // bn,),
        in_specs=[pl.BlockSpec((rows, d), lambda j: (0, 0)),
                  pl.BlockSpec((d, bn), lambda j: (0, j)),
                  pl.BlockSpec((1, bn), lambda j: (0, j))],
        out_specs=pl.BlockSpec((rows, bn), lambda j: (0, j)),
        out_shape=jax.ShapeDtypeStruct((rows, n), F32),
        name="modulation",
    )(c_pad, w_ada, b_ada.reshape(1, n))
    return out[:bsz]


def _inproj_kernel(x_ref, scale_ref, shift_ref, wzx_ref, wk_ref, wza_ref, wqvT_ref, wdt_ref,
                   convw_ref, convb_ref, dtb_ref,
                   zs_ref, xs_ref, bm_ref, cm_ref, qT_ref, k_ref, vT_ref, za_ref, dt_ref, dtT_ref,
                   carry_ref, *, tm, rows_per_pass, d_ssd, d_xbc, d_attn, n_bc):
    @pl.when(pl.program_id(1) == 0)
    def _():
        carry_ref[...] = jnp.zeros_like(carry_ref)

    pad = V7X_SUBLANES
    nh = dt_ref.shape[-1]
    nh_attn = d_attn // ATTN_V_DIM
    cw = convw_ref[...]
    w_hi, w_lo = _split_bf16(wdt_ref[...], 2)
    wrow = lax.broadcasted_iota(jnp.int32, (V7X_LANES, 1), 0)
    w_hilo = jnp.where(wrow < nh, w_hi, w_lo)

    tail = carry_ref[...]
    for r0 in range(0, tm, rows_per_pass):
        rows = slice(r0, r0 + rows_per_pass)
        h32 = x_ref[0, rows, :] * (1.0 + scale_ref[0]) + shift_ref[0]
        h = h32.astype(BF16)

        u = _dot(h, wzx_ref[:, d_ssd:d_ssd + d_xbc])
        u_ext = jnp.concatenate([tail, u], axis=0)
        tail = u[rows_per_pass - pad:rows_per_pass, :]
        acc = convb_ref[...] + cw[CONV_WIDTH - 1:CONV_WIDTH, :] * u
        for tap in range(CONV_WIDTH - 1):
            shift = CONV_WIDTH - 1 - tap
            acc = acc + (cw[tap:tap + 1, :]
                         * pltpu.roll(u_ext, shift, 0)[pad:pad + rows_per_pass, :])
        xbc = _silu(acc)
        xs_ref[0, rows, :] = xbc[:, :d_ssd].astype(BF16)
        bm_ref[0, rows, :] = xbc[:, d_ssd:d_ssd + n_bc].astype(BF16)
        cm_ref[0, rows, :] = xbc[:, d_ssd + n_bc:d_ssd + 2 * n_bc].astype(BF16)

        zs_ref[0, rows, :] = _silu(_dot(h, wzx_ref[:, 0:d_ssd])).astype(BF16)
        k_ref[0, rows, :] = _dot(h, wk_ref[...]).astype(BF16)
        za_ref[0, rows, :] = _silu(_dot(h, wza_ref[...])).astype(BF16)

        qvT = _dot_nt(wqvT_ref[...], h)
        qT = qvT[:d_attn] * (ATTN_QK_DIM ** -0.5 * LOG2_E)
        qT = qT.astype(BF16).reshape(nh_attn, ATTN_V_DIM, rows_per_pass)
        feat = lax.broadcasted_iota(jnp.int32, (1, ATTN_V_DIM, 1), 1)
        zero = jnp.zeros_like(qT)
        qT_ref[0, :, 0, :, rows] = jnp.where(feat < ATTN_QK_DIM, qT, zero)
        qT_ref[0, :, 0, :, tm + r0:tm + r0 + rows_per_pass] = jnp.where(feat >= ATTN_QK_DIM, qT, zero)
        vT_ref[0, :, 0, :, rows] = qvT[d_attn:].astype(BF16).reshape(nh_attn, ATTN_V_DIM,
                                                                      rows_per_pass)

        h_lo = (h32 - h.astype(F32)).astype(BF16)
        r1 = _dot_nt(h, w_hilo)
        r2 = _dot_nt(h_lo, w_hi)
        dt_raw = r1 + pltpu.roll(r1, V7X_LANES - nh, 1) + r2
        dt = _softplus(dt_raw + dtb_ref[...])
        dt_ref[0, rows, :] = dt[:, :nh]
        dtT_ref[0, :, rows] = dt.T[:nh, :]
    carry_ref[...] = tail


def _in_projection(x, scale, shift, w_main, w_qv_t, w_dt, conv_w, conv_b, dt_bias, *, d_ssd, d_xbc,
                   d_attn):
    bsz, seq, d = x.shape
    tm = TM_INPROJ
    nh = w_dt.shape[0]
    nh_attn = d_attn // ATTN_V_DIM
    n_bc = (d_xbc - d_ssd) // 2
    w_zx, w_k, w_za = w_main
    assert w_zx.shape == (d, d_ssd + d_xbc) and w_k.shape == (d, d_attn) and w_za.shape == (d, d_attn)
    n_main = w_zx.shape[1] + w_k.shape[1] + w_za.shape[1]
    grid = (bsz, seq // tm)

    row = lambda b, l: (b, l, 0)
    head_major = lambda b, l: (b, 0, l, 0, 0)
    const2 = lambda b, l: (0, 0)
    per_b = lambda b, l: (b, 0, 0)
    single = pl.Buffered(1)
    in_specs = [
        pl.BlockSpec((1, tm, d), row),
        pl.BlockSpec((1, 1, d), per_b),
        pl.BlockSpec((1, 1, d), per_b),
        pl.BlockSpec(w_zx.shape, const2, pipeline_mode=single),
        pl.BlockSpec(w_k.shape, const2, pipeline_mode=single),
        pl.BlockSpec(w_za.shape, const2, pipeline_mode=single),
        pl.BlockSpec((2 * d_attn, d), const2, pipeline_mode=single),
        pl.BlockSpec((V7X_LANES, d), const2),
        pl.BlockSpec((CONV_WIDTH, d_xbc), const2),
        pl.BlockSpec((1, d_xbc), const2),
        pl.BlockSpec((1, V7X_LANES), const2),
    ]
    out_specs = [
        pl.BlockSpec((1, tm, d_ssd), row),
        pl.BlockSpec((1, tm, d_ssd), row),
        pl.BlockSpec((1, tm, n_bc), row),
        pl.BlockSpec((1, tm, n_bc), row),
        pl.BlockSpec((1, nh_attn, 1, ATTN_V_DIM, 2 * tm), head_major),
        pl.BlockSpec((1, tm, d_attn), row),
        pl.BlockSpec((1, nh_attn, 1, ATTN_V_DIM, tm), head_major),
        pl.BlockSpec((1, tm, d_attn), row),
        pl.BlockSpec((1, tm, nh), row),
        pl.BlockSpec((1, nh, tm), lambda b, l: (b, 0, l)),
    ]
    out_shape = [
        jax.ShapeDtypeStruct((bsz, seq, d_ssd), BF16),
        jax.ShapeDtypeStruct((bsz, seq, d_ssd), BF16),
        jax.ShapeDtypeStruct((bsz, seq, n_bc), BF16),
        jax.ShapeDtypeStruct((bsz, seq, n_bc), BF16),
        jax.ShapeDtypeStruct((bsz, nh_attn, seq // tm, ATTN_V_DIM, 2 * tm), BF16),
        jax.ShapeDtypeStruct((bsz, seq, d_attn), BF16),
        jax.ShapeDtypeStruct((bsz, nh_attn, seq // tm, ATTN_V_DIM, tm), BF16),
        jax.ShapeDtypeStruct((bsz, seq, d_attn), BF16),
        jax.ShapeDtypeStruct((bsz, seq, nh), F32),
        jax.ShapeDtypeStruct((bsz, nh, seq), F32),
    ]
    pad = V7X_SUBLANES
    vmem = (d * (n_main + 2 * d_attn) * 2 + 2 * tm * d * 4
            + 2 * tm * (2 * d_ssd + 2 * n_bc + 4 * d_attn) * 2
            + (tm + 2 * pad) * d_xbc * 4 + 6 * tm * d_xbc * 4)
    kern = functools.partial(_inproj_kernel, tm=tm, rows_per_pass=TM_INPROJ_PASS, d_ssd=d_ssd,
                             d_xbc=d_xbc, d_attn=d_attn, n_bc=n_bc)
    assert 2 * nh <= V7X_LANES
    w_dt_pad = jnp.concatenate([w_dt, w_dt, jnp.zeros((V7X_LANES - 2 * nh, d), F32)], axis=0)
    dt_bias_pad = jnp.concatenate([dt_bias, jnp.zeros((V7X_LANES - nh,), F32)]).reshape(1, -1)
    return pl.pallas_call(
        kern, grid=grid, in_specs=in_specs, out_specs=out_specs, out_shape=out_shape,
        scratch_shapes=[pltpu.VMEM((pad, d_xbc), F32)],
        compiler_params=pltpu.CompilerParams(
            dimension_semantics=("arbitrary", "arbitrary"), vmem_limit_bytes=_vmem_limit(vmem)),
        name="in_projection",
    )(x, scale, shift, w_zx, w_k, w_za, w_qv_t, w_dt_pad, conv_w, conv_b.reshape(1, -1), dt_bias_pad)


def _ssd_kernel(xs_ref, bm_ref, cm_ref, zs_ref, dt_ref, dtT_ref, alog_ref, alogT_ref, dskip_ref,
                normw_ref, y_ref, state_ref, ybuf_ref, *, nheads, d_ssd, cps):
    t = CHUNK
    n = SSD_STATE
    hp = SSD_HEAD_DIM
    gw = d_ssd // SSD_GROUPS
    heads_per_group = nheads // SSD_GROUPS

    @pl.when(pl.program_id(1) == 0)
    def _():
        state_ref[...] = jnp.zeros_like(state_ref)

    row = lax.broadcasted_iota(jnp.int32, (t, t), 0)
    col = lax.broadcasted_iota(jnp.int32, (t, t), 1)
    causal = col <= row
    tri = jnp.where(causal, 1.0, 0.0).astype(BF16)
    tri_t = jnp.where(row <= col, 1.0, 0.0).astype(BF16)
    eh = lax.broadcasted_iota(jnp.int32, (nheads, d_ssd), 0)
    ej = lax.broadcasted_iota(jnp.int32, (nheads, d_ssd), 1)
    expand_m = jnp.where(ej // hp == eh, 1.0, 0.0).astype(BF16)
    lane = lax.broadcasted_iota(jnp.int32, (1, 2 * hp), 1)
    neg_inf = jnp.float32(-jnp.inf)
    a = -jnp.exp(alog_ref[...])
    a_t = -jnp.exp(alogT_ref[...])

    def expand(v, pieces):
        return sum(_dot(p, expand_m) for p in _split_bf16(v, pieces))

    for ci in range(cps):
        rows = slice(ci * t, (ci + 1) * t)
        dt = dt_ref[0, rows, :]
        dt_t = dtT_ref[0, :, rows]
        acum = sum(_dot(tri, p) for p in _split_bf16(dt * a, 3))
        acum_t = sum(_dot(p, tri_t) for p in _split_bf16(dt_t * a_t, 3))
        last = acum[t - 1:t, :]
        dt_x = expand(dt, 1)
        w_x = expand(jnp.exp(last - acum), 1)
        e_x = expand(jnp.exp(acum), 2)

        xs = xs_ref[0, rows, :].astype(F32)
        xdt = xs * dt_x
        xdt_b = xdt.astype(BF16)
        xw_b = (xdt * w_x).astype(BF16)

        for g in range(SSD_GROUPS):
            b_g = bm_ref[0, rows, g * n:(g + 1) * n]
            c_g = cm_ref[0, rows, g * n:(g + 1) * n]
            gsl = slice(g * gw, (g + 1) * gw)
            cb = _dot_nt(c_g, b_g)
            prev = state_ref[g]
            y_off = _dot(c_g, prev.astype(BF16)) * e_x[:, gsl]
            state_ref[g] = prev * e_x[t - 1:t, gsl] + _dot_tn(b_g, xw_b[:, gsl])
            for pair in range(heads_per_group // 2):
                lo = g * gw + pair * 2 * hp
                x_pair = xdt_b[:, lo:lo + 2 * hp]
                decays = []
                for half in range(2):
                    hd = g * heads_per_group + pair * 2 + half
                    diff = acum[:, hd:hd + 1] - acum_t[hd:hd + 1, :]
                    decays.append((cb * jnp.exp(jnp.where(causal, diff, neg_inf))).astype(BF16))
                zero = jnp.zeros_like(x_pair)
                x_diag = jnp.concatenate([jnp.where(lane < hp, x_pair, zero),
                                          jnp.where(lane >= hp, x_pair, zero)], axis=0)
                y_pair = (y_off[:, pair * 2 * hp:(pair + 1) * 2 * hp]
                          + _dot(jnp.concatenate(decays, axis=1), x_diag))
                ybuf_ref[rows, lo:lo + 2 * hp] = y_pair

    xs = xs_ref[0].astype(F32)
    y = ybuf_ref[...] + dskip_ref[...] * xs
    y = y * zs_ref[0].astype(F32)
    for g in range(SSD_GROUPS):
        gsl = slice(g * gw, (g + 1) * gw)
        y_g = y[:, gsl]
        ms = jnp.mean(y_g * y_g, axis=-1, keepdims=True)
        y_ref[0, :, gsl] = (y_g * lax.rsqrt(ms + EPS) * normw_ref[:, gsl]).astype(BF16)


def _ssd(xs, bm, cm, zs, dt, dt_t, a_log, d_skip, ssd_norm_w):
    bsz, seq, d_ssd = xs.shape
    nheads = dt.shape[-1]
    n_bc = bm.shape[-1]
    cps = SSD_CHUNKS_PER_STEP
    t = cps * CHUNK
    assert seq % t == 0
    grid = (bsz, seq // t)
    row = lambda b, c: (b, c, 0)
    const2 = lambda b, c: (0, 0)
    in_specs = [
        pl.BlockSpec((1, t, d_ssd), row),
        pl.BlockSpec((1, t, n_bc), row),
        pl.BlockSpec((1, t, n_bc), row),
        pl.BlockSpec((1, t, d_ssd), row),
        pl.BlockSpec((1, t, nheads), row),
        pl.BlockSpec((1, nheads, t), lambda b, c: (b, 0, c)),
        pl.BlockSpec((1, nheads), const2),
        pl.BlockSpec((nheads, 1), const2),
        pl.BlockSpec((1, d_ssd), const2),
        pl.BlockSpec((1, d_ssd), const2),
    ]
    kern = functools.partial(_ssd_kernel, nheads=nheads, d_ssd=d_ssd, cps=cps)
    return pl.pallas_call(
        kern, grid=grid, in_specs=in_specs,
        out_specs=pl.BlockSpec((1, t, d_ssd), row),
        out_shape=jax.ShapeDtypeStruct((bsz, seq, d_ssd), BF16),
        scratch_shapes=[pltpu.VMEM((SSD_GROUPS, SSD_STATE, d_ssd // SSD_GROUPS), F32),
                        pltpu.VMEM((t, d_ssd), F32)],
        compiler_params=pltpu.CompilerParams(dimension_semantics=("arbitrary", "arbitrary")),
        name="ssd_scan",
    )(xs, bm, cm, zs, dt, dt_t, a_log.reshape(1, -1), a_log.reshape(-1, 1),
      jnp.repeat(d_skip, SSD_HEAD_DIM).reshape(1, -1), ssd_norm_w.reshape(1, -1))


def _attn_kernel(qT_ref, k_ref, vT_ref, za_ref, lq1_ref, lk1_ref, lq2_ref, lk2_ref, normw_ref,
                 o_ref, s_ref, cmax_ref, m_ref, acc_ref, *, tk, nsub, nq):
    step = pl.program_id(2)
    dk = ATTN_QK_DIM
    dv = 2 * dk
    sw = 2 * tk
    nchunk = 2 * nsub
    tq = nsub * tk
    lead = ATTN_PRODUCE_LEAD
    per_trip = ATTN_BLOCKS_PER_TRIP
    assert lead <= nchunk and per_trip % 2 == 0 and nsub % 2 == 0 and (nsub * nq) % per_trip == 0
    ones_rows = jnp.ones((BF16_SUBLANE_PACK, tk), BF16)
    lam = (jnp.exp(jnp.sum(lq1_ref[...] * lk1_ref[...], axis=-1, keepdims=True))
           - jnp.exp(jnp.sum(lq2_ref[...] * lk2_ref[...], axis=-1, keepdims=True))
           + LAMBDA_INIT_L0)

    def col(c):
        return slice(c * tk, (c + 1) * tk)

    def query_block(g):
        sc = g % ATTN_SCRATCH_SETS

        def produce(slot, kv, c):
            start = pl.multiple_of(kv * tk, tk)
            q_cols = qT_ref[0, 0, g * nsub + c // 2, :, (c % 2) * tk:(c % 2 + 1) * tk]
            s = _dot(k_ref[0, pl.ds(start, tk), :], q_cols)
            s_ref[sc, slot, :, col(c)] = s
            cmax_ref[sc, slot, :, col(c)] = jnp.max(s, axis=0, keepdims=True)

        def consume(slot, kv, c, diagonal=False):
            s = s_ref[sc, slot, :, col(c)]
            if diagonal:
                r = lax.broadcasted_iota(jnp.int32, s.shape, 0)
                qpos = lax.broadcasted_iota(jnp.int32, s.shape, 1)
                s = jnp.where(r <= qpos, s, -jnp.inf)
                cmax = jnp.max(s, axis=0, keepdims=True)
            else:
                cmax = cmax_ref[sc, slot, :, col(c)]
            m_prev = m_ref[sc, :, col(c)]
            m_new = jnp.maximum(m_prev, cmax)
            alpha = jnp.exp2(m_prev - m_new)
            p = jnp.exp2(s - m_new).astype(BF16)
            v_ext = jnp.concatenate([vT_ref[0, 0, kv], ones_rows], axis=0)
            acc_ref[sc, :, col(c)] = alpha * acc_ref[sc, :, col(c)] + _dot(v_ext, p)
            m_ref[sc, :, col(c)] = m_new

        def finalize(sub):
            c1 = sub * sw
            c2 = c1 + tk
            o1 = acc_ref[sc, 0:dv, c1:c1 + tk] / acc_ref[sc, dv:dv + 1, c1:c1 + tk]
            o2 = acc_ref[sc, 0:dv, c2:c2 + tk] / acc_ref[sc, dv:dv + 1, c2:c2 + tk]
            o = (o1 - lam * o2).T
            ms = jnp.mean(o * o, axis=-1, keepdims=True)
            o = o * lax.rsqrt(ms + EPS) * normw_ref[...] * (1.0 - LAMBDA_INIT_L0)
            rows = slice(g * tq + sub * tk, g * tq + (sub + 1) * tk)
            o_ref[0, rows, :] = (o * za_ref[0, rows, :].astype(F32)).astype(BF16)

        m_ref[sc] = jnp.full(m_ref.shape[1:], -jnp.inf, F32)
        acc_ref[sc] = jnp.zeros(acc_ref.shape[1:], F32)

        for c in range(lead):
            produce(0, 0, c)

        def off_diagonal(kv, nblocks):
            for idx in range(nblocks * nchunk):
                ahead = idx + lead
                produce((ahead // nchunk) % 2, kv + ahead // nchunk, ahead % nchunk)
                consume((idx // nchunk) % 2, kv + idx // nchunk, idx % nchunk)

        trips = (nsub * nq // per_trip) * step + (nsub * g) // per_trip
        left = (nsub * g) % per_trip

        def trip(i, carry):
            off_diagonal(i * per_trip, per_trip)
            return carry

        lax.fori_loop(0, trips, trip, 0)
        if left:
            off_diagonal(trips * per_trip, left)

        kv0 = trips * per_trip + left
        units = [(d, c) for d in range(nsub) for c in range(nchunk) if c // 2 >= d]
        assert units[:lead] == [(0, c) for c in range(lead)]
        for i, (d, c) in enumerate(units):
            if i + lead < len(units):
                nd, nc = units[i + lead]
                produce(nd % 2, kv0 + nd, nc)
            consume(d % 2, kv0 + d, c, diagonal=(c // 2 == d))
            if c % 2 == 1 and c // 2 == d:
                finalize(d)

    for g in range(nq):
        query_block(g)


def _attention(q_t, k, v_t, za, lq1, lk1, lq2, lk2, attn_norm_w):
    bsz, seq, d_attn = k.shape
    hw = ATTN_V_DIM
    nh = d_attn // hw
    tk = TK_ATTN
    nsub = NSUB_ATTN
    nq = NQ_ATTN
    tq = nsub * tk
    rows = nq * tq
    nblk = seq // tk
    assert nsub % 2 == 0 and seq % rows == 0
    assert q_t.shape == (bsz, nh, nblk, hw, 2 * tk) and v_t.shape == (bsz, nh, nblk, hw, tk)
    grid = (bsz, nh, seq // rows)
    qrow = lambda b, h, i: (b, i, h)
    const2 = lambda b, h, i: (0, 0)
    vec = pl.BlockSpec((1, ATTN_QK_DIM), const2)
    in_specs = [
        pl.BlockSpec((1, 1, nq * nsub, hw, 2 * tk), lambda b, h, i: (b, h, i, 0, 0)),
        pl.BlockSpec((1, seq, hw), lambda b, h, i: (b, 0, h)),
        pl.BlockSpec((1, 1, nblk, hw, tk), lambda b, h, i: (b, h, 0, 0, 0)),
        pl.BlockSpec((1, rows, hw), qrow),
        vec, vec, vec, vec,
        pl.BlockSpec((1, hw), const2),
    ]
    width = 2 * tq
    nscr = ATTN_SCRATCH_SETS
    assert nscr >= min(nq, 2)
    vmem = (2 * 2 * seq * hw * 2 + nscr * 2 * tk * width * 4
            + nscr * (2 * hw + BF16_SUBLANE_PACK) * width * 4 + 4 * tk * width * 4)
    return pl.pallas_call(
        functools.partial(_attn_kernel, tk=tk, nsub=nsub, nq=nq), grid=grid, in_specs=in_specs,
        out_specs=pl.BlockSpec((1, rows, hw), qrow),
        out_shape=jax.ShapeDtypeStruct((bsz, seq, d_attn), BF16),
        scratch_shapes=[pltpu.VMEM((nscr, 2, tk, width), F32),
                        pltpu.VMEM((nscr, 2, 1, width), F32), pltpu.VMEM((nscr, 1, width), F32),
                        pltpu.VMEM((nscr, hw + BF16_SUBLANE_PACK, width), F32)],
        compiler_params=pltpu.CompilerParams(
            dimension_semantics=("arbitrary", "arbitrary", "arbitrary"),
            vmem_limit_bytes=_vmem_limit(vmem)),
        name="diff_attention",
    )(q_t, k, v_t, za, lq1.reshape(1, -1), lk1.reshape(1, -1), lq2.reshape(1, -1),
      lk2.reshape(1, -1), attn_norm_w.reshape(1, -1))


def _outproj_kernel(ys_ref, ya_ref, w_ref, x_ref, gate_ref, g_ref, b_ref, o_ref, *, d_ssd, alpha,
                    tm, rows_per_pass):
    for r0 in range(0, tm, rows_per_pass):
        rows = slice(r0, r0 + rows_per_pass)
        mixed = (_dot(ys_ref[0, rows, :], w_ref[0:d_ssd, :])
                 + _dot(ya_ref[0, rows, :], w_ref[d_ssd:, :]))
        xf = alpha * x_ref[0, rows, :] + gate_ref[0] * mixed
        mu = jnp.mean(xf, axis=-1, keepdims=True)
        xc = xf - mu
        var = jnp.mean(xc * xc, axis=-1, keepdims=True)
        o_ref[0, rows, :] = xc * lax.rsqrt(var + EPS) * g_ref[...] + b_ref[...]


def _out_projection(y_ssd, y_attn, w_out, x, gate, ln_g, ln_b, alpha):
    bsz, seq, d = x.shape
    d_ssd = y_ssd.shape[-1]
    d_attn = y_attn.shape[-1]
    tm = TM_OUT
    row = lambda b, l: (b, l, 0)
    const2 = lambda b, l: (0, 0)
    in_specs = [
        pl.BlockSpec((1, tm, d_ssd), row),
        pl.BlockSpec((1, tm, d_attn), row),
        pl.BlockSpec((d_ssd + d_attn, d), const2, pipeline_mode=pl.Buffered(1)),
        pl.BlockSpec((1, tm, d), row),
        pl.BlockSpec((1, 1, d), lambda b, l: (b, 0, 0)),
        pl.BlockSpec((1, d), const2),
        pl.BlockSpec((1, d), const2),
    ]
    vmem = (d_ssd + d_attn) * d * 2 + 2 * tm * (d_ssd + d_attn) * 2 + 4 * tm * d * 4 + 4 * tm * d * 4
    return pl.pallas_call(
        functools.partial(_outproj_kernel, d_ssd=d_ssd, alpha=alpha, tm=tm,
                          rows_per_pass=TM_OUT_PASS),
        grid=(bsz, seq // tm), in_specs=in_specs,
        out_specs=pl.BlockSpec((1, tm, d), row),
        out_shape=jax.ShapeDtypeStruct((bsz, seq, d), F32),
        compiler_params=pltpu.CompilerParams(
            dimension_semantics=("arbitrary", "arbitrary"), vmem_limit_bytes=_vmem_limit(vmem)),
        name="out_projection",
    )(y_ssd, y_attn, w_out, x, gate, ln_g.reshape(1, -1), ln_b.reshape(1, -1))


def _layer(x, c, w_ada, b_ada, w_in, conv_w, conv_b, dt_bias, a_log, d_skip, ssd_norm_w,
           lambda_q1, lambda_k1, lambda_q2, lambda_k2, attn_norm_w, w_out, ln_g, ln_b, depth):
    bsz, seq, d = x.shape
    nheads = a_log.shape[0]
    d_ssd = nheads * SSD_HEAD_DIM
    d_xbc = conv_w.shape[1]
    d_attn = w_out.shape[0] - d_ssd
    assert d_xbc == d_ssd + 2 * SSD_GROUPS * SSD_STATE
    assert w_in.shape[1] == d_ssd + d_xbc + nheads + 4 * d_attn
    assert seq % TM_INPROJ == 0 and seq % TM_OUT == 0 and seq % CHUNK == 0
    assert TM_INPROJ == TK_ATTN, "the in-projection writes q^T / v^T in attention-sized time blocks"

    mod = _modulation(c, w_ada, b_ada)
    shift = mod[:, None, 0:d]
    scale = mod[:, None, d:2 * d]
    gate = mod[:, None, 2 * d:3 * d]

    off_dt = d_ssd + d_xbc
    off_q = off_dt + nheads
    off_k, off_v, off_za = off_q + d_attn, off_q + 2 * d_attn, off_q + 3 * d_attn
    w_t = w_in.T
    w_main = tuple(w_t[lo:hi].astype(BF16).T
                   for lo, hi in ((0, off_dt), (off_k, off_v), (off_za, off_za + d_attn)))
    w_qv_t = jnp.concatenate([w_t[off_q:off_k], w_t[off_v:off_za]], axis=0).astype(BF16)
    w_dt = w_t[off_dt:off_q]

    zs, xs, bm, cm, q_t, k, v_t, za, dt, dt_t = _in_projection(
        x, scale, shift, w_main, w_qv_t, w_dt, conv_w, conv_b, dt_bias,
        d_ssd=d_ssd, d_xbc=d_xbc, d_attn=d_attn)
    y_ssd = _ssd(xs, bm, cm, zs, dt, dt_t, a_log, d_skip, ssd_norm_w)
    y_attn = _attention(q_t, k, v_t, za, lambda_q1, lambda_k1, lambda_q2, lambda_k2, attn_norm_w)
    alpha = (2.0 * depth) ** 0.25
    return _out_projection(y_ssd, y_attn, w_out.astype(BF16), x, gate, ln_g, ln_b, alpha)


def kernel(x, c, w_ada, b_ada, w_in, conv_w, conv_b, dt_bias, a_log, d_skip, ssd_norm_w, lambda_q1,
           lambda_k1, lambda_q2, lambda_k2, attn_norm_w, w_out, ln_g, ln_b):
    depth = w_in.shape[0]
    assert depth == 1, "lambda_init and the layer loop are specialised to a single layer"
    return _layer(x, c, w_ada[0], b_ada[0], w_in[0], conv_w[0], conv_b[0], dt_bias[0], a_log[0],
                  d_skip[0], ssd_norm_w[0], lambda_q1[0], lambda_k1[0], lambda_q2[0], lambda_k2[0],
                  attn_norm_w[0], w_out[0], ln_g[0], ln_b[0], depth)
```

```python
import functools
import math

import jax
import jax.numpy as jnp
from jax import lax
from jax.experimental import pallas as pl
from jax.experimental.pallas import tpu as pltpu

F32 = jnp.float32
BF16 = jnp.bfloat16

SSD_HEAD_DIM = 64
SSD_GROUPS = 2
SSD_STATE = 128
CONV_WIDTH = 4
CHUNK = 128
ATTN_QK_DIM = 64
ATTN_V_DIM = 2 * ATTN_QK_DIM
EPS = 1e-5
LAMBDA_INIT_L0 = 0.8 - 0.6 * math.exp(-0.3 * 0)
LOG2_E = math.log2(math.e)

V7X_LANES = 128
V7X_SUBLANES = 8
BF16_SUBLANE_PACK = 16
V7X_VMEM_BYTES = 64 * 1024 * 1024
V7X_VMEM_USABLE_BYTES = 56 * 1024 * 1024

TM_INPROJ = 512
TM_INPROJ_PASS = 256
TK_ATTN = 512
NSUB_ATTN = 2
NQ_ATTN = 4
ATTN_SCRATCH_SETS = 2
ATTN_PRODUCE_LEAD = 2
ATTN_BLOCKS_PER_TRIP = 4
TM_OUT = 1024
TM_OUT_PASS = 256
SSD_CHUNKS_PER_STEP = 4


def _vmem_limit(estimate_bytes):
    return int(min(V7X_VMEM_USABLE_BYTES, max(32 * 1024 * 1024, estimate_bytes)))


def _silu(x):
    hx = 0.5 * x
    return hx + hx * jnp.tanh(hx)


def _softplus(x):
    return jnp.maximum(x, 0.0) + jnp.log(1.0 + jnp.exp(-jnp.abs(x)))


def _split_bf16(a, pieces):
    out = []
    r = a
    for i in range(pieces):
        p = r.astype(BF16)
        out.append(p)
        if i + 1 < pieces:
            r = r - p.astype(F32)
    return out


def _dot(a, b):
    return jnp.dot(a, b, preferred_element_type=F32)


def _dot_nt(a, b):
    return lax.dot_general(a, b, (((1,), (1,)), ((), ())), preferred_element_type=F32)


def _dot_tn(a, b):
    return lax.dot_general(a, b, (((0,), (0,)), ((), ())), preferred_element_type=F32)


def _mod_kernel(c_ref, w_ref, b_ref, o_ref):
    c_hi, c_lo = _split_bf16(c_ref[...], 2)
    w_hi, w_lo = _split_bf16(w_ref[...], 2)
    o_ref[...] = _dot(c_hi, w_hi) + _dot(c_hi, w_lo) + _dot(c_lo, w_hi) + b_ref[...]


def _modulation(c, w_ada, b_ada):
    bsz, d = c.shape
    n = w_ada.shape[1]
    rows = V7X_SUBLANES
    c_pad = jnp.zeros((rows, d), F32).at[:bsz].set(c)
    bn = d
    out = pl.pallas_call(
        _mod_kernel,
        grid=(n // bn,),
        in_specs=[pl.BlockSpec((rows, d), lambda j: (0, 0)),
                  pl.BlockSpec((d, bn), lambda j: (0, j)),
                  pl.BlockSpec((1, bn), lambda j: (0, j))],
        out_specs=pl.BlockSpec((rows, bn), lambda j: (0, j)),
        out_shape=jax.ShapeDtypeStruct((rows, n), F32),
        name="modulation",
    )(c_pad, w_ada, b_ada.reshape(1, n))
    return out[:bsz]


def _inproj_kernel(x_ref, scale_ref, shift_ref, wzx_ref, wk_ref, wza_ref, wqvT_ref, wdt_ref,
                   convw_ref, convb_ref, dtb_ref,
                   zs_ref, xs_ref, bm_ref, cm_ref, qT_ref, k_ref, vT_ref, za_ref, dt_ref, dtT_ref,
                   carry_ref, *, tm, rows_per_pass, d_ssd, d_xbc, d_attn, n_bc):
    @pl.when(pl.program_id(1) == 0)
    def _():
        carry_ref[...] = jnp.zeros_like(carry_ref)

    pad = V7X_SUBLANES
    nh = dt_ref.shape[-1]
    nh_attn = d_attn // ATTN_V_DIM
    cw = convw_ref[...]
    w_hi, w_lo = _split_bf16(wdt_ref[...], 2)
    wrow = lax.broadcasted_iota(jnp.int32, (V7X_LANES, 1), 0)
    w_hilo = jnp.where(wrow < nh, w_hi, w_lo)

    tail = carry_ref[...]
    for r0 in range(0, tm, rows_per_pass):
        rows = slice(r0, r0 + rows_per_pass)
        h32 = x_ref[0, rows, :] * (1.0 + scale_ref[0]) + shift_ref[0]
        h = h32.astype(BF16)

        u = _dot(h, wzx_ref[:, d_ssd:d_ssd + d_xbc])
        u_ext = jnp.concatenate([tail, u], axis=0)
        tail = u[rows_per_pass - pad:rows_per_pass, :]
        acc = convb_ref[...] + cw[CONV_WIDTH - 1:CONV_WIDTH, :] * u
        for tap in range(CONV_WIDTH - 1):
            shift = CONV_WIDTH - 1 - tap
            acc = acc + (cw[tap:tap + 1, :]
                         * pltpu.roll(u_ext, shift, 0)[pad:pad + rows_per_pass, :])
        xbc = _silu(acc)
        xs_ref[0, rows, :] = xbc[:, :d_ssd].astype(BF16)
        bm_ref[0, rows, :] = xbc[:, d_ssd:d_ssd + n_bc].astype(BF16)
        cm_ref[0, rows, :] = xbc[:, d_ssd + n_bc:d_ssd + 2 * n_bc].astype(BF16)

        zs_ref[0, rows, :] = _silu(_dot(h, wzx_ref[:, 0:d_ssd])).astype(BF16)
        k_ref[0, rows, :] = _dot(h, wk_ref[...]).astype(BF16)
        za_ref[0, rows, :] = _silu(_dot(h, wza_ref[...])).astype(BF16)

        qvT = _dot_nt(wqvT_ref[...], h)
        qT = qvT[:d_attn] * (ATTN_QK_DIM ** -0.5 * LOG2_E)
        qT = qT.astype(BF16).reshape(nh_attn, ATTN_V_DIM, rows_per_pass)
        feat = lax.broadcasted_iota(jnp.int32, (1, ATTN_V_DIM, 1), 1)
        zero = jnp.zeros_like(qT)
        qT_ref[0, :, 0, :, rows] = jnp.where(feat < ATTN_QK_DIM, qT, zero)
        qT_ref[0, :, 0, :, tm + r0:tm + r0 + rows_per_pass] = jnp.where(feat >= ATTN_QK_DIM, qT, zero)
        vT_ref[0, :, 0, :, rows] = qvT[d_attn:].astype(BF16).reshape(nh_attn, ATTN_V_DIM,
                                                                      rows_per_pass)

        h_lo = (h32 - h.astype(F32)).astype(BF16)
        r1 = _dot_nt(h, w_hilo)
        r2 = _dot_nt(h_lo, w_hi)
        dt_raw = r1 + pltpu.roll(r1, V7X_LANES - nh, 1) + r2
        dt = _softplus(dt_raw + dtb_ref[...])
        dt_ref[0, rows, :] = dt[:, :nh]
        dtT_ref[0, :, rows] = dt.T[:nh, :]
    carry_ref[...] = tail


def _in_projection(x, scale, shift, w_main, w_qv_t, w_dt, conv_w, conv_b, dt_bias, *, d_ssd, d_xbc,
                   d_attn):
    bsz, seq, d = x.shape
    tm = TM_INPROJ
    nh = w_dt.shape[0]
    nh_attn = d_attn // ATTN_V_DIM
    n_bc = (d_xbc - d_ssd) // 2
    w_zx, w_k, w_za = w_main
    assert w_zx.shape == (d, d_ssd + d_xbc) and w_k.shape == (d, d_attn) and w_za.shape == (d, d_attn)
    n_main = w_zx.shape[1] + w_k.shape[1] + w_za.shape[1]
    grid = (bsz, seq // tm)

    row = lambda b, l: (b, l, 0)
    head_major = lambda b, l: (b, 0, l, 0, 0)
    const2 = lambda b, l: (0, 0)
    per_b = lambda b, l: (b, 0, 0)
    single = pl.Buffered(1)
    in_specs = [
        pl.BlockSpec((1, tm, d), row),
        pl.BlockSpec((1, 1, d), per_b),
        pl.BlockSpec((1, 1, d), per_b),
        pl.BlockSpec(w_zx.shape, const2, pipeline_mode=single),
        pl.BlockSpec(w_k.shape, const2, pipeline_mode=single),
        pl.BlockSpec(w_za.shape, const2, pipeline_mode=single),
        pl.BlockSpec((2 * d_attn, d), const2, pipeline_mode=single),
        pl.BlockSpec((V7X_LANES, d), const2),
        pl.BlockSpec((CONV_WIDTH, d_xbc), const2),
        pl.BlockSpec((1, d_xbc), const2),
        pl.BlockSpec((1, V7X_LANES), const2),
    ]
    out_specs = [
        pl.BlockSpec((1, tm, d_ssd), row),
        pl.BlockSpec((1, tm, d_ssd), row),
        pl.BlockSpec((1, tm, n_bc), row),
        pl.BlockSpec((1, tm, n_bc), row),
        pl.BlockSpec((1, nh_attn, 1, ATTN_V_DIM, 2 * tm), head_major),
        pl.BlockSpec((1, tm, d_attn), row),
        pl.BlockSpec((1, nh_attn, 1, ATTN_V_DIM, tm), head_major),
        pl.BlockSpec((1, tm, d_attn), row),
        pl.BlockSpec((1, tm, nh), row),
        pl.BlockSpec((1, nh, tm), lambda b, l: (b, 0, l)),
    ]
    out_shape = [
        jax.ShapeDtypeStruct((bsz, seq, d_ssd), BF16),
        jax.ShapeDtypeStruct((bsz, seq, d_ssd), BF16),
        jax.ShapeDtypeStruct((bsz, seq, n_bc), BF16),
        jax.ShapeDtypeStruct((bsz, seq, n_bc), BF16),
        jax.ShapeDtypeStruct((bsz, nh_attn, seq // tm, ATTN_V_DIM, 2 * tm), BF16),
        jax.ShapeDtypeStruct((bsz, seq, d_attn), BF16),
        jax.ShapeDtypeStruct((bsz, nh_attn, seq // tm, ATTN_V_DIM, tm), BF16),
        jax.ShapeDtypeStruct((bsz, seq, d_attn), BF16),
        jax.ShapeDtypeStruct((bsz, seq, nh), F32),
        jax.ShapeDtypeStruct((bsz, nh, seq), F32),
    ]
    pad = V7X_SUBLANES
    vmem = (d * (n_main + 2 * d_attn) * 2 + 2 * tm * d * 4
            + 2 * tm * (2 * d_ssd + 2 * n_bc + 4 * d_attn) * 2
            + (tm + 2 * pad) * d_xbc * 4 + 6 * tm * d_xbc * 4)
    kern = functools.partial(_inproj_kernel, tm=tm, rows_per_pass=TM_INPROJ_PASS, d_ssd=d_ssd,
                             d_xbc=d_xbc, d_attn=d_attn, n_bc=n_bc)
    assert 2 * nh <= V7X_LANES
    w_dt_pad = jnp.concatenate([w_dt, w_dt, jnp.zeros((V7X_LANES - 2 * nh, d), F32)], axis=0)
    dt_bias_pad = jnp.concatenate([dt_bias, jnp.zeros((V7X_LANES - nh,), F32)]).reshape(1, -1)
    return pl.pallas_call(
        kern, grid=grid, in_specs=in_specs, out_specs=out_specs, out_shape=out_shape,
        scratch_shapes=[pltpu.VMEM((pad, d_xbc), F32)],
        compiler_params=pltpu.CompilerParams(
            dimension_semantics=("arbitrary", "arbitrary"), vmem_limit_bytes=_vmem_limit(vmem)),
        name="in_projection",
    )(x, scale, shift, w_zx, w_k, w_za, w_qv_t, w_dt_pad, conv_w, conv_b.reshape(1, -1), dt_bias_pad)


def _ssd_kernel(xs_ref, bm_ref, cm_ref, zs_ref, dt_ref, dtT_ref, alog_ref, alogT_ref, dskip_ref,
                normw_ref, y_ref, state_ref, ybuf_ref, *, nheads, d_ssd, cps):
    t = CHUNK
    n = SSD_STATE
    hp = SSD_HEAD_DIM
    gw = d_ssd // SSD_GROUPS
    heads_per_group = nheads // SSD_GROUPS

    @pl.when(pl.program_id(1) == 0)
    def _():
        state_ref[...] = jnp.zeros_like(state_ref)

    row = lax.broadcasted_iota(jnp.int32, (t, t), 0)
    col = lax.broadcasted_iota(jnp.int32, (t, t), 1)
    causal = col <= row
    tri = jnp.where(causal, 1.0, 0.0).astype(BF16)
    tri_t = jnp.where(row <= col, 1.0, 0.0).astype(BF16)
    eh = lax.broadcasted_iota(jnp.int32, (nheads, d_ssd), 0)
    ej = lax.broadcasted_iota(jnp.int32, (nheads, d_ssd), 1)
    expand_m = jnp.where(ej // hp == eh, 1.0, 0.0).astype(BF16)
    lane = lax.broadcasted_iota(jnp.int32, (1, 2 * hp), 1)
    neg_inf = jnp.float32(-jnp.inf)
    a = -jnp.exp(alog_ref[...])
    a_t = -jnp.exp(alogT_ref[...])

    def expand(v, pieces):
        return sum(_dot(p, expand_m) for p in _split_bf16(v, pieces))

    for ci in range(cps):
        rows = slice(ci * t, (ci + 1) * t)
        dt = dt_ref[0, rows, :]
        dt_t = dtT_ref[0, :, rows]
        acum = sum(_dot(tri, p) for p in _split_bf16(dt * a, 3))
        acum_t = sum(_dot(p, tri_t) for p in _split_bf16(dt_t * a_t, 3))
        last = acum[t - 1:t, :]
        dt_x = expand(dt, 1)
        w_x = expand(jnp.exp(last - acum), 1)
        e_x = expand(jnp.exp(acum), 2)

        xs = xs_ref[0, rows, :].astype(F32)
        xdt = xs * dt_x
        xdt_b = xdt.astype(BF16)
        xw_b = (xdt * w_x).astype(BF16)

        for g in range(SSD_GROUPS):
            b_g = bm_ref[0, rows, g * n:(g + 1) * n]
            c_g = cm_ref[0, rows, g * n:(g + 1) * n]
            gsl = slice(g * gw, (g + 1) * gw)
            cb = _dot_nt(c_g, b_g)
            prev = state_ref[g]
            y_off = _dot(c_g, prev.astype(BF16)) * e_x[:, gsl]
            state_ref[g] = prev * e_x[t - 1:t, gsl] + _dot_tn(b_g, xw_b[:, gsl])
            for pair in range(heads_per_group // 2):
                lo = g * gw + pair * 2 * hp
                x_pair = xdt_b[:, lo:lo + 2 * hp]
                decays = []
                for half in range(2):
                    hd = g * heads_per_group + pair * 2 + half
                    diff = acum[:, hd:hd + 1] - acum_t[hd:hd + 1, :]
                    decays.append((cb * jnp.exp(jnp.where(causal, diff, neg_inf))).astype(BF16))
                zero = jnp.zeros_like(x_pair)
                x_diag = jnp.concatenate([jnp.where(lane < hp, x_pair, zero),
                                          jnp.where(lane >= hp, x_pair, zero)], axis=0)
                y_pair = (y_off[:, pair * 2 * hp:(pair + 1) * 2 * hp]
                          + _dot(jnp.concatenate(decays, axis=1), x_diag))
                ybuf_ref[rows, lo:lo + 2 * hp] = y_pair

    xs = xs_ref[0].astype(F32)
    y = ybuf_ref[...] + dskip_ref[...] * xs
    y = y * zs_ref[0].astype(F32)
    for g in range(SSD_GROUPS):
        gsl = slice(g * gw, (g + 1) * gw)
        y_g = y[:, gsl]
        ms = jnp.mean(y_g * y_g, axis=-1, keepdims=True)
        y_ref[0, :, gsl] = (y_g * lax.rsqrt(ms + EPS) * normw_ref[:, gsl]).astype(BF16)


def _ssd(xs, bm, cm, zs, dt, dt_t, a_log, d_skip, ssd_norm_w):
    bsz, seq, d_ssd = xs.shape
    nheads = dt.shape[-1]
    n_bc = bm.shape[-1]
    cps = SSD_CHUNKS_PER_STEP
    t = cps * CHUNK
    assert seq % t == 0
    grid = (bsz, seq // t)
    row = lambda b, c: (b, c, 0)
    const2 = lambda b, c: (0, 0)
    in_specs = [
        pl.BlockSpec((1, t, d_ssd), row),
        pl.BlockSpec((1, t, n_bc), row),
        pl.BlockSpec((1, t, n_bc), row),
        pl.BlockSpec((1, t, d_ssd), row),
        pl.BlockSpec((1, t, nheads), row),
        pl.BlockSpec((1, nheads, t), lambda b, c: (b, 0, c)),
        pl.BlockSpec((1, nheads), const2),
        pl.BlockSpec((nheads, 1), const2),
        pl.BlockSpec((1, d_ssd), const2),
        pl.BlockSpec((1, d_ssd), const2),
    ]
    kern = functools.partial(_ssd_kernel, nheads=nheads, d_ssd=d_ssd, cps=cps)
    return pl.pallas_call(
        kern, grid=grid, in_specs=in_specs,
        out_specs=pl.BlockSpec((1, t, d_ssd), row),
        out_shape=jax.ShapeDtypeStruct((bsz, seq, d_ssd), BF16),
        scratch_shapes=[pltpu.VMEM((SSD_GROUPS, SSD_STATE, d_ssd // SSD_GROUPS), F32),
                        pltpu.VMEM((t, d_ssd), F32)],
        compiler_params=pltpu.CompilerParams(dimension_semantics=("arbitrary", "arbitrary")),
        name="ssd_scan",
    )(xs, bm, cm, zs, dt, dt_t, a_log.reshape(1, -1), a_log.reshape(-1, 1),
      jnp.repeat(d_skip, SSD_HEAD_DIM).reshape(1, -1), ssd_norm_w.reshape(1, -1))


def _attn_kernel(qT_ref, k_ref, vT_ref, za_ref, lq1_ref, lk1_ref, lq2_ref, lk2_ref, normw_ref,
                 o_ref, s_ref, cmax_ref, m_ref, acc_ref, *, tk, nsub, nq):
    step = pl.program_id(2)
    dk = ATTN_QK_DIM
    dv = 2 * dk
    sw = 2 * tk
    nchunk = 2 * nsub
    tq = nsub * tk
    half = tk // 2
    lead = ATTN_PRODUCE_LEAD
    per_trip = ATTN_BLOCKS_PER_TRIP
    assert lead <= nchunk and per_trip % 2 == 0 and nsub % 2 == 0 and (nsub * nq) % per_trip == 0
    ones_rows = jnp.ones((BF16_SUBLANE_PACK, tk), BF16)
    lam = (jnp.exp(jnp.sum(lq1_ref[...] * lk1_ref[...], axis=-1, keepdims=True))
           - jnp.exp(jnp.sum(lq2_ref[...] * lk2_ref[...], axis=-1, keepdims=True))
           + LAMBDA_INIT_L0)

    def col(c):
        return slice(c * tk, (c + 1) * tk)

    def query_block(g):
        sc = g % ATTN_SCRATCH_SETS

        def q_columns(c):
            return qT_ref[0, 0, g * nsub + c // 2, :, (c % 2) * tk:(c % 2 + 1) * tk]

        def produce(slot, kv, c):
            start = pl.multiple_of(kv * tk, tk)
            s = _dot(k_ref[0, pl.ds(start, tk), :], q_columns(c))
            s_ref[sc, slot, :, col(c)] = s
            cmax_ref[sc, slot, :, col(c)] = jnp.max(s, axis=0, keepdims=True)

        def produce_diagonal(slot, kv, c):
            start = pl.multiple_of(kv * tk, tk)
            lo = c * tk
            s_ref[sc, slot, 0:half, lo:lo + tk] = _dot(k_ref[0, pl.ds(start, half), :], q_columns(c))
            s_ref[sc, slot, half:tk, lo + half:lo + tk] = _dot(
                k_ref[0, pl.ds(start + half, half), :], q_columns(c)[:, half:tk])

        def update(s, cmax, kv, cols, nrows):
            m_prev = m_ref[sc, :, cols]
            m_new = jnp.maximum(m_prev, cmax)
            alpha = jnp.exp2(m_prev - m_new)
            p = jnp.exp2(s - m_new).astype(BF16)
            v_ext = jnp.concatenate([vT_ref[0, 0, kv, :, 0:nrows], ones_rows[:, 0:nrows]], axis=0)
            acc_ref[sc, :, cols] = alpha * acc_ref[sc, :, cols] + _dot(v_ext, p)
            m_ref[sc, :, cols] = m_new

        def consume(slot, kv, c, diagonal=False):
            if not diagonal:
                update(s_ref[sc, slot, :, col(c)], cmax_ref[sc, slot, :, col(c)], kv, col(c), tk)
                return
            lo = c * tk
            for cols, nrows in ((slice(lo, lo + half), half), (slice(lo + half, lo + tk), tk)):
                s = s_ref[sc, slot, 0:nrows, cols]
                r = lax.broadcasted_iota(jnp.int32, s.shape, 0)
                qpos = lax.broadcasted_iota(jnp.int32, s.shape, 1) + (cols.start - lo)
                s = jnp.where(r <= qpos, s, -jnp.inf)
                update(s, jnp.max(s, axis=0, keepdims=True), kv, cols, nrows)

        def finalize(sub):
            c1 = sub * sw
            c2 = c1 + tk
            o1 = acc_ref[sc, 0:dv, c1:c1 + tk] / acc_ref[sc, dv:dv + 1, c1:c1 + tk]
            o2 = acc_ref[sc, 0:dv, c2:c2 + tk] / acc_ref[sc, dv:dv + 1, c2:c2 + tk]
            o = (o1 - lam * o2).T
            ms = jnp.mean(o * o, axis=-1, keepdims=True)
            o = o * lax.rsqrt(ms + EPS) * normw_ref[...] * (1.0 - LAMBDA_INIT_L0)
            rows = slice(g * tq + sub * tk, g * tq + (sub + 1) * tk)
            o_ref[0, rows, :] = (o * za_ref[0, rows, :].astype(F32)).astype(BF16)

        m_ref[sc] = jnp.full(m_ref.shape[1:], -jnp.inf, F32)
        acc_ref[sc] = jnp.zeros(acc_ref.shape[1:], F32)

        for c in range(lead):
            produce(0, 0, c)

        def off_diagonal(kv, nblocks):
            for idx in range(nblocks * nchunk):
                ahead = idx + lead
                produce((ahead // nchunk) % 2, kv + ahead // nchunk, ahead % nchunk)
                consume((idx // nchunk) % 2, kv + idx // nchunk, idx % nchunk)

        trips = (nsub * nq // per_trip) * step + (nsub * g) // per_trip
        left = (nsub * g) % per_trip

        def trip(i, carry):
            off_diagonal(i * per_trip, per_trip)
            return carry

        lax.fori_loop(0, trips, trip, 0)
        if left:
            off_diagonal(trips * per_trip, left)

        kv0 = trips * per_trip + left
        units = [(d, c) for d in range(nsub) for c in range(nchunk) if c // 2 >= d]
        assert units[:lead] == [(0, c) for c in range(lead)]
        for i, (d, c) in enumerate(units):
            if i + lead < len(units):
                nd, nc = units[i + lead]
                (produce_diagonal if nc // 2 == nd else produce)(nd % 2, kv0 + nd, nc)
            consume(d % 2, kv0 + d, c, diagonal=(c // 2 == d))
            if c % 2 == 1 and c // 2 == d:
                finalize(d)

    for g in range(nq):
        query_block(g)


def _attention(q_t, k, v_t, za, lq1, lk1, lq2, lk2, attn_norm_w):
    bsz, seq, d_attn = k.shape
    hw = ATTN_V_DIM
    nh = d_attn // hw
    tk = TK_ATTN
    nsub = NSUB_ATTN
    nq = NQ_ATTN
    tq = nsub * tk
    rows = nq * tq
    nblk = seq // tk
    assert nsub % 2 == 0 and seq % rows == 0
    assert q_t.shape == (bsz, nh, nblk, hw, 2 * tk) and v_t.shape == (bsz, nh, nblk, hw, tk)
    grid = (bsz, nh, seq // rows)
    qrow = lambda b, h, i: (b, i, h)
    const2 = lambda b, h, i: (0, 0)
    vec = pl.BlockSpec((1, ATTN_QK_DIM), const2)
    in_specs = [
        pl.BlockSpec((1, 1, nq * nsub, hw, 2 * tk), lambda b, h, i: (b, h, i, 0, 0)),
        pl.BlockSpec((1, seq, hw), lambda b, h, i: (b, 0, h)),
        pl.BlockSpec((1, 1, nblk, hw, tk), lambda b, h, i: (b, h, 0, 0, 0)),
        pl.BlockSpec((1, rows, hw), qrow),
        vec, vec, vec, vec,
        pl.BlockSpec((1, hw), const2),
    ]
    width = 2 * tq
    nscr = ATTN_SCRATCH_SETS
    assert nscr >= min(nq, 2)
    vmem = (2 * 2 * seq * hw * 2 + nscr * 2 * tk * width * 4
            + nscr * (2 * hw + BF16_SUBLANE_PACK) * width * 4 + 4 * tk * width * 4)
    return pl.pallas_call(
        functools.partial(_attn_kernel, tk=tk, nsub=nsub, nq=nq), grid=grid, in_specs=in_specs,
        out_specs=pl.BlockSpec((1, rows, hw), qrow),
        out_shape=jax.ShapeDtypeStruct((bsz, seq, d_attn), BF16),
        scratch_shapes=[pltpu.VMEM((nscr, 2, tk, width), F32),
                        pltpu.VMEM((nscr, 2, 1, width), F32), pltpu.VMEM((nscr, 1, width), F32),
                        pltpu.VMEM((nscr, hw + BF16_SUBLANE_PACK, width), F32)],
        compiler_params=pltpu.CompilerParams(
            dimension_semantics=("arbitrary", "arbitrary", "arbitrary"),
            vmem_limit_bytes=_vmem_limit(vmem)),
        name="diff_attention",
    )(q_t, k, v_t, za, lq1.reshape(1, -1), lk1.reshape(1, -1), lq2.reshape(1, -1),
      lk2.reshape(1, -1), attn_norm_w.reshape(1, -1))


def _outproj_kernel(ys_ref, ya_ref, w_ref, x_ref, gate_ref, g_ref, b_ref, o_ref, *, d_ssd, alpha,
                    tm, rows_per_pass):
    for r0 in range(0, tm, rows_per_pass):
        rows = slice(r0, r0 + rows_per_pass)
        mixed = (_dot(ys_ref[0, rows, :], w_ref[0:d_ssd, :])
                 + _dot(ya_ref[0, rows, :], w_ref[d_ssd:, :]))
        xf = alpha * x_ref[0, rows, :] + gate_ref[0] * mixed
        mu = jnp.mean(xf, axis=-1, keepdims=True)
        xc = xf - mu
        var = jnp.mean(xc * xc, axis=-1, keepdims=True)
        o_ref[0, rows, :] = xc * lax.rsqrt(var + EPS) * g_ref[...] + b_ref[...]


def _out_projection(y_ssd, y_attn, w_out, x, gate, ln_g, ln_b, alpha):
    bsz, seq, d = x.shape
    d_ssd = y_ssd.shape[-1]
    d_attn = y_attn.shape[-1]
    tm = TM_OUT
    row = lambda b, l: (b, l, 0)
    const2 = lambda b, l: (0, 0)
    in_specs = [
        pl.BlockSpec((1, tm, d_ssd), row),
        pl.BlockSpec((1, tm, d_attn), row),
        pl.BlockSpec((d_ssd + d_attn, d), const2, pipeline_mode=pl.Buffered(1)),
        pl.BlockSpec((1, tm, d), row),
        pl.BlockSpec((1, 1, d), lambda b, l: (b, 0, 0)),
        pl.BlockSpec((1, d), const2),
        pl.BlockSpec((1, d), const2),
    ]
    vmem = (d_ssd + d_attn) * d * 2 + 2 * tm * (d_ssd + d_attn) * 2 + 4 * tm * d * 4 + 4 * tm * d * 4
    return pl.pallas_call(
        functools.partial(_outproj_kernel, d_ssd=d_ssd, alpha=alpha, tm=tm,
                          rows_per_pass=TM_OUT_PASS),
        grid=(bsz, seq // tm), in_specs=in_specs,
        out_specs=pl.BlockSpec((1, tm, d), row),
        out_shape=jax.ShapeDtypeStruct((bsz, seq, d), F32),
        compiler_params=pltpu.CompilerParams(
            dimension_semantics=("arbitrary", "arbitrary"), vmem_limit_bytes=_vmem_limit(vmem)),
        name="out_projection",
    )(y_ssd, y_attn, w_out, x, gate, ln_g.reshape(1, -1), ln_b.reshape(1, -1))


def _layer(x, c, w_ada, b_ada, w_in, conv_w, conv_b, dt_bias, a_log, d_skip, ssd_norm_w,
           lambda_q1, lambda_k1, lambda_q2, lambda_k2, attn_norm_w, w_out, ln_g, ln_b, depth):
    bsz, seq, d = x.shape
    nheads = a_log.shape[0]
    d_ssd = nheads * SSD_HEAD_DIM
    d_xbc = conv_w.shape[1]
    d_attn = w_out.shape[0] - d_ssd
    assert d_xbc == d_ssd + 2 * SSD_GROUPS * SSD_STATE
    assert w_in.shape[1] == d_ssd + d_xbc + nheads + 4 * d_attn
    assert seq % TM_INPROJ == 0 and seq % TM_OUT == 0 and seq % CHUNK == 0
    assert TM_INPROJ == TK_ATTN, "the in-projection writes q^T / v^T in attention-sized time blocks"

    mod = _modulation(c, w_ada, b_ada)
    shift = mod[:, None, 0:d]
    scale = mod[:, None, d:2 * d]
    gate = mod[:, None, 2 * d:3 * d]

    off_dt = d_ssd + d_xbc
    off_q = off_dt + nheads
    off_k, off_v, off_za = off_q + d_attn, off_q + 2 * d_attn, off_q + 3 * d_attn
    w_t = w_in.T
    w_main = tuple(w_t[lo:hi].astype(BF16).T
                   for lo, hi in ((0, off_dt), (off_k, off_v), (off_za, off_za + d_attn)))
    w_qv_t = jnp.concatenate([w_t[off_q:off_k], w_t[off_v:off_za]], axis=0).astype(BF16)
    w_dt = w_t[off_dt:off_q]

    zs, xs, bm, cm, q_t, k, v_t, za, dt, dt_t = _in_projection(
        x, scale, shift, w_main, w_qv_t, w_dt, conv_w, conv_b, dt_bias,
        d_ssd=d_ssd, d_xbc=d_xbc, d_attn=d_attn)
    y_ssd = _ssd(xs, bm, cm, zs, dt, dt_t, a_log, d_skip, ssd_norm_w)
    y_attn = _attention(q_t, k, v_t, za, lambda_q1, lambda_k1, lambda_q2, lambda_k2, attn_norm_w)
    alpha = (2.0 * depth) ** 0.25
    return _out_projection(y_ssd, y_attn, w_out.astype(BF16), x, gate, ln_g, ln_b, alpha)


def kernel(x, c, w_ada, b_ada, w_in, conv_w, conv_b, dt_bias, a_log, d_skip, ssd_norm_w, lambda_q1,
           lambda_k1, lambda_q2, lambda_k2, attn_norm_w, w_out, ln_g, ln_b):
    depth = w_in.shape[0]
    assert depth == 1, "lambda_init and the layer loop are specialised to a single layer"
    return _layer(x, c, w_ada[0], b_ada[0], w_in[0], conv_w[0], conv_b[0], dt_bias[0], a_log[0],
                  d_skip[0], ssd_norm_w[0], lambda_q1[0], lambda_k1[0], lambda_q2[0], lambda_k2[0],
                  attn_norm_w[0], w_out[0], ln_g[0], ln_b[0], depth)
```

```python
import functools
import math

import jax
import jax.numpy as jnp
from jax import lax
from jax.experimental import pallas as pl
from jax.experimental.pallas import tpu as pltpu

F32 = jnp.float32
BF16 = jnp.bfloat16

SSD_HEAD_DIM = 64
SSD_GROUPS = 2
SSD_STATE = 128
CONV_WIDTH = 4
CHUNK = 128
ATTN_QK_DIM = 64
ATTN_V_DIM = 2 * ATTN_QK_DIM
EPS = 1e-5
LAMBDA_INIT_L0 = 0.8 - 0.6 * math.exp(-0.3 * 0)
LOG2_E = math.log2(math.e)

V7X_LANES = 128
V7X_SUBLANES = 8
BF16_SUBLANE_PACK = 16
V7X_VMEM_BYTES = 64 * 1024 * 1024
V7X_VMEM_DEFAULT_SCOPED_BYTES = V7X_VMEM_BYTES // 2
V7X_VMEM_REQUEST_CAP_BYTES = V7X_VMEM_BYTES * 7 // 8

TM_INPROJ = 512
TM_INPROJ_PASS = 256
TK_ATTN = 512
NSUB_ATTN = 2
NQ_ATTN = 4
ATTN_SCRATCH_SETS = 2
ATTN_PRODUCE_LEAD = 2
ATTN_BLOCKS_PER_TRIP = 4
TM_OUT = 1024
TM_OUT_PASS = 256
SSD_CHUNKS_PER_STEP = 4


def _vmem_limit(estimate_bytes):
    return int(min(V7X_VMEM_REQUEST_CAP_BYTES, max(V7X_VMEM_DEFAULT_SCOPED_BYTES, estimate_bytes)))


def _silu(x):
    hx = 0.5 * x
    return hx + hx * jnp.tanh(hx)


def _softplus(x):
    return jnp.maximum(x, 0.0) + jnp.log(1.0 + jnp.exp(-jnp.abs(x)))


def _split_bf16(a, pieces):
    out = []
    r = a
    for i in range(pieces):
        p = r.astype(BF16)
        out.append(p)
        if i + 1 < pieces:
            r = r - p.astype(F32)
    return out


def _dot(a, b):
    return jnp.dot(a, b, preferred_element_type=F32)


def _dot_nt(a, b):
    return lax.dot_general(a, b, (((1,), (1,)), ((), ())), preferred_element_type=F32)


def _dot_tn(a, b):
    return lax.dot_general(a, b, (((0,), (0,)), ((), ())), preferred_element_type=F32)


def _mod_kernel(c_ref, w_ref, b_ref, o_ref):
    c_hi, c_lo = _split_bf16(c_ref[...], 2)
    w_hi, w_lo = _split_bf16(w_ref[...], 2)
    o_ref[...] = _dot(c_hi, w_hi) + _dot(c_hi, w_lo) + _dot(c_lo, w_hi) + b_ref[...]


def _modulation(c, w_ada, b_ada):
    bsz, d = c.shape
    n = w_ada.shape[1]
    rows = V7X_SUBLANES
    c_pad = jnp.zeros((rows, d), F32).at[:bsz].set(c)
    bn = d
    out = pl.pallas_call(
        _mod_kernel,
        grid=(n // bn,),
        in_specs=[pl.BlockSpec((rows, d), lambda j: (0, 0)),
                  pl.BlockSpec((d, bn), lambda j: (0, j)),
                  pl.BlockSpec((1, bn), lambda j: (0, j))],
        out_specs=pl.BlockSpec((rows, bn), lambda j: (0, j)),
        out_shape=jax.ShapeDtypeStruct((rows, n), F32),
        name="modulation",
    )(c_pad, w_ada, b_ada.reshape(1, n))
    return out[:bsz]


def _inproj_kernel(x_ref, scale_ref, shift_ref, wzx_ref, wk_ref, wza_ref, wqvT_ref, wdt_ref,
                   convw_ref, convb_ref, dtb_ref,
                   zs_ref, xs_ref, bm_ref, cm_ref, qT_ref, k_ref, vT_ref, za_ref, dt_ref, dtT_ref,
                   carry_ref, *, tm, rows_per_pass, d_ssd, d_xbc, d_attn, n_bc):
    @pl.when(pl.program_id(1) == 0)
    def _():
        carry_ref[...] = jnp.zeros_like(carry_ref)

    pad = V7X_SUBLANES
    nh = dt_ref.shape[-1]
    nh_attn = d_attn // ATTN_V_DIM
    cw = convw_ref[...]
    w_hi, w_lo = _split_bf16(wdt_ref[...], 2)
    wrow = lax.broadcasted_iota(jnp.int32, (V7X_LANES, 1), 0)
    w_hilo = jnp.where(wrow < nh, w_hi, w_lo)

    tail = carry_ref[...]
    for r0 in range(0, tm, rows_per_pass):
        rows = slice(r0, r0 + rows_per_pass)
        h32 = x_ref[0, rows, :] * (1.0 + scale_ref[0]) + shift_ref[0]
        h = h32.astype(BF16)

        u = _dot(h, wzx_ref[:, d_ssd:d_ssd + d_xbc])
        u_ext = jnp.concatenate([tail, u], axis=0)
        tail = u[rows_per_pass - pad:rows_per_pass, :]
        acc = convb_ref[...] + cw[CONV_WIDTH - 1:CONV_WIDTH, :] * u
        for tap in range(CONV_WIDTH - 1):
            shift = CONV_WIDTH - 1 - tap
            acc = acc + (cw[tap:tap + 1, :]
                         * pltpu.roll(u_ext, shift, 0)[pad:pad + rows_per_pass, :])
        xbc = _silu(acc)
        xs_ref[0, rows, :] = xbc[:, :d_ssd].astype(BF16)
        bm_ref[0, rows, :] = xbc[:, d_ssd:d_ssd + n_bc].astype(BF16)
        cm_ref[0, rows, :] = xbc[:, d_ssd + n_bc:d_ssd + 2 * n_bc].astype(BF16)

        zs_ref[0, rows, :] = _silu(_dot(h, wzx_ref[:, 0:d_ssd])).astype(BF16)
        k_ref[0, rows, :] = _dot(h, wk_ref[...]).astype(BF16)
        za_ref[0, rows, :] = _silu(_dot(h, wza_ref[...])).astype(BF16)

        qvT = _dot_nt(wqvT_ref[...], h)
        qT = qvT[:d_attn] * (ATTN_QK_DIM ** -0.5 * LOG2_E)
        qT = qT.astype(BF16).reshape(nh_attn, ATTN_V_DIM, rows_per_pass)
        feat = lax.broadcasted_iota(jnp.int32, (1, ATTN_V_DIM, 1), 1)
        zero = jnp.zeros_like(qT)
        qT_ref[0, :, 0, :, rows] = jnp.where(feat < ATTN_QK_DIM, qT, zero)
        qT_ref[0, :, 0, :, tm + r0:tm + r0 + rows_per_pass] = jnp.where(feat >= ATTN_QK_DIM, qT, zero)
        vT_ref[0, :, 0, :, rows] = qvT[d_attn:].astype(BF16).reshape(nh_attn, ATTN_V_DIM,
                                                                      rows_per_pass)

        h_lo = (h32 - h.astype(F32)).astype(BF16)
        r1 = _dot_nt(h, w_hilo)
        r2 = _dot_nt(h_lo, w_hi)
        dt_raw = r1 + pltpu.roll(r1, V7X_LANES - nh, 1) + r2
        dt = _softplus(dt_raw + dtb_ref[...])
        dt_ref[0, rows, :] = dt[:, :nh]
        dtT_ref[0, :, rows] = dt.T[:nh, :]
    carry_ref[...] = tail


def _in_projection(x, scale, shift, w_main, w_qv_t, w_dt, conv_w, conv_b, dt_bias, *, d_ssd, d_xbc,
                   d_attn):
    bsz, seq, d = x.shape
    tm = TM_INPROJ
    nh = w_dt.shape[0]
    nh_attn = d_attn // ATTN_V_DIM
    n_bc = (d_xbc - d_ssd) // 2
    w_zx, w_k, w_za = w_main
    assert w_zx.shape == (d, d_ssd + d_xbc) and w_k.shape == (d, d_attn) and w_za.shape == (d, d_attn)
    n_main = w_zx.shape[1] + w_k.shape[1] + w_za.shape[1]
    grid = (bsz, seq // tm)

    row = lambda b, l: (b, l, 0)
    head_major = lambda b, l: (b, 0, l, 0, 0)
    const2 = lambda b, l: (0, 0)
    per_b = lambda b, l: (b, 0, 0)
    single = pl.Buffered(1)
    in_specs = [
        pl.BlockSpec((1, tm, d), row),
        pl.BlockSpec((1, 1, d), per_b),
        pl.BlockSpec((1, 1, d), per_b),
        pl.BlockSpec(w_zx.shape, const2, pipeline_mode=single),
        pl.BlockSpec(w_k.shape, const2, pipeline_mode=single),
        pl.BlockSpec(w_za.shape, const2, pipeline_mode=single),
        pl.BlockSpec((2 * d_attn, d), const2, pipeline_mode=single),
        pl.BlockSpec((V7X_LANES, d), const2),
        pl.BlockSpec((CONV_WIDTH, d_xbc), const2),
        pl.BlockSpec((1, d_xbc), const2),
        pl.BlockSpec((1, V7X_LANES), const2),
    ]
    out_specs = [
        pl.BlockSpec((1, tm, d_ssd), row),
        pl.BlockSpec((1, tm, d_ssd), row),
        pl.BlockSpec((1, tm, n_bc), row),
        pl.BlockSpec((1, tm, n_bc), row),
        pl.BlockSpec((1, nh_attn, 1, ATTN_V_DIM, 2 * tm), head_major),
        pl.BlockSpec((1, tm, d_attn), row),
        pl.BlockSpec((1, nh_attn, 1, ATTN_V_DIM, tm), head_major),
        pl.BlockSpec((1, tm, d_attn), row),
        pl.BlockSpec((1, tm, nh), row),
        pl.BlockSpec((1, nh, tm), lambda b, l: (b, 0, l)),
    ]
    out_shape = [
        jax.ShapeDtypeStruct((bsz, seq, d_ssd), BF16),
        jax.ShapeDtypeStruct((bsz, seq, d_ssd), BF16),
        jax.ShapeDtypeStruct((bsz, seq, n_bc), BF16),
        jax.ShapeDtypeStruct((bsz, seq, n_bc), BF16),
        jax.ShapeDtypeStruct((bsz, nh_attn, seq // tm, ATTN_V_DIM, 2 * tm), BF16),
        jax.ShapeDtypeStruct((bsz, seq, d_attn), BF16),
        jax.ShapeDtypeStruct((bsz, nh_attn, seq // tm, ATTN_V_DIM, tm), BF16),
        jax.ShapeDtypeStruct((bsz, seq, d_attn), BF16),
        jax.ShapeDtypeStruct((bsz, seq, nh), F32),
        jax.ShapeDtypeStruct((bsz, nh, seq), F32),
    ]
    pad = V7X_SUBLANES
    vmem = (d * (n_main + 2 * d_attn) * 2 + 2 * tm * d * 4
            + 2 * tm * (2 * d_ssd + 2 * n_bc + 4 * d_attn) * 2
            + (tm + 2 * pad) * d_xbc * 4 + 6 * tm * d_xbc * 4)
    kern = functools.partial(_inproj_kernel, tm=tm, rows_per_pass=TM_INPROJ_PASS, d_ssd=d_ssd,
                             d_xbc=d_xbc, d_attn=d_attn, n_bc=n_bc)
    assert 2 * nh <= V7X_LANES
    w_dt_pad = jnp.concatenate([w_dt, w_dt, jnp.zeros((V7X_LANES - 2 * nh, d), F32)], axis=0)
    dt_bias_pad = jnp.concatenate([dt_bias, jnp.zeros((V7X_LANES - nh,), F32)]).reshape(1, -1)
    return pl.pallas_call(
        kern, grid=grid, in_specs=in_specs, out_specs=out_specs, out_shape=out_shape,
        scratch_shapes=[pltpu.VMEM((pad, d_xbc), F32)],
        compiler_params=pltpu.CompilerParams(
            dimension_semantics=("arbitrary", "arbitrary"), vmem_limit_bytes=_vmem_limit(vmem)),
        name="in_projection",
    )(x, scale, shift, w_zx, w_k, w_za, w_qv_t, w_dt_pad, conv_w, conv_b.reshape(1, -1), dt_bias_pad)


def _ssd_kernel(xs_ref, bm_ref, cm_ref, zs_ref, dt_ref, dtT_ref, alog_ref, alogT_ref, dskip_ref,
                normw_ref, y_ref, state_ref, ybuf_ref, *, nheads, d_ssd, cps):
    t = CHUNK
    n = SSD_STATE
    hp = SSD_HEAD_DIM
    gw = d_ssd // SSD_GROUPS
    heads_per_group = nheads // SSD_GROUPS

    @pl.when(pl.program_id(1) == 0)
    def _():
        state_ref[...] = jnp.zeros_like(state_ref)

    row = lax.broadcasted_iota(jnp.int32, (t, t), 0)
    col = lax.broadcasted_iota(jnp.int32, (t, t), 1)
    causal = col <= row
    tri = jnp.where(causal, 1.0, 0.0).astype(BF16)
    tri_t = jnp.where(row <= col, 1.0, 0.0).astype(BF16)
    eh = lax.broadcasted_iota(jnp.int32, (nheads, d_ssd), 0)
    ej = lax.broadcasted_iota(jnp.int32, (nheads, d_ssd), 1)
    expand_m = jnp.where(ej // hp == eh, 1.0, 0.0).astype(BF16)
    lane = lax.broadcasted_iota(jnp.int32, (1, 2 * hp), 1)
    neg_inf = jnp.float32(-jnp.inf)
    a = -jnp.exp(alog_ref[...])
    a_t = -jnp.exp(alogT_ref[...])

    def expand(v, pieces):
        return sum(_dot(p, expand_m) for p in _split_bf16(v, pieces))

    for ci in range(cps):
        rows = slice(ci * t, (ci + 1) * t)
        dt = dt_ref[0, rows, :]
        dt_t = dtT_ref[0, :, rows]
        acum = sum(_dot(tri, p) for p in _split_bf16(dt * a, 3))
        acum_t = sum(_dot(p, tri_t) for p in _split_bf16(dt_t * a_t, 3))
        last = acum[t - 1:t, :]
        dt_x = expand(dt, 1)
        w_x = expand(jnp.exp(last - acum), 1)
        e_x = expand(jnp.exp(acum), 2)

        xs = xs_ref[0, rows, :].astype(F32)
        xdt = xs * dt_x
        xdt_b = xdt.astype(BF16)
        xw_b = (xdt * w_x).astype(BF16)

        for g in range(SSD_GROUPS):
            b_g = bm_ref[0, rows, g * n:(g + 1) * n]
            c_g = cm_ref[0, rows, g * n:(g + 1) * n]
            gsl = slice(g * gw, (g + 1) * gw)
            cb = _dot_nt(c_g, b_g)
            prev = state_ref[g]
            y_off = _dot(c_g, prev.astype(BF16)) * e_x[:, gsl]
            state_ref[g] = prev * e_x[t - 1:t, gsl] + _dot_tn(b_g, xw_b[:, gsl])
            for pair in range(heads_per_group // 2):
                lo = g * gw + pair * 2 * hp
                x_pair = xdt_b[:, lo:lo + 2 * hp]
                decays = []
                for half in range(2):
                    hd = g * heads_per_group + pair * 2 + half
                    diff = acum[:, hd:hd + 1] - acum_t[hd:hd + 1, :]
                    decays.append((cb * jnp.exp(jnp.where(causal, diff, neg_inf))).astype(BF16))
                zero = jnp.zeros_like(x_pair)
                x_diag = jnp.concatenate([jnp.where(lane < hp, x_pair, zero),
                                          jnp.where(lane >= hp, x_pair, zero)], axis=0)
                y_pair = (y_off[:, pair * 2 * hp:(pair + 1) * 2 * hp]
                          + _dot(jnp.concatenate(decays, axis=1), x_diag))
                ybuf_ref[rows, lo:lo + 2 * hp] = y_pair

    xs = xs_ref[0].astype(F32)
    y = ybuf_ref[...] + dskip_ref[...] * xs
    y = y * zs_ref[0].astype(F32)
    for g in range(SSD_GROUPS):
        gsl = slice(g * gw, (g + 1) * gw)
        y_g = y[:, gsl]
        ms = jnp.mean(y_g * y_g, axis=-1, keepdims=True)
        y_ref[0, :, gsl] = (y_g * lax.rsqrt(ms + EPS) * normw_ref[:, gsl]).astype(BF16)


def _ssd(xs, bm, cm, zs, dt, dt_t, a_log, d_skip, ssd_norm_w):
    bsz, seq, d_ssd = xs.shape
    nheads = dt.shape[-1]
    n_bc = bm.shape[-1]
    cps = SSD_CHUNKS_PER_STEP
    t = cps * CHUNK
    assert seq % t == 0
    grid = (bsz, seq // t)
    row = lambda b, c: (b, c, 0)
    const2 = lambda b, c: (0, 0)
    in_specs = [
        pl.BlockSpec((1, t, d_ssd), row),
        pl.BlockSpec((1, t, n_bc), row),
        pl.BlockSpec((1, t, n_bc), row),
        pl.BlockSpec((1, t, d_ssd), row),
        pl.BlockSpec((1, t, nheads), row),
        pl.BlockSpec((1, nheads, t), lambda b, c: (b, 0, c)),
        pl.BlockSpec((1, nheads), const2),
        pl.BlockSpec((nheads, 1), const2),
        pl.BlockSpec((1, d_ssd), const2),
        pl.BlockSpec((1, d_ssd), const2),
    ]
    kern = functools.partial(_ssd_kernel, nheads=nheads, d_ssd=d_ssd, cps=cps)
    return pl.pallas_call(
        kern, grid=grid, in_specs=in_specs,
        out_specs=pl.BlockSpec((1, t, d_ssd), row),
        out_shape=jax.ShapeDtypeStruct((bsz, seq, d_ssd), BF16),
        scratch_shapes=[pltpu.VMEM((SSD_GROUPS, SSD_STATE, d_ssd // SSD_GROUPS), F32),
                        pltpu.VMEM((t, d_ssd), F32)],
        compiler_params=pltpu.CompilerParams(dimension_semantics=("arbitrary", "arbitrary")),
        name="ssd_scan",
    )(xs, bm, cm, zs, dt, dt_t, a_log.reshape(1, -1), a_log.reshape(-1, 1),
      jnp.repeat(d_skip, SSD_HEAD_DIM).reshape(1, -1), ssd_norm_w.reshape(1, -1))


def _attn_kernel(qT_ref, k_ref, vT_ref, za_ref, lq1_ref, lk1_ref, lq2_ref, lk2_ref, normw_ref,
                 o_ref, s_ref, cmax_ref, m_ref, acc_ref, *, tk, nsub, nq):
    step = pl.program_id(2)
    dk = ATTN_QK_DIM
    dv = 2 * dk
    sw = 2 * tk
    nchunk = 2 * nsub
    tq = nsub * tk
    half = tk // 2
    lead = ATTN_PRODUCE_LEAD
    per_trip = ATTN_BLOCKS_PER_TRIP
    assert lead <= nchunk and per_trip % 2 == 0 and nsub % 2 == 0 and (nsub * nq) % per_trip == 0
    ones_rows = jnp.ones((BF16_SUBLANE_PACK, tk), BF16)
    lam = (jnp.exp(jnp.sum(lq1_ref[...] * lk1_ref[...], axis=-1, keepdims=True))
           - jnp.exp(jnp.sum(lq2_ref[...] * lk2_ref[...], axis=-1, keepdims=True))
           + LAMBDA_INIT_L0)

    def col(c):
        return slice(c * tk, (c + 1) * tk)

    def query_block(g):
        sc = g % ATTN_SCRATCH_SETS

        def q_columns(c):
            return qT_ref[0, 0, g * nsub + c // 2, :, (c % 2) * tk:(c % 2 + 1) * tk]

        def produce(slot, kv, c):
            start = pl.multiple_of(kv * tk, tk)
            s = _dot(k_ref[0, pl.ds(start, tk), :], q_columns(c))
            s_ref[sc, slot, :, col(c)] = s
            cmax_ref[sc, slot, :, col(c)] = jnp.max(s, axis=0, keepdims=True)

        def produce_diagonal(slot, kv, c):
            start = pl.multiple_of(kv * tk, tk)
            lo = c * tk
            s_ref[sc, slot, 0:half, lo:lo + tk] = _dot(k_ref[0, pl.ds(start, half), :], q_columns(c))
            s_ref[sc, slot, half:tk, lo + half:lo + tk] = _dot(
                k_ref[0, pl.ds(start + half, half), :], q_columns(c)[:, half:tk])

        def update(s, cmax, kv, cols, nrows):
            m_prev = m_ref[sc, :, cols]
            m_new = jnp.maximum(m_prev, cmax)
            alpha = jnp.exp2(m_prev - m_new)
            p = jnp.exp2(s - m_new).astype(BF16)
            v_ext = jnp.concatenate([vT_ref[0, 0, kv, :, 0:nrows], ones_rows[:, 0:nrows]], axis=0)
            acc_ref[sc, :, cols] = alpha * acc_ref[sc, :, cols] + _dot(v_ext, p)
            m_ref[sc, :, cols] = m_new

        def consume(slot, kv, c, diagonal=False):
            if not diagonal:
                update(s_ref[sc, slot, :, col(c)], cmax_ref[sc, slot, :, col(c)], kv, col(c), tk)
                return
            lo = c * tk
            for cols, nrows in ((slice(lo, lo + half), half), (slice(lo + half, lo + tk), tk)):
                s = s_ref[sc, slot, 0:nrows, cols]
                r = lax.broadcasted_iota(jnp.int32, s.shape, 0)
                qpos = lax.broadcasted_iota(jnp.int32, s.shape, 1) + (cols.start - lo)
                s = jnp.where(r <= qpos, s, -jnp.inf)
                update(s, jnp.max(s, axis=0, keepdims=True), kv, cols, nrows)

        def finalize(sub):
            c1 = sub * sw
            c2 = c1 + tk
            o1 = acc_ref[sc, 0:dv, c1:c1 + tk] / acc_ref[sc, dv:dv + 1, c1:c1 + tk]
            o2 = acc_ref[sc, 0:dv, c2:c2 + tk] / acc_ref[sc, dv:dv + 1, c2:c2 + tk]
            o = (o1 - lam * o2).T
            ms = jnp.mean(o * o, axis=-1, keepdims=True)
            o = o * lax.rsqrt(ms + EPS) * normw_ref[...] * (1.0 - LAMBDA_INIT_L0)
            rows = slice(g * tq + sub * tk, g * tq + (sub + 1) * tk)
            o_ref[0, rows, :] = (o * za_ref[0, rows, :].astype(F32)).astype(BF16)

        m_ref[sc] = jnp.full(m_ref.shape[1:], -jnp.inf, F32)
        acc_ref[sc] = jnp.zeros(acc_ref.shape[1:], F32)

        for c in range(lead):
            produce(0, 0, c)

        def off_diagonal(kv, nblocks):
            for idx in range(nblocks * nchunk):
                ahead = idx + lead
                produce((ahead // nchunk) % 2, kv + ahead // nchunk, ahead % nchunk)
                consume((idx // nchunk) % 2, kv + idx // nchunk, idx % nchunk)

        trips = (nsub * nq // per_trip) * step + (nsub * g) // per_trip
        left = (nsub * g) % per_trip

        def trip(i, carry):
            off_diagonal(i * per_trip, per_trip)
            return carry

        lax.fori_loop(0, trips, trip, 0)
        if left:
            off_diagonal(trips * per_trip, left)

        kv0 = trips * per_trip + left
        units = [(d, c) for d in range(nsub) for c in range(nchunk) if c // 2 >= d]
        assert units[:lead] == [(0, c) for c in range(lead)]
        for i, (d, c) in enumerate(units):
            if i + lead < len(units):
                nd, nc = units[i + lead]
                (produce_diagonal if nc // 2 == nd else produce)(nd % 2, kv0 + nd, nc)
            consume(d % 2, kv0 + d, c, diagonal=(c // 2 == d))
            if c % 2 == 1 and c // 2 == d:
                finalize(d)

    for g in range(nq):
        query_block(g)


def _attention(q_t, k, v_t, za, lq1, lk1, lq2, lk2, attn_norm_w):
    bsz, seq, d_attn = k.shape
    hw = ATTN_V_DIM
    nh = d_attn // hw
    tk = TK_ATTN
    nsub = NSUB_ATTN
    nq = NQ_ATTN
    tq = nsub * tk
    rows = nq * tq
    nblk = seq // tk
    assert nsub % 2 == 0 and seq % rows == 0
    assert q_t.shape == (bsz, nh, nblk, hw, 2 * tk) and v_t.shape == (bsz, nh, nblk, hw, tk)
    grid = (bsz, nh, seq // rows)
    qrow = lambda b, h, i: (b, i, h)
    const2 = lambda b, h, i: (0, 0)
    vec = pl.BlockSpec((1, ATTN_QK_DIM), const2)
    in_specs = [
        pl.BlockSpec((1, 1, nq * nsub, hw, 2 * tk), lambda b, h, i: (b, h, i, 0, 0)),
        pl.BlockSpec((1, seq, hw), lambda b, h, i: (b, 0, h)),
        pl.BlockSpec((1, 1, nblk, hw, tk), lambda b, h, i: (b, h, 0, 0, 0)),
        pl.BlockSpec((1, rows, hw), qrow),
        vec, vec, vec, vec,
        pl.BlockSpec((1, hw), const2),
    ]
    width = 2 * tq
    nscr = ATTN_SCRATCH_SETS
    assert nscr >= min(nq, 2)
    vmem = (2 * 2 * seq * hw * 2 + nscr * 2 * tk * width * 4
            + nscr * (2 * hw + BF16_SUBLANE_PACK) * width * 4 + 4 * tk * width * 4)
    return pl.pallas_call(
        functools.partial(_attn_kernel, tk=tk, nsub=nsub, nq=nq), grid=grid, in_specs=in_specs,
        out_specs=pl.BlockSpec((1, rows, hw), qrow),
        out_shape=jax.ShapeDtypeStruct((bsz, seq, d_attn), BF16),
        scratch_shapes=[pltpu.VMEM((nscr, 2, tk, width), F32),
                        pltpu.VMEM((nscr, 2, 1, width), F32), pltpu.VMEM((nscr, 1, width), F32),
                        pltpu.VMEM((nscr, hw + BF16_SUBLANE_PACK, width), F32)],
        compiler_params=pltpu.CompilerParams(
            dimension_semantics=("arbitrary", "arbitrary", "arbitrary"),
            vmem_limit_bytes=_vmem_limit(vmem)),
        name="diff_attention",
    )(q_t, k, v_t, za, lq1.reshape(1, -1), lk1.reshape(1, -1), lq2.reshape(1, -1),
      lk2.reshape(1, -1), attn_norm_w.reshape(1, -1))


def _outproj_kernel(ys_ref, ya_ref, w_ref, x_ref, gate_ref, g_ref, b_ref, o_ref, *, d_ssd, alpha,
                    tm, rows_per_pass):
    for r0 in range(0, tm, rows_per_pass):
        rows = slice(r0, r0 + rows_per_pass)
        mixed = (_dot(ys_ref[0, rows, :], w_ref[0:d_ssd, :])
                 + _dot(ya_ref[0, rows, :], w_ref[d_ssd:, :]))
        xf = alpha * x_ref[0, rows, :] + gate_ref[0] * mixed
        mu = jnp.mean(xf, axis=-1, keepdims=True)
        xc = xf - mu
        var = jnp.mean(xc * xc, axis=-1, keepdims=True)
        o_ref[0, rows, :] = xc * lax.rsqrt(var + EPS) * g_ref[...] + b_ref[...]


def _out_projection(y_ssd, y_attn, w_out, x, gate, ln_g, ln_b, alpha):
    bsz, seq, d = x.shape
    d_ssd = y_ssd.shape[-1]
    d_attn = y_attn.shape[-1]
    tm = TM_OUT
    row = lambda b, l: (b, l, 0)
    const2 = lambda b, l: (0, 0)
    in_specs = [
        pl.BlockSpec((1, tm, d_ssd), row),
        pl.BlockSpec((1, tm, d_attn), row),
        pl.BlockSpec((d_ssd + d_attn, d), const2, pipeline_mode=pl.Buffered(1)),
        pl.BlockSpec((1, tm, d), row),
        pl.BlockSpec((1, 1, d), lambda b, l: (b, 0, 0)),
        pl.BlockSpec((1, d), const2),
        pl.BlockSpec((1, d), const2),
    ]
    vmem = (d_ssd + d_attn) * d * 2 + 2 * tm * (d_ssd + d_attn) * 2 + 4 * tm * d * 4 + 4 * tm * d * 4
    return pl.pallas_call(
        functools.partial(_outproj_kernel, d_ssd=d_ssd, alpha=alpha, tm=tm,
                          rows_per_pass=TM_OUT_PASS),
        grid=(bsz, seq // tm), in_specs=in_specs,
        out_specs=pl.BlockSpec((1, tm, d), row),
        out_shape=jax.ShapeDtypeStruct((bsz, seq, d), F32),
        compiler_params=pltpu.CompilerParams(
            dimension_semantics=("arbitrary", "arbitrary"), vmem_limit_bytes=_vmem_limit(vmem)),
        name="out_projection",
    )(y_ssd, y_attn, w_out, x, gate, ln_g.reshape(1, -1), ln_b.reshape(1, -1))


def _layer(x, c, w_ada, b_ada, w_in, conv_w, conv_b, dt_bias, a_log, d_skip, ssd_norm_w,
           lambda_q1, lambda_k1, lambda_q2, lambda_k2, attn_norm_w, w_out, ln_g, ln_b, depth):
    bsz, seq, d = x.shape
    nheads = a_log.shape[0]
    d_ssd = nheads * SSD_HEAD_DIM
    d_xbc = conv_w.shape[1]
    d_attn = w_out.shape[0] - d_ssd
    assert d_xbc == d_ssd + 2 * SSD_GROUPS * SSD_STATE
    assert w_in.shape[1] == d_ssd + d_xbc + nheads + 4 * d_attn
    assert seq % TM_INPROJ == 0 and seq % TM_OUT == 0 and seq % CHUNK == 0
    assert TM_INPROJ == TK_ATTN, "the in-projection writes q^T / v^T in attention-sized time blocks"

    mod = _modulation(c, w_ada, b_ada)
    shift = mod[:, None, 0:d]
    scale = mod[:, None, d:2 * d]
    gate = mod[:, None, 2 * d:3 * d]

    off_dt = d_ssd + d_xbc
    off_q = off_dt + nheads
    off_k, off_v, off_za = off_q + d_attn, off_q + 2 * d_attn, off_q + 3 * d_attn
    w_t = w_in.T
    w_main = tuple(w_t[lo:hi].astype(BF16).T
                   for lo, hi in ((0, off_dt), (off_k, off_v), (off_za, off_za + d_attn)))
    w_qv_t = jnp.concatenate([w_t[off_q:off_k], w_t[off_v:off_za]], axis=0).astype(BF16)
    w_dt = w_t[off_dt:off_q]

    zs, xs, bm, cm, q_t, k, v_t, za, dt, dt_t = _in_projection(
        x, scale, shift, w_main, w_qv_t, w_dt, conv_w, conv_b, dt_bias,
        d_ssd=d_ssd, d_xbc=d_xbc, d_attn=d_attn)
    y_ssd = _ssd(xs, bm, cm, zs, dt, dt_t, a_log, d_skip, ssd_norm_w)
    y_attn = _attention(q_t, k, v_t, za, lambda_q1, lambda_k1, lambda_q2, lambda_k2, attn_norm_w)
    alpha = (2.0 * depth) ** 0.25
    return _out_projection(y_ssd, y_attn, w_out.astype(BF16), x, gate, ln_g, ln_b, alpha)


def kernel(x, c, w_ada, b_ada, w_in, conv_w, conv_b, dt_bias, a_log, d_skip, ssd_norm_w, lambda_q1,
           lambda_k1, lambda_q2, lambda_k2, attn_norm_w, w_out, ln_g, ln_b):
    depth = w_in.shape[0]
    assert depth == 1, "lambda_init and the layer loop are specialised to a single layer"
    return _layer(x, c, w_ada[0], b_ada[0], w_in[0], conv_w[0], conv_b[0], dt_bias[0], a_log[0],
                  d_skip[0], ssd_norm_w[0], lambda_q1[0], lambda_k1[0], lambda_q2[0], lambda_k2[0],
                  attn_norm_w[0], w_out[0], ln_g[0], ln_b[0], depth)
```

```python
import functools
import math

import jax
import jax.numpy as jnp
from jax import lax
from jax.experimental import pallas as pl
from jax.experimental.pallas import tpu as pltpu

F32 = jnp.float32
BF16 = jnp.bfloat16

SSD_HEAD_DIM = 64
SSD_GROUPS = 2
SSD_STATE = 128
CONV_WIDTH = 4
CHUNK = 128
ATTN_QK_DIM = 64
ATTN_V_DIM = 2 * ATTN_QK_DIM
EPS = 1e-5
LAMBDA_INIT_L0 = 0.8 - 0.6 * math.exp(-0.3 * 0)
LOG2_E = math.log2(math.e)

V7X_LANES = 128
V7X_SUBLANES = 8
BF16_SUBLANE_PACK = 16
V7X_VMEM_BYTES = 64 * 1024 * 1024
V7X_VMEM_DEFAULT_SCOPED_BYTES = V7X_VMEM_BYTES // 2
V7X_VMEM_REQUEST_CAP_BYTES = V7X_VMEM_BYTES * 7 // 8

TM_INPROJ = 512
TM_INPROJ_PASS = 256
TK_ATTN = 512
NSUB_ATTN = 2
NQ_ATTN = 8
ATTN_SCRATCH_SETS = 2
ATTN_PRODUCE_LEAD = 2
ATTN_BLOCKS_PER_TRIP = 4
TM_OUT = 1024
TM_OUT_PASS = 256
SSD_CHUNKS_PER_STEP = 4


def _vmem_limit(estimate_bytes):
    return int(min(V7X_VMEM_REQUEST_CAP_BYTES, max(V7X_VMEM_DEFAULT_SCOPED_BYTES, estimate_bytes)))


def _silu(x):
    hx = 0.5 * x
    return hx + hx * jnp.tanh(hx)


def _softplus(x):
    return jnp.maximum(x, 0.0) + jnp.log(1.0 + jnp.exp(-jnp.abs(x)))


def _split_bf16(a, pieces):
    out = []
    r = a
    for i in range(pieces):
        p = r.astype(BF16)
        out.append(p)
        if i + 1 < pieces:
            r = r - p.astype(F32)
    return out


def _dot(a, b):
    return jnp.dot(a, b, preferred_element_type=F32)


def _dot_nt(a, b):
    return lax.dot_general(a, b, (((1,), (1,)), ((), ())), preferred_element_type=F32)


def _dot_tn(a, b):
    return lax.dot_general(a, b, (((0,), (0,)), ((), ())), preferred_element_type=F32)


def _mod_kernel(c_ref, w_ref, b_ref, o_ref):
    c_hi, c_lo = _split_bf16(c_ref[...], 2)
    w_hi, w_lo = _split_bf16(w_ref[...], 2)
    o_ref[...] = _dot(c_hi, w_hi) + _dot(c_hi, w_lo) + _dot(c_lo, w_hi) + b_ref[...]


def _modulation(c, w_ada, b_ada):
    bsz, d = c.shape
    n = w_ada.shape[1]
    rows = V7X_SUBLANES
    c_pad = jnp.zeros((rows, d), F32).at[:bsz].set(c)
    bn = d
    out = pl.pallas_call(
        _mod_kernel,
        grid=(n // bn,),
        in_specs=[pl.BlockSpec((rows, d), lambda j: (0, 0)),
                  pl.BlockSpec((d, bn), lambda j: (0, j)),
                  pl.BlockSpec((1, bn), lambda j: (0, j))],
        out_specs=pl.BlockSpec((rows, bn), lambda j: (0, j)),
        out_shape=jax.ShapeDtypeStruct((rows, n), F32),
        name="modulation",
    )(c_pad, w_ada, b_ada.reshape(1, n))
    return out[:bsz]


def _inproj_kernel(x_ref, scale_ref, shift_ref, wzx_ref, wk_ref, wza_ref, wqvT_ref, wdt_ref,
                   convw_ref, convb_ref, dtb_ref,
                   zs_ref, xs_ref, bm_ref, cm_ref, qT_ref, k_ref, vT_ref, za_ref, dt_ref, dtT_ref,
                   carry_ref, *, tm, rows_per_pass, d_ssd, d_xbc, d_attn, n_bc):
    @pl.when(pl.program_id(1) == 0)
    def _():
        carry_ref[...] = jnp.zeros_like(carry_ref)

    pad = V7X_SUBLANES
    nh = dt_ref.shape[-1]
    nh_attn = d_attn // ATTN_V_DIM
    cw = convw_ref[...]
    w_hi, w_lo = _split_bf16(wdt_ref[...], 2)
    wrow = lax.broadcasted_iota(jnp.int32, (V7X_LANES, 1), 0)
    w_hilo = jnp.where(wrow < nh, w_hi, w_lo)

    tail = carry_ref[...]
    for r0 in range(0, tm, rows_per_pass):
        rows = slice(r0, r0 + rows_per_pass)
        h32 = x_ref[0, rows, :] * (1.0 + scale_ref[0]) + shift_ref[0]
        h = h32.astype(BF16)

        u = _dot(h, wzx_ref[:, d_ssd:d_ssd + d_xbc])
        u_ext = jnp.concatenate([tail, u], axis=0)
        tail = u[rows_per_pass - pad:rows_per_pass, :]
        acc = convb_ref[...] + cw[CONV_WIDTH - 1:CONV_WIDTH, :] * u
        for tap in range(CONV_WIDTH - 1):
            shift = CONV_WIDTH - 1 - tap
            acc = acc + (cw[tap:tap + 1, :]
                         * pltpu.roll(u_ext, shift, 0)[pad:pad + rows_per_pass, :])
        xbc = _silu(acc)
        xs_ref[0, rows, :] = xbc[:, :d_ssd].astype(BF16)
        bm_ref[0, rows, :] = xbc[:, d_ssd:d_ssd + n_bc].astype(BF16)
        cm_ref[0, rows, :] = xbc[:, d_ssd + n_bc:d_ssd + 2 * n_bc].astype(BF16)

        zs_ref[0, rows, :] = _silu(_dot(h, wzx_ref[:, 0:d_ssd])).astype(BF16)
        k_ref[0, rows, :] = _dot(h, wk_ref[...]).astype(BF16)
        za_ref[0, rows, :] = _silu(_dot(h, wza_ref[...])).astype(BF16)

        qvT = _dot_nt(wqvT_ref[...], h)
        qT = qvT[:d_attn] * (ATTN_QK_DIM ** -0.5 * LOG2_E)
        qT = qT.astype(BF16).reshape(nh_attn, ATTN_V_DIM, rows_per_pass)
        feat = lax.broadcasted_iota(jnp.int32, (1, ATTN_V_DIM, 1), 1)
        zero = jnp.zeros_like(qT)
        qT_ref[0, :, 0, :, rows] = jnp.where(feat < ATTN_QK_DIM, qT, zero)
        qT_ref[0, :, 0, :, tm + r0:tm + r0 + rows_per_pass] = jnp.where(feat >= ATTN_QK_DIM, qT, zero)
        vT_ref[0, :, 0, :, rows] = qvT[d_attn:].astype(BF16).reshape(nh_attn, ATTN_V_DIM,
                                                                      rows_per_pass)

        h_lo = (h32 - h.astype(F32)).astype(BF16)
        r1 = _dot_nt(h, w_hilo)
        r2 = _dot_nt(h_lo, w_hi)
        dt_raw = r1 + pltpu.roll(r1, V7X_LANES - nh, 1) + r2
        dt = _softplus(dt_raw + dtb_ref[...])
        dt_ref[0, rows, :] = dt[:, :nh]
        dtT_ref[0, :, rows] = dt.T[:nh, :]
    carry_ref[...] = tail


def _in_projection(x, scale, shift, w_main, w_qv_t, w_dt, conv_w, conv_b, dt_bias, *, d_ssd, d_xbc,
                   d_attn):
    bsz, seq, d = x.shape
    tm = TM_INPROJ
    nh = w_dt.shape[0]
    nh_attn = d_attn // ATTN_V_DIM
    n_bc = (d_xbc - d_ssd) // 2
    w_zx, w_k, w_za = w_main
    assert w_zx.shape == (d, d_ssd + d_xbc) and w_k.shape == (d, d_attn) and w_za.shape == (d, d_attn)
    n_main = w_zx.shape[1] + w_k.shape[1] + w_za.shape[1]
    grid = (bsz, seq // tm)

    row = lambda b, l: (b, l, 0)
    head_major = lambda b, l: (b, 0, l, 0, 0)
    const2 = lambda b, l: (0, 0)
    per_b = lambda b, l: (b, 0, 0)
    single = pl.Buffered(1)
    in_specs = [
        pl.BlockSpec((1, tm, d), row),
        pl.BlockSpec((1, 1, d), per_b),
        pl.BlockSpec((1, 1, d), per_b),
        pl.BlockSpec(w_zx.shape, const2, pipeline_mode=single),
        pl.BlockSpec(w_k.shape, const2, pipeline_mode=single),
        pl.BlockSpec(w_za.shape, const2, pipeline_mode=single),
        pl.BlockSpec((2 * d_attn, d), const2, pipeline_mode=single),
        pl.BlockSpec((V7X_LANES, d), const2),
        pl.BlockSpec((CONV_WIDTH, d_xbc), const2),
        pl.BlockSpec((1, d_xbc), const2),
        pl.BlockSpec((1, V7X_LANES), const2),
    ]
    out_specs = [
        pl.BlockSpec((1, tm, d_ssd), row),
        pl.BlockSpec((1, tm, d_ssd), row),
        pl.BlockSpec((1, tm, n_bc), row),
        pl.BlockSpec((1, tm, n_bc), row),
        pl.BlockSpec((1, nh_attn, 1, ATTN_V_DIM, 2 * tm), head_major),
        pl.BlockSpec((1, tm, d_attn), row),
        pl.BlockSpec((1, nh_attn, 1, ATTN_V_DIM, tm), head_major),
        pl.BlockSpec((1, tm, d_attn), row),
        pl.BlockSpec((1, tm, nh), row),
        pl.BlockSpec((1, nh, tm), lambda b, l: (b, 0, l)),
    ]
    out_shape = [
        jax.ShapeDtypeStruct((bsz, seq, d_ssd), BF16),
        jax.ShapeDtypeStruct((bsz, seq, d_ssd), BF16),
        jax.ShapeDtypeStruct((bsz, seq, n_bc), BF16),
        jax.ShapeDtypeStruct((bsz, seq, n_bc), BF16),
        jax.ShapeDtypeStruct((bsz, nh_attn, seq // tm, ATTN_V_DIM, 2 * tm), BF16),
        jax.ShapeDtypeStruct((bsz, seq, d_attn), BF16),
        jax.ShapeDtypeStruct((bsz, nh_attn, seq // tm, ATTN_V_DIM, tm), BF16),
        jax.ShapeDtypeStruct((bsz, seq, d_attn), BF16),
        jax.ShapeDtypeStruct((bsz, seq, nh), F32),
        jax.ShapeDtypeStruct((bsz, nh, seq), F32),
    ]
    pad = V7X_SUBLANES
    vmem = (d * (n_main + 2 * d_attn) * 2 + 2 * tm * d * 4
            + 2 * tm * (2 * d_ssd + 2 * n_bc + 4 * d_attn) * 2
            + (tm + 2 * pad) * d_xbc * 4 + 6 * tm * d_xbc * 4)
    kern = functools.partial(_inproj_kernel, tm=tm, rows_per_pass=TM_INPROJ_PASS, d_ssd=d_ssd,
                             d_xbc=d_xbc, d_attn=d_attn, n_bc=n_bc)
    assert 2 * nh <= V7X_LANES
    w_dt_pad = jnp.concatenate([w_dt, w_dt, jnp.zeros((V7X_LANES - 2 * nh, d), F32)], axis=0)
    dt_bias_pad = jnp.concatenate([dt_bias, jnp.zeros((V7X_LANES - nh,), F32)]).reshape(1, -1)
    return pl.pallas_call(
        kern, grid=grid, in_specs=in_specs, out_specs=out_specs, out_shape=out_shape,
        scratch_shapes=[pltpu.VMEM((pad, d_xbc), F32)],
        compiler_params=pltpu.CompilerParams(
            dimension_semantics=("arbitrary", "arbitrary"), vmem_limit_bytes=_vmem_limit(vmem)),
        name="in_projection",
    )(x, scale, shift, w_zx, w_k, w_za, w_qv_t, w_dt_pad, conv_w, conv_b.reshape(1, -1), dt_bias_pad)


def _ssd_kernel(xs_ref, bm_ref, cm_ref, zs_ref, dt_ref, dtT_ref, alog_ref, alogT_ref, dskip_ref,
                normw_ref, y_ref, state_ref, ybuf_ref, *, nheads, d_ssd, cps):
    t = CHUNK
    n = SSD_STATE
    hp = SSD_HEAD_DIM
    gw = d_ssd // SSD_GROUPS
    heads_per_group = nheads // SSD_GROUPS

    @pl.when(pl.program_id(1) == 0)
    def _():
        state_ref[...] = jnp.zeros_like(state_ref)

    row = lax.broadcasted_iota(jnp.int32, (t, t), 0)
    col = lax.broadcasted_iota(jnp.int32, (t, t), 1)
    causal = col <= row
    tri = jnp.where(causal, 1.0, 0.0).astype(BF16)
    tri_t = jnp.where(row <= col, 1.0, 0.0).astype(BF16)
    eh = lax.broadcasted_iota(jnp.int32, (nheads, d_ssd), 0)
    ej = lax.broadcasted_iota(jnp.int32, (nheads, d_ssd), 1)
    expand_m = jnp.where(ej // hp == eh, 1.0, 0.0).astype(BF16)
    lane = lax.broadcasted_iota(jnp.int32, (1, 2 * hp), 1)
    neg_inf = jnp.float32(-jnp.inf)
    a = -jnp.exp(alog_ref[...])
    a_t = -jnp.exp(alogT_ref[...])

    def expand(v, pieces):
        return sum(_dot(p, expand_m) for p in _split_bf16(v, pieces))

    for ci in range(cps):
        rows = slice(ci * t, (ci + 1) * t)
        dt = dt_ref[0, rows, :]
        dt_t = dtT_ref[0, :, rows]
        acum = sum(_dot(tri, p) for p in _split_bf16(dt * a, 3))
        acum_t = sum(_dot(p, tri_t) for p in _split_bf16(dt_t * a_t, 3))
        last = acum[t - 1:t, :]
        dt_x = expand(dt, 1)
        w_x = expand(jnp.exp(last - acum), 1)
        e_x = expand(jnp.exp(acum), 2)

        xs = xs_ref[0, rows, :].astype(F32)
        xdt = xs * dt_x
        xdt_b = xdt.astype(BF16)
        xw_b = (xdt * w_x).astype(BF16)

        for g in range(SSD_GROUPS):
            b_g = bm_ref[0, rows, g * n:(g + 1) * n]
            c_g = cm_ref[0, rows, g * n:(g + 1) * n]
            gsl = slice(g * gw, (g + 1) * gw)
            cb = _dot_nt(c_g, b_g)
            prev = state_ref[g]
            y_off = _dot(c_g, prev.astype(BF16)) * e_x[:, gsl]
            state_ref[g] = prev * e_x[t - 1:t, gsl] + _dot_tn(b_g, xw_b[:, gsl])
            for pair in range(heads_per_group // 2):
                lo = g * gw + pair * 2 * hp
                x_pair = xdt_b[:, lo:lo + 2 * hp]
                decays = []
                for half in range(2):
                    hd = g * heads_per_group + pair * 2 + half
                    diff = acum[:, hd:hd + 1] - acum_t[hd:hd + 1, :]
                    decays.append((cb * jnp.exp(jnp.where(causal, diff, neg_inf))).astype(BF16))
                zero = jnp.zeros_like(x_pair)
                x_diag = jnp.concatenate([jnp.where(lane < hp, x_pair, zero),
                                          jnp.where(lane >= hp, x_pair, zero)], axis=0)
                y_pair = (y_off[:, pair * 2 * hp:(pair + 1) * 2 * hp]
                          + _dot(jnp.concatenate(decays, axis=1), x_diag))
                ybuf_ref[rows, lo:lo + 2 * hp] = y_pair

    xs = xs_ref[0].astype(F32)
    y = ybuf_ref[...] + dskip_ref[...] * xs
    y = y * zs_ref[0].astype(F32)
    for g in range(SSD_GROUPS):
        gsl = slice(g * gw, (g + 1) * gw)
        y_g = y[:, gsl]
        ms = jnp.mean(y_g * y_g, axis=-1, keepdims=True)
        y_ref[0, :, gsl] = (y_g * lax.rsqrt(ms + EPS) * normw_ref[:, gsl]).astype(BF16)


def _ssd(xs, bm, cm, zs, dt, dt_t, a_log, d_skip, ssd_norm_w):
    bsz, seq, d_ssd = xs.shape
    nheads = dt.shape[-1]
    n_bc = bm.shape[-1]
    cps = SSD_CHUNKS_PER_STEP
    t = cps * CHUNK
    assert seq % t == 0
    grid = (bsz, seq // t)
    row = lambda b, c: (b, c, 0)
    const2 = lambda b, c: (0, 0)
    in_specs = [
        pl.BlockSpec((1, t, d_ssd), row),
        pl.BlockSpec((1, t, n_bc), row),
        pl.BlockSpec((1, t, n_bc), row),
        pl.BlockSpec((1, t, d_ssd), row),
        pl.BlockSpec((1, t, nheads), row),
        pl.BlockSpec((1, nheads, t), lambda b, c: (b, 0, c)),
        pl.BlockSpec((1, nheads), const2),
        pl.BlockSpec((nheads, 1), const2),
        pl.BlockSpec((1, d_ssd), const2),
        pl.BlockSpec((1, d_ssd), const2),
    ]
    kern = functools.partial(_ssd_kernel, nheads=nheads, d_ssd=d_ssd, cps=cps)
    return pl.pallas_call(
        kern, grid=grid, in_specs=in_specs,
        out_specs=pl.BlockSpec((1, t, d_ssd), row),
        out_shape=jax.ShapeDtypeStruct((bsz, seq, d_ssd), BF16),
        scratch_shapes=[pltpu.VMEM((SSD_GROUPS, SSD_STATE, d_ssd // SSD_GROUPS), F32),
                        pltpu.VMEM((t, d_ssd), F32)],
        compiler_params=pltpu.CompilerParams(dimension_semantics=("arbitrary", "arbitrary")),
        name="ssd_scan",
    )(xs, bm, cm, zs, dt, dt_t, a_log.reshape(1, -1), a_log.reshape(-1, 1),
      jnp.repeat(d_skip, SSD_HEAD_DIM).reshape(1, -1), ssd_norm_w.reshape(1, -1))


def _attn_kernel(qT_ref, k_ref, vT_ref, za_ref, lq1_ref, lk1_ref, lq2_ref, lk2_ref, normw_ref,
                 o_ref, s_ref, cmax_ref, m_ref, acc_ref, *, tk, nsub, nq):
    step = pl.program_id(2)
    dk = ATTN_QK_DIM
    dv = 2 * dk
    sw = 2 * tk
    nchunk = 2 * nsub
    tq = nsub * tk
    half = tk // 2
    lead = ATTN_PRODUCE_LEAD
    per_trip = ATTN_BLOCKS_PER_TRIP
    assert lead <= nchunk and per_trip % 2 == 0 and nsub % 2 == 0 and (nsub * nq) % per_trip == 0
    ones_rows = jnp.ones((BF16_SUBLANE_PACK, tk), BF16)
    lam = (jnp.exp(jnp.sum(lq1_ref[...] * lk1_ref[...], axis=-1, keepdims=True))
           - jnp.exp(jnp.sum(lq2_ref[...] * lk2_ref[...], axis=-1, keepdims=True))
           + LAMBDA_INIT_L0)

    def col(c):
        return slice(c * tk, (c + 1) * tk)

    def query_block(g):
        sc = g % ATTN_SCRATCH_SETS

        def q_columns(c):
            return qT_ref[0, 0, g * nsub + c // 2, :, (c % 2) * tk:(c % 2 + 1) * tk]

        def produce(slot, kv, c):
            start = pl.multiple_of(kv * tk, tk)
            s = _dot(k_ref[0, pl.ds(start, tk), :], q_columns(c))
            s_ref[sc, slot, :, col(c)] = s
            cmax_ref[sc, slot, :, col(c)] = jnp.max(s, axis=0, keepdims=True)

        def produce_diagonal(slot, kv, c):
            start = pl.multiple_of(kv * tk, tk)
            lo = c * tk
            s_ref[sc, slot, 0:half, lo:lo + tk] = _dot(k_ref[0, pl.ds(start, half), :], q_columns(c))
            s_ref[sc, slot, half:tk, lo + half:lo + tk] = _dot(
                k_ref[0, pl.ds(start + half, half), :], q_columns(c)[:, half:tk])

        def update(s, cmax, kv, cols, nrows):
            m_prev = m_ref[sc, :, cols]
            m_new = jnp.maximum(m_prev, cmax)
            alpha = jnp.exp2(m_prev - m_new)
            p = jnp.exp2(s - m_new).astype(BF16)
            v_ext = jnp.concatenate([vT_ref[0, 0, kv, :, 0:nrows], ones_rows[:, 0:nrows]], axis=0)
            acc_ref[sc, :, cols] = alpha * acc_ref[sc, :, cols] + _dot(v_ext, p)
            m_ref[sc, :, cols] = m_new

        def consume(slot, kv, c, diagonal=False):
            if not diagonal:
                update(s_ref[sc, slot, :, col(c)], cmax_ref[sc, slot, :, col(c)], kv, col(c), tk)
                return
            lo = c * tk
            for cols, nrows in ((slice(lo, lo + half), half), (slice(lo + half, lo + tk), tk)):
                s = s_ref[sc, slot, 0:nrows, cols]
                r = lax.broadcasted_iota(jnp.int32, s.shape, 0)
                qpos = lax.broadcasted_iota(jnp.int32, s.shape, 1) + (cols.start - lo)
                s = jnp.where(r <= qpos, s, -jnp.inf)
                update(s, jnp.max(s, axis=0, keepdims=True), kv, cols, nrows)

        def finalize(sub):
            c1 = sub * sw
            c2 = c1 + tk
            o1 = acc_ref[sc, 0:dv, c1:c1 + tk] / acc_ref[sc, dv:dv + 1, c1:c1 + tk]
            o2 = acc_ref[sc, 0:dv, c2:c2 + tk] / acc_ref[sc, dv:dv + 1, c2:c2 + tk]
            o = (o1 - lam * o2).T
            ms = jnp.mean(o * o, axis=-1, keepdims=True)
            o = o * lax.rsqrt(ms + EPS) * normw_ref[...] * (1.0 - LAMBDA_INIT_L0)
            rows = slice(g * tq + sub * tk, g * tq + (sub + 1) * tk)
            o_ref[0, rows, :] = (o * za_ref[0, rows, :].astype(F32)).astype(BF16)

        m_ref[sc] = jnp.full(m_ref.shape[1:], -jnp.inf, F32)
        acc_ref[sc] = jnp.zeros(acc_ref.shape[1:], F32)

        for c in range(lead):
            produce(0, 0, c)

        def off_diagonal(kv, nblocks):
            for idx in range(nblocks * nchunk):
                ahead = idx + lead
                produce((ahead // nchunk) % 2, kv + ahead // nchunk, ahead % nchunk)
                consume((idx // nchunk) % 2, kv + idx // nchunk, idx % nchunk)

        trips = (nsub * nq // per_trip) * step + (nsub * g) // per_trip
        left = (nsub * g) % per_trip

        def trip(i, carry):
            off_diagonal(i * per_trip, per_trip)
            return carry

        lax.fori_loop(0, trips, trip, 0)
        if left:
            off_diagonal(trips * per_trip, left)

        kv0 = trips * per_trip + left
        units = [(d, c) for d in range(nsub) for c in range(nchunk) if c // 2 >= d]
        assert units[:lead] == [(0, c) for c in range(lead)]
        for i, (d, c) in enumerate(units):
            if i + lead < len(units):
                nd, nc = units[i + lead]
                (produce_diagonal if nc // 2 == nd else produce)(nd % 2, kv0 + nd, nc)
            consume(d % 2, kv0 + d, c, diagonal=(c // 2 == d))
            if c % 2 == 1 and c // 2 == d:
                finalize(d)

    for g in range(nq):
        query_block(g)


def _attention(q_t, k, v_t, za, lq1, lk1, lq2, lk2, attn_norm_w):
    bsz, seq, d_attn = k.shape
    hw = ATTN_V_DIM
    nh = d_attn // hw
    tk = TK_ATTN
    nsub = NSUB_ATTN
    nq = NQ_ATTN
    tq = nsub * tk
    rows = nq * tq
    nblk = seq // tk
    assert nsub % 2 == 0 and seq % rows == 0
    assert q_t.shape == (bsz, nh, nblk, hw, 2 * tk) and v_t.shape == (bsz, nh, nblk, hw, tk)
    grid = (bsz, nh, seq // rows)
    qrow = lambda b, h, i: (b, i, h)
    const2 = lambda b, h, i: (0, 0)
    vec = pl.BlockSpec((1, ATTN_QK_DIM), const2)
    in_specs = [
        pl.BlockSpec((1, 1, nq * nsub, hw, 2 * tk), lambda b, h, i: (b, h, i, 0, 0)),
        pl.BlockSpec((1, seq, hw), lambda b, h, i: (b, 0, h)),
        pl.BlockSpec((1, 1, nblk, hw, tk), lambda b, h, i: (b, h, 0, 0, 0)),
        pl.BlockSpec((1, rows, hw), qrow),
        vec, vec, vec, vec,
        pl.BlockSpec((1, hw), const2),
    ]
    width = 2 * tq
    nscr = ATTN_SCRATCH_SETS
    assert nscr >= min(nq, 2)
    vmem = (2 * 2 * seq * hw * 2 + nscr * 2 * tk * width * 4
            + nscr * (2 * hw + BF16_SUBLANE_PACK) * width * 4 + 4 * tk * width * 4)
    return pl.pallas_call(
        functools.partial(_attn_kernel, tk=tk, nsub=nsub, nq=nq), grid=grid, in_specs=in_specs,
        out_specs=pl.BlockSpec((1, rows, hw), qrow),
        out_shape=jax.ShapeDtypeStruct((bsz, seq, d_attn), BF16),
        scratch_shapes=[pltpu.VMEM((nscr, 2, tk, width), F32),
                        pltpu.VMEM((nscr, 2, 1, width), F32), pltpu.VMEM((nscr, 1, width), F32),
                        pltpu.VMEM((nscr, hw + BF16_SUBLANE_PACK, width), F32)],
        compiler_params=pltpu.CompilerParams(
            dimension_semantics=("arbitrary", "arbitrary", "arbitrary"),
            vmem_limit_bytes=_vmem_limit(vmem)),
        name="diff_attention",
    )(q_t, k, v_t, za, lq1.reshape(1, -1), lk1.reshape(1, -1), lq2.reshape(1, -1),
      lk2.reshape(1, -1), attn_norm_w.reshape(1, -1))


def _outproj_kernel(ys_ref, ya_ref, w_ref, x_ref, gate_ref, g_ref, b_ref, o_ref, *, d_ssd, alpha,
                    tm, rows_per_pass):
    for r0 in range(0, tm, rows_per_pass):
        rows = slice(r0, r0 + rows_per_pass)
        mixed = (_dot(ys_ref[0, rows, :], w_ref[0:d_ssd, :])
                 + _dot(ya_ref[0, rows, :], w_ref[d_ssd:, :]))
        xf = alpha * x_ref[0, rows, :] + gate_ref[0] * mixed
        mu = jnp.mean(xf, axis=-1, keepdims=True)
        xc = xf - mu
        var = jnp.mean(xc * xc, axis=-1, keepdims=True)
        o_ref[0, rows, :] = xc * lax.rsqrt(var + EPS) * g_ref[...] + b_ref[...]


def _out_projection(y_ssd, y_attn, w_out, x, gate, ln_g, ln_b, alpha):
    bsz, seq, d = x.shape
    d_ssd = y_ssd.shape[-1]
    d_attn = y_attn.shape[-1]
    tm = TM_OUT
    row = lambda b, l: (b, l, 0)
    const2 = lambda b, l: (0, 0)
    in_specs = [
        pl.BlockSpec((1, tm, d_ssd), row),
        pl.BlockSpec((1, tm, d_attn), row),
        pl.BlockSpec((d_ssd + d_attn, d), const2, pipeline_mode=pl.Buffered(1)),
        pl.BlockSpec((1, tm, d), row),
        pl.BlockSpec((1, 1, d), lambda b, l: (b, 0, 0)),
        pl.BlockSpec((1, d), const2),
        pl.BlockSpec((1, d), const2),
    ]
    vmem = (d_ssd + d_attn) * d * 2 + 2 * tm * (d_ssd + d_attn) * 2 + 4 * tm * d * 4 + 4 * tm * d * 4
    return pl.pallas_call(
        functools.partial(_outproj_kernel, d_ssd=d_ssd, alpha=alpha, tm=tm,
                          rows_per_pass=TM_OUT_PASS),
        grid=(bsz, seq // tm), in_specs=in_specs,
        out_specs=pl.BlockSpec((1, tm, d), row),
        out_shape=jax.ShapeDtypeStruct((bsz, seq, d), F32),
        compiler_params=pltpu.CompilerParams(
            dimension_semantics=("arbitrary", "arbitrary"), vmem_limit_bytes=_vmem_limit(vmem)),
        name="out_projection",
    )(y_ssd, y_attn, w_out, x, gate, ln_g.reshape(1, -1), ln_b.reshape(1, -1))


def _layer(x, c, w_ada, b_ada, w_in, conv_w, conv_b, dt_bias, a_log, d_skip, ssd_norm_w,
           lambda_q1, lambda_k1, lambda_q2, lambda_k2, attn_norm_w, w_out, ln_g, ln_b, depth):
    bsz, seq, d = x.shape
    nheads = a_log.shape[0]
    d_ssd = nheads * SSD_HEAD_DIM
    d_xbc = conv_w.shape[1]
    d_attn = w_out.shape[0] - d_ssd
    assert d_xbc == d_ssd + 2 * SSD_GROUPS * SSD_STATE
    assert w_in.shape[1] == d_ssd + d_xbc + nheads + 4 * d_attn
    assert seq % TM_INPROJ == 0 and seq % TM_OUT == 0 and seq % CHUNK == 0
    assert TM_INPROJ == TK_ATTN, "the in-projection writes q^T / v^T in attention-sized time blocks"

    mod = _modulation(c, w_ada, b_ada)
    shift = mod[:, None, 0:d]
    scale = mod[:, None, d:2 * d]
    gate = mod[:, None, 2 * d:3 * d]

    off_dt = d_ssd + d_xbc
    off_q = off_dt + nheads
    off_k, off_v, off_za = off_q + d_attn, off_q + 2 * d_attn, off_q + 3 * d_attn
    w_t = w_in.T
    w_main = tuple(w_t[lo:hi].astype(BF16).T
                   for lo, hi in ((0, off_dt), (off_k, off_v), (off_za, off_za + d_attn)))
    w_qv_t = jnp.concatenate([w_t[off_q:off_k], w_t[off_v:off_za]], axis=0).astype(BF16)
    w_dt = w_t[off_dt:off_q]

    zs, xs, bm, cm, q_t, k, v_t, za, dt, dt_t = _in_projection(
        x, scale, shift, w_main, w_qv_t, w_dt, conv_w, conv_b, dt_bias,
        d_ssd=d_ssd, d_xbc=d_xbc, d_attn=d_attn)
    y_ssd = _ssd(xs, bm, cm, zs, dt, dt_t, a_log, d_skip, ssd_norm_w)
    y_attn = _attention(q_t, k, v_t, za, lambda_q1, lambda_k1, lambda_q2, lambda_k2, attn_norm_w)
    alpha = (2.0 * depth) ** 0.25
    return _out_projection(y_ssd, y_attn, w_out.astype(BF16), x, gate, ln_g, ln_b, alpha)


def kernel(x, c, w_ada, b_ada, w_in, conv_w, conv_b, dt_bias, a_log, d_skip, ssd_norm_w, lambda_q1,
           lambda_k1, lambda_q2, lambda_k2, attn_norm_w, w_out, ln_g, ln_b):
    depth = w_in.shape[0]
    assert depth == 1, "lambda_init and the layer loop are specialised to a single layer"
    return _layer(x, c, w_ada[0], b_ada[0], w_in[0], conv_w[0], conv_b[0], dt_bias[0], a_log[0],
                  d_skip[0], ssd_norm_w[0], lambda_q1[0], lambda_k1[0], lambda_q2[0], lambda_k2[0],
                  attn_norm_w[0], w_out[0], ln_g[0], ln_b[0], depth)
```

```python
import functools
import math

import jax
import jax.numpy as jnp
from jax import lax
from jax.experimental import pallas as pl
from jax.experimental.pallas import tpu as pltpu

F32 = jnp.float32
BF16 = jnp.bfloat16

SSD_HEAD_DIM = 64
SSD_GROUPS = 2
SSD_STATE = 128
CONV_WIDTH = 4
CHUNK = 128
ATTN_QK_DIM = 64
ATTN_V_DIM = 2 * ATTN_QK_DIM
EPS = 1e-5
LAMBDA_INIT_L0 = 0.8 - 0.6 * math.exp(-0.3 * 0)
LOG2_E = math.log2(math.e)

V7X_LANES = 128
V7X_SUBLANES = 8
BF16_SUBLANE_PACK = 16
V7X_VMEM_BYTES = 64 * 1024 * 1024
V7X_VMEM_DEFAULT_SCOPED_BYTES = V7X_VMEM_BYTES // 2
V7X_VMEM_REQUEST_CAP_BYTES = V7X_VMEM_BYTES * 7 // 8

TM_INPROJ = 512
TM_INPROJ_PASS = 256
TK_ATTN = 512
NSUB_ATTN = 2
NQ_ATTN = 4
ATTN_SCRATCH_SETS = 2
ATTN_PRODUCE_LEAD = 2
ATTN_BLOCKS_PER_TRIP = 4
TM_OUT = 1024
TM_OUT_PASS = 256
SSD_CHUNKS_PER_STEP = 8


def _vmem_limit(estimate_bytes):
    return int(min(V7X_VMEM_REQUEST_CAP_BYTES, max(V7X_VMEM_DEFAULT_SCOPED_BYTES, estimate_bytes)))


def _silu(x):
    hx = 0.5 * x
    return hx + hx * jnp.tanh(hx)


def _softplus(x):
    return jnp.maximum(x, 0.0) + jnp.log(1.0 + jnp.exp(-jnp.abs(x)))


def _split_bf16(a, pieces):
    out = []
    r = a
    for i in range(pieces):
        p = r.astype(BF16)
        out.append(p)
        if i + 1 < pieces:
            r = r - p.astype(F32)
    return out


def _dot(a, b):
    return jnp.dot(a, b, preferred_element_type=F32)


def _dot_nt(a, b):
    return lax.dot_general(a, b, (((1,), (1,)), ((), ())), preferred_element_type=F32)


def _dot_tn(a, b):
    return lax.dot_general(a, b, (((0,), (0,)), ((), ())), preferred_element_type=F32)


def _mod_kernel(c_ref, w_ref, b_ref, o_ref):
    c_hi, c_lo = _split_bf16(c_ref[...], 2)
    w_hi, w_lo = _split_bf16(w_ref[...], 2)
    o_ref[...] = _dot(c_hi, w_hi) + _dot(c_hi, w_lo) + _dot(c_lo, w_hi) + b_ref[...]


def _modulation(c, w_ada, b_ada):
    bsz, d = c.shape
    n = w_ada.shape[1]
    rows = V7X_SUBLANES
    c_pad = jnp.zeros((rows, d), F32).at[:bsz].set(c)
    bn = d
    out = pl.pallas_call(
        _mod_kernel,
        grid=(n // bn,),
        in_specs=[pl.BlockSpec((rows, d), lambda j: (0, 0)),
                  pl.BlockSpec((d, bn), lambda j: (0, j)),
                  pl.BlockSpec((1, bn), lambda j: (0, j))],
        out_specs=pl.BlockSpec((rows, bn), lambda j: (0, j)),
        out_shape=jax.ShapeDtypeStruct((rows, n), F32),
        name="modulation",
    )(c_pad, w_ada, b_ada.reshape(1, n))
    return out[:bsz]


def _inproj_kernel(x_ref, scale_ref, shift_ref, wzx_ref, wk_ref, wza_ref, wqvT_ref, wdt_ref,
                   convw_ref, convb_ref, dtb_ref,
                   zs_ref, xs_ref, bm_ref, cm_ref, qT_ref, k_ref, vT_ref, za_ref, dt_ref, dtT_ref,
                   carry_ref, *, tm, rows_per_pass, d_ssd, d_xbc, d_attn, n_bc):
    @pl.when(pl.program_id(1) == 0)
    def _():
        carry_ref[...] = jnp.zeros_like(carry_ref)

    pad = V7X_SUBLANES
    nh = dt_ref.shape[-1]
    nh_attn = d_attn // ATTN_V_DIM
    cw = convw_ref[...]
    w_hi, w_lo = _split_bf16(wdt_ref[...], 2)
    wrow = lax.broadcasted_iota(jnp.int32, (V7X_LANES, 1), 0)
    w_hilo = jnp.where(wrow < nh, w_hi, w_lo)

    tail = carry_ref[...]
    for r0 in range(0, tm, rows_per_pass):
        rows = slice(r0, r0 + rows_per_pass)
        h32 = x_ref[0, rows, :] * (1.0 + scale_ref[0]) + shift_ref[0]
        h = h32.astype(BF16)

        u = _dot(h, wzx_ref[:, d_ssd:d_ssd + d_xbc])
        u_ext = jnp.concatenate([tail, u], axis=0)
        tail = u[rows_per_pass - pad:rows_per_pass, :]
        acc = convb_ref[...] + cw[CONV_WIDTH - 1:CONV_WIDTH, :] * u
        for tap in range(CONV_WIDTH - 1):
            shift = CONV_WIDTH - 1 - tap
            acc = acc + (cw[tap:tap + 1, :]
                         * pltpu.roll(u_ext, shift, 0)[pad:pad + rows_per_pass, :])
        xbc = _silu(acc)
        xs_ref[0, rows, :] = xbc[:, :d_ssd].astype(BF16)
        bm_ref[0, rows, :] = xbc[:, d_ssd:d_ssd + n_bc].astype(BF16)
        cm_ref[0, rows, :] = xbc[:, d_ssd + n_bc:d_ssd + 2 * n_bc].astype(BF16)

        zs_ref[0, rows, :] = _silu(_dot(h, wzx_ref[:, 0:d_ssd])).astype(BF16)
        k_ref[0, rows, :] = _dot(h, wk_ref[...]).astype(BF16)
        za_ref[0, rows, :] = _silu(_dot(h, wza_ref[...])).astype(BF16)

        qvT = _dot_nt(wqvT_ref[...], h)
        qT = qvT[:d_attn] * (ATTN_QK_DIM ** -0.5 * LOG2_E)
        qT = qT.astype(BF16).reshape(nh_attn, ATTN_V_DIM, rows_per_pass)
        feat = lax.broadcasted_iota(jnp.int32, (1, ATTN_V_DIM, 1), 1)
        zero = jnp.zeros_like(qT)
        qT_ref[0, :, 0, :, rows] = jnp.where(feat < ATTN_QK_DIM, qT, zero)
        qT_ref[0, :, 0, :, tm + r0:tm + r0 + rows_per_pass] = jnp.where(feat >= ATTN_QK_DIM, qT, zero)
        vT_ref[0, :, 0, :, rows] = qvT[d_attn:].astype(BF16).reshape(nh_attn, ATTN_V_DIM,
                                                                      rows_per_pass)

        h_lo = (h32 - h.astype(F32)).astype(BF16)
        r1 = _dot_nt(h, w_hilo)
        r2 = _dot_nt(h_lo, w_hi)
        dt_raw = r1 + pltpu.roll(r1, V7X_LANES - nh, 1) + r2
        dt = _softplus(dt_raw + dtb_ref[...])
        dt_ref[0, rows, :] = dt[:, :nh]
        dtT_ref[0, :, rows] = dt.T[:nh, :]
    carry_ref[...] = tail


def _in_projection(x, scale, shift, w_main, w_qv_t, w_dt, conv_w, conv_b, dt_bias, *, d_ssd, d_xbc,
                   d_attn):
    bsz, seq, d = x.shape
    tm = TM_INPROJ
    nh = w_dt.shape[0]
    nh_attn = d_attn // ATTN_V_DIM
    n_bc = (d_xbc - d_ssd) // 2
    w_zx, w_k, w_za = w_main
    assert w_zx.shape == (d, d_ssd + d_xbc) and w_k.shape == (d, d_attn) and w_za.shape == (d, d_attn)
    n_main = w_zx.shape[1] + w_k.shape[1] + w_za.shape[1]
    grid = (bsz, seq // tm)

    row = lambda b, l: (b, l, 0)
    head_major = lambda b, l: (b, 0, l, 0, 0)
    const2 = lambda b, l: (0, 0)
    per_b = lambda b, l: (b, 0, 0)
    single = pl.Buffered(1)
    in_specs = [
        pl.BlockSpec((1, tm, d), row),
        pl.BlockSpec((1, 1, d), per_b),
        pl.BlockSpec((1, 1, d), per_b),
        pl.BlockSpec(w_zx.shape, const2, pipeline_mode=single),
        pl.BlockSpec(w_k.shape, const2, pipeline_mode=single),
        pl.BlockSpec(w_za.shape, const2, pipeline_mode=single),
        pl.BlockSpec((2 * d_attn, d), const2, pipeline_mode=single),
        pl.BlockSpec((V7X_LANES, d), const2),
        pl.BlockSpec((CONV_WIDTH, d_xbc), const2),
        pl.BlockSpec((1, d_xbc), const2),
        pl.BlockSpec((1, V7X_LANES), const2),
    ]
    out_specs = [
        pl.BlockSpec((1, tm, d_ssd), row),
        pl.BlockSpec((1, tm, d_ssd), row),
        pl.BlockSpec((1, tm, n_bc), row),
        pl.BlockSpec((1, tm, n_bc), row),
        pl.BlockSpec((1, nh_attn, 1, ATTN_V_DIM, 2 * tm), head_major),
        pl.BlockSpec((1, tm, d_attn), row),
        pl.BlockSpec((1, nh_attn, 1, ATTN_V_DIM, tm), head_major),
        pl.BlockSpec((1, tm, d_attn), row),
        pl.BlockSpec((1, tm, nh), row),
        pl.BlockSpec((1, nh, tm), lambda b, l: (b, 0, l)),
    ]
    out_shape = [
        jax.ShapeDtypeStruct((bsz, seq, d_ssd), BF16),
        jax.ShapeDtypeStruct((bsz, seq, d_ssd), BF16),
        jax.ShapeDtypeStruct((bsz, seq, n_bc), BF16),
        jax.ShapeDtypeStruct((bsz, seq, n_bc), BF16),
        jax.ShapeDtypeStruct((bsz, nh_attn, seq // tm, ATTN_V_DIM, 2 * tm), BF16),
        jax.ShapeDtypeStruct((bsz, seq, d_attn), BF16),
        jax.ShapeDtypeStruct((bsz, nh_attn, seq // tm, ATTN_V_DIM, tm), BF16),
        jax.ShapeDtypeStruct((bsz, seq, d_attn), BF16),
        jax.ShapeDtypeStruct((bsz, seq, nh), F32),
        jax.ShapeDtypeStruct((bsz, nh, seq), F32),
    ]
    pad = V7X_SUBLANES
    vmem = (d * (n_main + 2 * d_attn) * 2 + 2 * tm * d * 4
            + 2 * tm * (2 * d_ssd + 2 * n_bc + 4 * d_attn) * 2
            + (tm + 2 * pad) * d_xbc * 4 + 6 * tm * d_xbc * 4)
    kern = functools.partial(_inproj_kernel, tm=tm, rows_per_pass=TM_INPROJ_PASS, d_ssd=d_ssd,
                             d_xbc=d_xbc, d_attn=d_attn, n_bc=n_bc)
    assert 2 * nh <= V7X_LANES
    w_dt_pad = jnp.concatenate([w_dt, w_dt, jnp.zeros((V7X_LANES - 2 * nh, d), F32)], axis=0)
    dt_bias_pad = jnp.concatenate([dt_bias, jnp.zeros((V7X_LANES - nh,), F32)]).reshape(1, -1)
    return pl.pallas_call(
        kern, grid=grid, in_specs=in_specs, out_specs=out_specs, out_shape=out_shape,
        scratch_shapes=[pltpu.VMEM((pad, d_xbc), F32)],
        compiler_params=pltpu.CompilerParams(
            dimension_semantics=("arbitrary", "arbitrary"), vmem_limit_bytes=_vmem_limit(vmem)),
        name="in_projection",
    )(x, scale, shift, w_zx, w_k, w_za, w_qv_t, w_dt_pad, conv_w, conv_b.reshape(1, -1), dt_bias_pad)


def _ssd_kernel(xs_ref, bm_ref, cm_ref, zs_ref, dt_ref, dtT_ref, alog_ref, alogT_ref, dskip_ref,
                normw_ref, y_ref, state_ref, ybuf_ref, *, nheads, d_ssd, cps):
    t = CHUNK
    n = SSD_STATE
    hp = SSD_HEAD_DIM
    gw = d_ssd // SSD_GROUPS
    heads_per_group = nheads // SSD_GROUPS

    @pl.when(pl.program_id(1) == 0)
    def _():
        state_ref[...] = jnp.zeros_like(state_ref)

    row = lax.broadcasted_iota(jnp.int32, (t, t), 0)
    col = lax.broadcasted_iota(jnp.int32, (t, t), 1)
    causal = col <= row
    tri = jnp.where(causal, 1.0, 0.0).astype(BF16)
    tri_t = jnp.where(row <= col, 1.0, 0.0).astype(BF16)
    eh = lax.broadcasted_iota(jnp.int32, (nheads, d_ssd), 0)
    ej = lax.broadcasted_iota(jnp.int32, (nheads, d_ssd), 1)
    expand_m = jnp.where(ej // hp == eh, 1.0, 0.0).astype(BF16)
    lane = lax.broadcasted_iota(jnp.int32, (1, 2 * hp), 1)
    neg_inf = jnp.float32(-jnp.inf)
    a = -jnp.exp(alog_ref[...])
    a_t = -jnp.exp(alogT_ref[...])

    def expand(v, pieces):
        return sum(_dot(p, expand_m) for p in _split_bf16(v, pieces))

    for ci in range(cps):
        rows = slice(ci * t, (ci + 1) * t)
        dt = dt_ref[0, rows, :]
        dt_t = dtT_ref[0, :, rows]
        acum = sum(_dot(tri, p) for p in _split_bf16(dt * a, 3))
        acum_t = sum(_dot(p, tri_t) for p in _split_bf16(dt_t * a_t, 3))
        last = acum[t - 1:t, :]
        dt_x = expand(dt, 1)
        w_x = expand(jnp.exp(last - acum), 1)
        e_x = expand(jnp.exp(acum), 2)

        xs = xs_ref[0, rows, :].astype(F32)
        xdt = xs * dt_x
        xdt_b = xdt.astype(BF16)
        xw_b = (xdt * w_x).astype(BF16)

        for g in range(SSD_GROUPS):
            b_g = bm_ref[0, rows, g * n:(g + 1) * n]
            c_g = cm_ref[0, rows, g * n:(g + 1) * n]
            gsl = slice(g * gw, (g + 1) * gw)
            cb = _dot_nt(c_g, b_g)
            prev = state_ref[g]
            y_off = _dot(c_g, prev.astype(BF16)) * e_x[:, gsl]
            state_ref[g] = prev * e_x[t - 1:t, gsl] + _dot_tn(b_g, xw_b[:, gsl])
            for pair in range(heads_per_group // 2):
                lo = g * gw + pair * 2 * hp
                x_pair = xdt_b[:, lo:lo + 2 * hp]
                decays = []
                for half in range(2):
                    hd = g * heads_per_group + pair * 2 + half
                    diff = acum[:, hd:hd + 1] - acum_t[hd:hd + 1, :]
                    decays.append((cb * jnp.exp(jnp.where(causal, diff, neg_inf))).astype(BF16))
                zero = jnp.zeros_like(x_pair)
                x_diag = jnp.concatenate([jnp.where(lane < hp, x_pair, zero),
                                          jnp.where(lane >= hp, x_pair, zero)], axis=0)
                y_pair = (y_off[:, pair * 2 * hp:(pair + 1) * 2 * hp]
                          + _dot(jnp.concatenate(decays, axis=1), x_diag))
                ybuf_ref[rows, lo:lo + 2 * hp] = y_pair

    xs = xs_ref[0].astype(F32)
    y = ybuf_ref[...] + dskip_ref[...] * xs
    y = y * zs_ref[0].astype(F32)
    for g in range(SSD_GROUPS):
        gsl = slice(g * gw, (g + 1) * gw)
        y_g = y[:, gsl]
        ms = jnp.mean(y_g * y_g, axis=-1, keepdims=True)
        y_ref[0, :, gsl] = (y_g * lax.rsqrt(ms + EPS) * normw_ref[:, gsl]).astype(BF16)


def _ssd(xs, bm, cm, zs, dt, dt_t, a_log, d_skip, ssd_norm_w):
    bsz, seq, d_ssd = xs.shape
    nheads = dt.shape[-1]
    n_bc = bm.shape[-1]
    cps = SSD_CHUNKS_PER_STEP
    t = cps * CHUNK
    assert seq % t == 0
    grid = (bsz, seq // t)
    row = lambda b, c: (b, c, 0)
    const2 = lambda b, c: (0, 0)
    in_specs = [
        pl.BlockSpec((1, t, d_ssd), row),
        pl.BlockSpec((1, t, n_bc), row),
        pl.BlockSpec((1, t, n_bc), row),
        pl.BlockSpec((1, t, d_ssd), row),
        pl.BlockSpec((1, t, nheads), row),
        pl.BlockSpec((1, nheads, t), lambda b, c: (b, 0, c)),
        pl.BlockSpec((1, nheads), const2),
        pl.BlockSpec((nheads, 1), const2),
        pl.BlockSpec((1, d_ssd), const2),
        pl.BlockSpec((1, d_ssd), const2),
    ]
    kern = functools.partial(_ssd_kernel, nheads=nheads, d_ssd=d_ssd, cps=cps)
    return pl.pallas_call(
        kern, grid=grid, in_specs=in_specs,
        out_specs=pl.BlockSpec((1, t, d_ssd), row),
        out_shape=jax.ShapeDtypeStruct((bsz, seq, d_ssd), BF16),
        scratch_shapes=[pltpu.VMEM((SSD_GROUPS, SSD_STATE, d_ssd // SSD_GROUPS), F32),
                        pltpu.VMEM((t, d_ssd), F32)],
        compiler_params=pltpu.CompilerParams(dimension_semantics=("arbitrary", "arbitrary")),
        name="ssd_scan",
    )(xs, bm, cm, zs, dt, dt_t, a_log.reshape(1, -1), a_log.reshape(-1, 1),
      jnp.repeat(d_skip, SSD_HEAD_DIM).reshape(1, -1), ssd_norm_w.reshape(1, -1))


def _attn_kernel(qT_ref, k_ref, vT_ref, za_ref, lq1_ref, lk1_ref, lq2_ref, lk2_ref, normw_ref,
                 o_ref, s_ref, cmax_ref, m_ref, acc_ref, *, tk, nsub, nq):
    step = pl.program_id(2)
    dk = ATTN_QK_DIM
    dv = 2 * dk
    sw = 2 * tk
    nchunk = 2 * nsub
    tq = nsub * tk
    half = tk // 2
    lead = ATTN_PRODUCE_LEAD
    per_trip = ATTN_BLOCKS_PER_TRIP
    assert lead <= nchunk and per_trip % 2 == 0 and nsub % 2 == 0 and (nsub * nq) % per_trip == 0
    ones_rows = jnp.ones((BF16_SUBLANE_PACK, tk), BF16)
    lam = (jnp.exp(jnp.sum(lq1_ref[...] * lk1_ref[...], axis=-1, keepdims=True))
           - jnp.exp(jnp.sum(lq2_ref[...] * lk2_ref[...], axis=-1, keepdims=True))
           + LAMBDA_INIT_L0)

    def col(c):
        return slice(c * tk, (c + 1) * tk)

    def query_block(g):
        sc = g % ATTN_SCRATCH_SETS

        def q_columns(c):
            return qT_ref[0, 0, g * nsub + c // 2, :, (c % 2) * tk:(c % 2 + 1) * tk]

        def produce(slot, kv, c):
            start = pl.multiple_of(kv * tk, tk)
            s = _dot(k_ref[0, pl.ds(start, tk), :], q_columns(c))
            s_ref[sc, slot, :, col(c)] = s
            cmax_ref[sc, slot, :, col(c)] = jnp.max(s, axis=0, keepdims=True)

        def produce_diagonal(slot, kv, c):
            start = pl.multiple_of(kv * tk, tk)
            lo = c * tk
            s_ref[sc, slot, 0:half, lo:lo + tk] = _dot(k_ref[0, pl.ds(start, half), :], q_columns(c))
            s_ref[sc, slot, half:tk, lo + half:lo + tk] = _dot(
                k_ref[0, pl.ds(start + half, half), :], q_columns(c)[:, half:tk])

        def update(s, cmax, kv, cols, nrows):
            m_prev = m_ref[sc, :, cols]
            m_new = jnp.maximum(m_prev, cmax)
            alpha = jnp.exp2(m_prev - m_new)
            p = jnp.exp2(s - m_new).astype(BF16)
            v_ext = jnp.concatenate([vT_ref[0, 0, kv, :, 0:nrows], ones_rows[:, 0:nrows]], axis=0)
            acc_ref[sc, :, cols] = alpha * acc_ref[sc, :, cols] + _dot(v_ext, p)
            m_ref[sc, :, cols] = m_new

        def consume(slot, kv, c, diagonal=False):
            if not diagonal:
                update(s_ref[sc, slot, :, col(c)], cmax_ref[sc, slot, :, col(c)], kv, col(c), tk)
                return
            lo = c * tk
            for cols, nrows in ((slice(lo, lo + half), half), (slice(lo + half, lo + tk), tk)):
                s = s_ref[sc, slot, 0:nrows, cols]
                r = lax.broadcasted_iota(jnp.int32, s.shape, 0)
                qpos = lax.broadcasted_iota(jnp.int32, s.shape, 1) + (cols.start - lo)
                s = jnp.where(r <= qpos, s, -jnp.inf)
                update(s, jnp.max(s, axis=0, keepdims=True), kv, cols, nrows)

        def finalize(sub):
            c1 = sub * sw
            c2 = c1 + tk
            o1 = acc_ref[sc, 0:dv, c1:c1 + tk] / acc_ref[sc, dv:dv + 1, c1:c1 + tk]
            o2 = acc_ref[sc, 0:dv, c2:c2 + tk] / acc_ref[sc, dv:dv + 1, c2:c2 + tk]
            o = (o1 - lam * o2).T
            ms = jnp.mean(o * o, axis=-1, keepdims=True)
            o = o * lax.rsqrt(ms + EPS) * normw_ref[...] * (1.0 - LAMBDA_INIT_L0)
            rows = slice(g * tq + sub * tk, g * tq + (sub + 1) * tk)
            o_ref[0, rows, :] = (o * za_ref[0, rows, :].astype(F32)).astype(BF16)

        m_ref[sc] = jnp.full(m_ref.shape[1:], -jnp.inf, F32)
        acc_ref[sc] = jnp.zeros(acc_ref.shape[1:], F32)

        for c in range(lead):
            produce(0, 0, c)

        def off_diagonal(kv, nblocks):
            for idx in range(nblocks * nchunk):
                ahead = idx + lead
                produce((ahead // nchunk) % 2, kv + ahead // nchunk, ahead % nchunk)
                consume((idx // nchunk) % 2, kv + idx // nchunk, idx % nchunk)

        trips = (nsub * nq // per_trip) * step + (nsub * g) // per_trip
        left = (nsub * g) % per_trip

        def trip(i, carry):
            off_diagonal(i * per_trip, per_trip)
            return carry

        lax.fori_loop(0, trips, trip, 0)
        if left:
            off_diagonal(trips * per_trip, left)

        kv0 = trips * per_trip + left
        units = [(d, c) for d in range(nsub) for c in range(nchunk) if c // 2 >= d]
        assert units[:lead] == [(0, c) for c in range(lead)]
        for i, (d, c) in enumerate(units):
            if i + lead < len(units):
                nd, nc = units[i + lead]
                (produce_diagonal if nc // 2 == nd else produce)(nd % 2, kv0 + nd, nc)
            consume(d % 2, kv0 + d, c, diagonal=(c // 2 == d))
            if c % 2 == 1 and c // 2 == d:
                finalize(d)

    for g in range(nq):
        query_block(g)


def _attention(q_t, k, v_t, za, lq1, lk1, lq2, lk2, attn_norm_w):
    bsz, seq, d_attn = k.shape
    hw = ATTN_V_DIM
    nh = d_attn // hw
    tk = TK_ATTN
    nsub = NSUB_ATTN
    nq = NQ_ATTN
    tq = nsub * tk
    rows = nq * tq
    nblk = seq // tk
    assert nsub % 2 == 0 and seq % rows == 0
    assert q_t.shape == (bsz, nh, nblk, hw, 2 * tk) and v_t.shape == (bsz, nh, nblk, hw, tk)
    grid = (bsz, nh, seq // rows)
    qrow = lambda b, h, i: (b, i, h)
    const2 = lambda b, h, i: (0, 0)
    vec = pl.BlockSpec((1, ATTN_QK_DIM), const2)
    in_specs = [
        pl.BlockSpec((1, 1, nq * nsub, hw, 2 * tk), lambda b, h, i: (b, h, i, 0, 0)),
        pl.BlockSpec((1, seq, hw), lambda b, h, i: (b, 0, h)),
        pl.BlockSpec((1, 1, nblk, hw, tk), lambda b, h, i: (b, h, 0, 0, 0)),
        pl.BlockSpec((1, rows, hw), qrow),
        vec, vec, vec, vec,
        pl.BlockSpec((1, hw), const2),
    ]
    width = 2 * tq
    nscr = ATTN_SCRATCH_SETS
    assert nscr >= min(nq, 2)
    vmem = (2 * 2 * seq * hw * 2 + nscr * 2 * tk * width * 4
            + nscr * (2 * hw + BF16_SUBLANE_PACK) * width * 4 + 4 * tk * width * 4)
    return pl.pallas_call(
        functools.partial(_attn_kernel, tk=tk, nsub=nsub, nq=nq), grid=grid, in_specs=in_specs,
        out_specs=pl.BlockSpec((1, rows, hw), qrow),
        out_shape=jax.ShapeDtypeStruct((bsz, seq, d_attn), BF16),
        scratch_shapes=[pltpu.VMEM((nscr, 2, tk, width), F32),
                        pltpu.VMEM((nscr, 2, 1, width), F32), pltpu.VMEM((nscr, 1, width), F32),
                        pltpu.VMEM((nscr, hw + BF16_SUBLANE_PACK, width), F32)],
        compiler_params=pltpu.CompilerParams(
            dimension_semantics=("arbitrary", "arbitrary", "arbitrary"),
            vmem_limit_bytes=_vmem_limit(vmem)),
        name="diff_attention",
    )(q_t, k, v_t, za, lq1.reshape(1, -1), lk1.reshape(1, -1), lq2.reshape(1, -1),
      lk2.reshape(1, -1), attn_norm_w.reshape(1, -1))


def _outproj_kernel(ys_ref, ya_ref, w_ref, x_ref, gate_ref, g_ref, b_ref, o_ref, *, d_ssd, alpha,
                    tm, rows_per_pass):
    for r0 in range(0, tm, rows_per_pass):
        rows = slice(r0, r0 + rows_per_pass)
        mixed = (_dot(ys_ref[0, rows, :], w_ref[0:d_ssd, :])
                 + _dot(ya_ref[0, rows, :], w_ref[d_ssd:, :]))
        xf = alpha * x_ref[0, rows, :] + gate_ref[0] * mixed
        mu = jnp.mean(xf, axis=-1, keepdims=True)
        xc = xf - mu
        var = jnp.mean(xc * xc, axis=-1, keepdims=True)
        o_ref[0, rows, :] = xc * lax.rsqrt(var + EPS) * g_ref[...] + b_ref[...]


def _out_projection(y_ssd, y_attn, w_out, x, gate, ln_g, ln_b, alpha):
    bsz, seq, d = x.shape
    d_ssd = y_ssd.shape[-1]
    d_attn = y_attn.shape[-1]
    tm = TM_OUT
    row = lambda b, l: (b, l, 0)
    const2 = lambda b, l: (0, 0)
    in_specs = [
        pl.BlockSpec((1, tm, d_ssd), row),
        pl.BlockSpec((1, tm, d_attn), row),
        pl.BlockSpec((d_ssd + d_attn, d), const2, pipeline_mode=pl.Buffered(1)),
        pl.BlockSpec((1, tm, d), row),
        pl.BlockSpec((1, 1, d), lambda b, l: (b, 0, 0)),
        pl.BlockSpec((1, d), const2),
        pl.BlockSpec((1, d), const2),
    ]
    vmem = (d_ssd + d_attn) * d * 2 + 2 * tm * (d_ssd + d_attn) * 2 + 4 * tm * d * 4 + 4 * tm * d * 4
    return pl.pallas_call(
        functools.partial(_outproj_kernel, d_ssd=d_ssd, alpha=alpha, tm=tm,
                          rows_per_pass=TM_OUT_PASS),
        grid=(bsz, seq // tm), in_specs=in_specs,
        out_specs=pl.BlockSpec((1, tm, d), row),
        out_shape=jax.ShapeDtypeStruct((bsz, seq, d), F32),
        compiler_params=pltpu.CompilerParams(
            dimension_semantics=("arbitrary", "arbitrary"), vmem_limit_bytes=_vmem_limit(vmem)),
        name="out_projection",
    )(y_ssd, y_attn, w_out, x, gate, ln_g.reshape(1, -1), ln_b.reshape(1, -1))


def _layer(x, c, w_ada, b_ada, w_in, conv_w, conv_b, dt_bias, a_log, d_skip, ssd_norm_w,
           lambda_q1, lambda_k1, lambda_q2, lambda_k2, attn_norm_w, w_out, ln_g, ln_b, depth):
    bsz, seq, d = x.shape
    nheads = a_log.shape[0]
    d_ssd = nheads * SSD_HEAD_DIM
    d_xbc = conv_w.shape[1]
    d_attn = w_out.shape[0] - d_ssd
    assert d_xbc == d_ssd + 2 * SSD_GROUPS * SSD_STATE
    assert w_in.shape[1] == d_ssd + d_xbc + nheads + 4 * d_attn
    assert seq % TM_INPROJ == 0 and seq % TM_OUT == 0 and seq % CHUNK == 0
    assert TM_INPROJ == TK_ATTN, "the in-projection writes q^T / v^T in attention-sized time blocks"

    mod = _modulation(c, w_ada, b_ada)
    shift = mod[:, None, 0:d]
    scale = mod[:, None, d:2 * d]
    gate = mod[:, None, 2 * d:3 * d]

    off_dt = d_ssd + d_xbc
    off_q = off_dt + nheads
    off_k, off_v, off_za = off_q + d_attn, off_q + 2 * d_attn, off_q + 3 * d_attn
    w_t = w_in.T
    w_main = tuple(w_t[lo:hi].astype(BF16).T
                   for lo, hi in ((0, off_dt), (off_k, off_v), (off_za, off_za + d_attn)))
    w_qv_t = jnp.concatenate([w_t[off_q:off_k], w_t[off_v:off_za]], axis=0).astype(BF16)
    w_dt = w_t[off_dt:off_q]

    zs, xs, bm, cm, q_t, k, v_t, za, dt, dt_t = _in_projection(
        x, scale, shift, w_main, w_qv_t, w_dt, conv_w, conv_b, dt_bias,
        d_ssd=d_ssd, d_xbc=d_xbc, d_attn=d_attn)
    y_ssd = _ssd(xs, bm, cm, zs, dt, dt_t, a_log, d_skip, ssd_norm_w)
    y_attn = _attention(q_t, k, v_t, za, lambda_q1, lambda_k1, lambda_q2, lambda_k2, attn_norm_w)
    alpha = (2.0 * depth) ** 0.25
    return _out_projection(y_ssd, y_attn, w_out.astype(BF16), x, gate, ln_g, ln_b, alpha)


def kernel(x, c, w_ada, b_ada, w_in, conv_w, conv_b, dt_bias, a_log, d_skip, ssd_norm_w, lambda_q1,
           lambda_k1, lambda_q2, lambda_k2, attn_norm_w, w_out, ln_g, ln_b):
    depth = w_in.shape[0]
    assert depth == 1, "lambda_init and the layer loop are specialised to a single layer"
    return _layer(x, c, w_ada[0], b_ada[0], w_in[0], conv_w[0], conv_b[0], dt_bias[0], a_log[0],
                  d_skip[0], ssd_norm_w[0], lambda_q1[0], lambda_k1[0], lambda_q2[0], lambda_k2[0],
                  attn_norm_w[0], w_out[0], ln_g[0], ln_b[0], depth)
```

```python
import functools
import math

import jax
import jax.numpy as jnp
from jax import lax
from jax.experimental import pallas as pl
from jax.experimental.pallas import tpu as pltpu

F32 = jnp.float32
BF16 = jnp.bfloat16

SSD_HEAD_DIM = 64
SSD_GROUPS = 2
SSD_STATE = 128
CONV_WIDTH = 4
CHUNK = 128
ATTN_QK_DIM = 64
ATTN_V_DIM = 2 * ATTN_QK_DIM
EPS = 1e-5
LAMBDA_INIT_L0 = 0.8 - 0.6 * math.exp(-0.3 * 0)
LOG2_E = math.log2(math.e)

V7X_LANES = 128
V7X_SUBLANES = 8
BF16_SUBLANE_PACK = 16
V7X_VMEM_BYTES = 64 * 1024 * 1024
V7X_VMEM_DEFAULT_SCOPED_BYTES = V7X_VMEM_BYTES // 2
V7X_VMEM_REQUEST_CAP_BYTES = V7X_VMEM_BYTES * 7 // 8

TM_INPROJ = 512
TM_INPROJ_PASS = 256
TK_ATTN = 512
NSUB_ATTN = 2
NQ_ATTN = 4
ATTN_SCRATCH_SETS = 2
ATTN_PRODUCE_LEAD = 2
ATTN_BLOCKS_PER_TRIP = 4
TM_OUT = 1024
TM_OUT_PASS = 256
SSD_CHUNKS_PER_STEP = 4


def _vmem_limit(estimate_bytes):
    return int(min(V7X_VMEM_REQUEST_CAP_BYTES, max(V7X_VMEM_DEFAULT_SCOPED_BYTES, estimate_bytes)))


def _silu(x):
    hx = 0.5 * x
    return hx + hx * jnp.tanh(hx)


def _softplus(x):
    return jnp.maximum(x, 0.0) + jnp.log(1.0 + jnp.exp(-jnp.abs(x)))


def _split_bf16(a, pieces):
    out = []
    r = a
    for i in range(pieces):
        p = r.astype(BF16)
        out.append(p)
        if i + 1 < pieces:
            r = r - p.astype(F32)
    return out


def _dot(a, b):
    return jnp.dot(a, b, preferred_element_type=F32)


def _dot_nt(a, b):
    return lax.dot_general(a, b, (((1,), (1,)), ((), ())), preferred_element_type=F32)


def _dot_tn(a, b):
    return lax.dot_general(a, b, (((0,), (0,)), ((), ())), preferred_element_type=F32)


def _mod_kernel(c_ref, w_ref, b_ref, o_ref):
    c_hi, c_lo = _split_bf16(c_ref[...], 2)
    w_hi, w_lo = _split_bf16(w_ref[...], 2)
    o_ref[...] = _dot(c_hi, w_hi) + _dot(c_hi, w_lo) + _dot(c_lo, w_hi) + b_ref[...]


def _modulation(c, w_ada, b_ada):
    bsz, d = c.shape
    n = w_ada.shape[1]
    rows = V7X_SUBLANES
    c_pad = jnp.zeros((rows, d), F32).at[:bsz].set(c)
    bn = d
    out = pl.pallas_call(
        _mod_kernel,
        grid=(n // bn,),
        in_specs=[pl.BlockSpec((rows, d), lambda j: (0, 0)),
                  pl.BlockSpec((d, bn), lambda j: (0, j)),
                  pl.BlockSpec((1, bn), lambda j: (0, j))],
        out_specs=pl.BlockSpec((rows, bn), lambda j: (0, j)),
        out_shape=jax.ShapeDtypeStruct((rows, n), F32),
        name="modulation",
    )(c_pad, w_ada, b_ada.reshape(1, n))
    return out[:bsz]


def _inproj_kernel(x_ref, scale_ref, shift_ref, wzx_ref, wk_ref, wza_ref, wqvT_ref, wdt_ref,
                   convw_ref, convb_ref, dtb_ref,
                   zs_ref, xs_ref, bm_ref, cm_ref, qT_ref, k_ref, vT_ref, za_ref, dt_ref, dtT_ref,
                   carry_ref, *, tm, rows_per_pass, d_ssd, d_xbc, d_attn, n_bc):
    @pl.when(pl.program_id(1) == 0)
    def _():
        carry_ref[...] = jnp.zeros_like(carry_ref)

    pad = V7X_SUBLANES
    nh = dt_ref.shape[-1]
    nh_attn = d_attn // ATTN_V_DIM
    cw = convw_ref[...]
    w_hi, w_lo = _split_bf16(wdt_ref[...], 2)
    wrow = lax.broadcasted_iota(jnp.int32, (V7X_LANES, 1), 0)
    w_hilo = jnp.where(wrow < nh, w_hi, w_lo)

    tail = carry_ref[...]
    for r0 in range(0, tm, rows_per_pass):
        rows = slice(r0, r0 + rows_per_pass)
        h32 = x_ref[0, rows, :] * (1.0 + scale_ref[0]) + shift_ref[0]
        h = h32.astype(BF16)

        u = _dot(h, wzx_ref[:, d_ssd:d_ssd + d_xbc])
        u_ext = jnp.concatenate([tail, u], axis=0)
        tail = u[rows_per_pass - pad:rows_per_pass, :]
        acc = convb_ref[...] + cw[CONV_WIDTH - 1:CONV_WIDTH, :] * u
        for tap in range(CONV_WIDTH - 1):
            shift = CONV_WIDTH - 1 - tap
            acc = acc + (cw[tap:tap + 1, :]
                         * pltpu.roll(u_ext, shift, 0)[pad:pad + rows_per_pass, :])
        xbc = _silu(acc)
        xs_ref[0, rows, :] = xbc[:, :d_ssd].astype(BF16)
        bm_ref[0, rows, :] = xbc[:, d_ssd:d_ssd + n_bc].astype(BF16)
        cm_ref[0, rows, :] = xbc[:, d_ssd + n_bc:d_ssd + 2 * n_bc].astype(BF16)

        zs_ref[0, rows, :] = _silu(_dot(h, wzx_ref[:, 0:d_ssd])).astype(BF16)
        k_ref[0, rows, :] = _dot(h, wk_ref[...]).astype(BF16)
        za_ref[0, rows, :] = _silu(_dot(h, wza_ref[...])).astype(BF16)

        qvT = _dot_nt(wqvT_ref[...], h)
        qT = qvT[:d_attn] * (ATTN_QK_DIM ** -0.5 * LOG2_E)
        qT = qT.astype(BF16).reshape(nh_attn, ATTN_V_DIM, rows_per_pass)
        feat = lax.broadcasted_iota(jnp.int32, (1, ATTN_V_DIM, 1), 1)
        zero = jnp.zeros_like(qT)
        qT_ref[0, :, 0, :, rows] = jnp.where(feat < ATTN_QK_DIM, qT, zero)
        qT_ref[0, :, 0, :, tm + r0:tm + r0 + rows_per_pass] = jnp.where(feat >= ATTN_QK_DIM, qT, zero)
        vT_ref[0, :, 0, :, rows] = qvT[d_attn:].astype(BF16).reshape(nh_attn, ATTN_V_DIM,
                                                                      rows_per_pass)

        r1 = _dot_nt(h, w_hilo)
        dt_raw = r1 + pltpu.roll(r1, V7X_LANES - nh, 1)
        dt = _softplus(dt_raw + dtb_ref[...])
        dt_ref[0, rows, :] = dt[:, :nh]
        dtT_ref[0, :, rows] = dt.T[:nh, :]
    carry_ref[...] = tail


def _in_projection(x, scale, shift, w_main, w_qv_t, w_dt, conv_w, conv_b, dt_bias, *, d_ssd, d_xbc,
                   d_attn):
    bsz, seq, d = x.shape
    tm = TM_INPROJ
    nh = w_dt.shape[0]
    nh_attn = d_attn // ATTN_V_DIM
    n_bc = (d_xbc - d_ssd) // 2
    w_zx, w_k, w_za = w_main
    assert w_zx.shape == (d, d_ssd + d_xbc) and w_k.shape == (d, d_attn) and w_za.shape == (d, d_attn)
    n_main = w_zx.shape[1] + w_k.shape[1] + w_za.shape[1]
    grid = (bsz, seq // tm)

    row = lambda b, l: (b, l, 0)
    head_major = lambda b, l: (b, 0, l, 0, 0)
    const2 = lambda b, l: (0, 0)
    per_b = lambda b, l: (b, 0, 0)
    single = pl.Buffered(1)
    in_specs = [
        pl.BlockSpec((1, tm, d), row),
        pl.BlockSpec((1, 1, d), per_b),
        pl.BlockSpec((1, 1, d), per_b),
        pl.BlockSpec(w_zx.shape, const2, pipeline_mode=single),
        pl.BlockSpec(w_k.shape, const2, pipeline_mode=single),
        pl.BlockSpec(w_za.shape, const2, pipeline_mode=single),
        pl.BlockSpec((2 * d_attn, d), const2, pipeline_mode=single),
        pl.BlockSpec((V7X_LANES, d), const2),
        pl.BlockSpec((CONV_WIDTH, d_xbc), const2),
        pl.BlockSpec((1, d_xbc), const2),
        pl.BlockSpec((1, V7X_LANES), const2),
    ]
    out_specs = [
        pl.BlockSpec((1, tm, d_ssd), row),
        pl.BlockSpec((1, tm, d_ssd), row),
        pl.BlockSpec((1, tm, n_bc), row),
        pl.BlockSpec((1, tm, n_bc), row),
        pl.BlockSpec((1, nh_attn, 1, ATTN_V_DIM, 2 * tm), head_major),
        pl.BlockSpec((1, tm, d_attn), row),
        pl.BlockSpec((1, nh_attn, 1, ATTN_V_DIM, tm), head_major),
        pl.BlockSpec((1, tm, d_attn), row),
        pl.BlockSpec((1, tm, nh), row),
        pl.BlockSpec((1, nh, tm), lambda b, l: (b, 0, l)),
    ]
    out_shape = [
        jax.ShapeDtypeStruct((bsz, seq, d_ssd), BF16),
        jax.ShapeDtypeStruct((bsz, seq, d_ssd), BF16),
        jax.ShapeDtypeStruct((bsz, seq, n_bc), BF16),
        jax.ShapeDtypeStruct((bsz, seq, n_bc), BF16),
        jax.ShapeDtypeStruct((bsz, nh_attn, seq // tm, ATTN_V_DIM, 2 * tm), BF16),
        jax.ShapeDtypeStruct((bsz, seq, d_attn), BF16),
        jax.ShapeDtypeStruct((bsz, nh_attn, seq // tm, ATTN_V_DIM, tm), BF16),
        jax.ShapeDtypeStruct((bsz, seq, d_attn), BF16),
        jax.ShapeDtypeStruct((bsz, seq, nh), F32),
        jax.ShapeDtypeStruct((bsz, nh, seq), F32),
    ]
    pad = V7X_SUBLANES
    vmem = (d * (n_main + 2 * d_attn) * 2 + 2 * tm * d * 4
            + 2 * tm * (2 * d_ssd + 2 * n_bc + 4 * d_attn) * 2
            + (tm + 2 * pad) * d_xbc * 4 + 6 * tm * d_xbc * 4)
    kern = functools.partial(_inproj_kernel, tm=tm, rows_per_pass=TM_INPROJ_PASS, d_ssd=d_ssd,
                             d_xbc=d_xbc, d_attn=d_attn, n_bc=n_bc)
    assert 2 * nh <= V7X_LANES
    w_dt_pad = jnp.concatenate([w_dt, w_dt, jnp.zeros((V7X_LANES - 2 * nh, d), F32)], axis=0)
    dt_bias_pad = jnp.concatenate([dt_bias, jnp.zeros((V7X_LANES - nh,), F32)]).reshape(1, -1)
    return pl.pallas_call(
        kern, grid=grid, in_specs=in_specs, out_specs=out_specs, out_shape=out_shape,
        scratch_shapes=[pltpu.VMEM((pad, d_xbc), F32)],
        compiler_params=pltpu.CompilerParams(
            dimension_semantics=("arbitrary", "arbitrary"), vmem_limit_bytes=_vmem_limit(vmem)),
        name="in_projection",
    )(x, scale, shift, w_zx, w_k, w_za, w_qv_t, w_dt_pad, conv_w, conv_b.reshape(1, -1), dt_bias_pad)


def _ssd_kernel(xs_ref, bm_ref, cm_ref, zs_ref, dt_ref, dtT_ref, alog_ref, alogT_ref, dskip_ref,
                normw_ref, y_ref, state_ref, ybuf_ref, *, nheads, d_ssd, cps):
    t = CHUNK
    n = SSD_STATE
    hp = SSD_HEAD_DIM
    gw = d_ssd // SSD_GROUPS
    heads_per_group = nheads // SSD_GROUPS

    @pl.when(pl.program_id(1) == 0)
    def _():
        state_ref[...] = jnp.zeros_like(state_ref)

    row = lax.broadcasted_iota(jnp.int32, (t, t), 0)
    col = lax.broadcasted_iota(jnp.int32, (t, t), 1)
    causal = col <= row
    tri = jnp.where(causal, 1.0, 0.0).astype(BF16)
    tri_t = jnp.where(row <= col, 1.0, 0.0).astype(BF16)
    eh = lax.broadcasted_iota(jnp.int32, (nheads, d_ssd), 0)
    ej = lax.broadcasted_iota(jnp.int32, (nheads, d_ssd), 1)
    expand_m = jnp.where(ej // hp == eh, 1.0, 0.0).astype(BF16)
    lane = lax.broadcasted_iota(jnp.int32, (1, 2 * hp), 1)
    neg_inf = jnp.float32(-jnp.inf)
    a = -jnp.exp(alog_ref[...])
    a_t = -jnp.exp(alogT_ref[...])

    def expand(v, pieces):
        return sum(_dot(p, expand_m) for p in _split_bf16(v, pieces))

    for ci in range(cps):
        rows = slice(ci * t, (ci + 1) * t)
        dt = dt_ref[0, rows, :]
        dt_t = dtT_ref[0, :, rows]
        acum = sum(_dot(tri, p) for p in _split_bf16(dt * a, 3))
        acum_t = sum(_dot(p, tri_t) for p in _split_bf16(dt_t * a_t, 3))
        last = acum[t - 1:t, :]
        dt_x = expand(dt, 1)
        w_x = expand(jnp.exp(last - acum), 1)
        e_x = expand(jnp.exp(acum), 2)

        xs = xs_ref[0, rows, :].astype(F32)
        xdt = xs * dt_x
        xdt_b = xdt.astype(BF16)
        xw_b = (xdt * w_x).astype(BF16)

        for g in range(SSD_GROUPS):
            b_g = bm_ref[0, rows, g * n:(g + 1) * n]
            c_g = cm_ref[0, rows, g * n:(g + 1) * n]
            gsl = slice(g * gw, (g + 1) * gw)
            cb = _dot_nt(c_g, b_g)
            prev = state_ref[g]
            y_off = _dot(c_g, prev.astype(BF16)) * e_x[:, gsl]
            state_ref[g] = prev * e_x[t - 1:t, gsl] + _dot_tn(b_g, xw_b[:, gsl])
            for pair in range(heads_per_group // 2):
                lo = g * gw + pair * 2 * hp
                x_pair = xdt_b[:, lo:lo + 2 * hp]
                decays = []
                for half in range(2):
                    hd = g * heads_per_group + pair * 2 + half
                    diff = acum[:, hd:hd + 1] - acum_t[hd:hd + 1, :]
                    decays.append((cb * jnp.exp(jnp.where(causal, diff, neg_inf))).astype(BF16))
                zero = jnp.zeros_like(x_pair)
                x_diag = jnp.concatenate([jnp.where(lane < hp, x_pair, zero),
                                          jnp.where(lane >= hp, x_pair, zero)], axis=0)
                y_pair = (y_off[:, pair * 2 * hp:(pair + 1) * 2 * hp]
                          + _dot(jnp.concatenate(decays, axis=1), x_diag))
                ybuf_ref[rows, lo:lo + 2 * hp] = y_pair

    xs = xs_ref[0].astype(F32)
    y = ybuf_ref[...] + dskip_ref[...] * xs
    y = y * zs_ref[0].astype(F32)
    for g in range(SSD_GROUPS):
        gsl = slice(g * gw, (g + 1) * gw)
        y_g = y[:, gsl]
        ms = jnp.mean(y_g * y_g, axis=-1, keepdims=True)
        y_ref[0, :, gsl] = (y_g * lax.rsqrt(ms + EPS) * normw_ref[:, gsl]).astype(BF16)


def _ssd(xs, bm, cm, zs, dt, dt_t, a_log, d_skip, ssd_norm_w):
    bsz, seq, d_ssd = xs.shape
    nheads = dt.shape[-1]
    n_bc = bm.shape[-1]
    cps = SSD_CHUNKS_PER_STEP
    t = cps * CHUNK
    assert seq % t == 0
    grid = (bsz, seq // t)
    row = lambda b, c: (b, c, 0)
    const2 = lambda b, c: (0, 0)
    in_specs = [
        pl.BlockSpec((1, t, d_ssd), row),
        pl.BlockSpec((1, t, n_bc), row),
        pl.BlockSpec((1, t, n_bc), row),
        pl.BlockSpec((1, t, d_ssd), row),
        pl.BlockSpec((1, t, nheads), row),
        pl.BlockSpec((1, nheads, t), lambda b, c: (b, 0, c)),
        pl.BlockSpec((1, nheads), const2),
        pl.BlockSpec((nheads, 1), const2),
        pl.BlockSpec((1, d_ssd), const2),
        pl.BlockSpec((1, d_ssd), const2),
    ]
    kern = functools.partial(_ssd_kernel, nheads=nheads, d_ssd=d_ssd, cps=cps)
    return pl.pallas_call(
        kern, grid=grid, in_specs=in_specs,
        out_specs=pl.BlockSpec((1, t, d_ssd), row),
        out_shape=jax.ShapeDtypeStruct((bsz, seq, d_ssd), BF16),
        scratch_shapes=[pltpu.VMEM((SSD_GROUPS, SSD_STATE, d_ssd // SSD_GROUPS), F32),
                        pltpu.VMEM((t, d_ssd), F32)],
        compiler_params=pltpu.CompilerParams(dimension_semantics=("arbitrary", "arbitrary")),
        name="ssd_scan",
    )(xs, bm, cm, zs, dt, dt_t, a_log.reshape(1, -1), a_log.reshape(-1, 1),
      jnp.repeat(d_skip, SSD_HEAD_DIM).reshape(1, -1), ssd_norm_w.reshape(1, -1))


def _attn_kernel(qT_ref, k_ref, vT_ref, za_ref, lq1_ref, lk1_ref, lq2_ref, lk2_ref, normw_ref,
                 o_ref, s_ref, cmax_ref, m_ref, acc_ref, *, tk, nsub, nq):
    step = pl.program_id(2)
    dk = ATTN_QK_DIM
    dv = 2 * dk
    sw = 2 * tk
    nchunk = 2 * nsub
    tq = nsub * tk
    half = tk // 2
    lead = ATTN_PRODUCE_LEAD
    per_trip = ATTN_BLOCKS_PER_TRIP
    assert lead <= nchunk and per_trip % 2 == 0 and nsub % 2 == 0 and (nsub * nq) % per_trip == 0
    ones_rows = jnp.ones((BF16_SUBLANE_PACK, tk), BF16)
    lam = (jnp.exp(jnp.sum(lq1_ref[...] * lk1_ref[...], axis=-1, keepdims=True))
           - jnp.exp(jnp.sum(lq2_ref[...] * lk2_ref[...], axis=-1, keepdims=True))
           + LAMBDA_INIT_L0)

    def col(c):
        return slice(c * tk, (c + 1) * tk)

    def query_block(g):
        sc = g % ATTN_SCRATCH_SETS

        def q_columns(c):
            return qT_ref[0, 0, g * nsub + c // 2, :, (c % 2) * tk:(c % 2 + 1) * tk]

        def produce(slot, kv, c):
            start = pl.multiple_of(kv * tk, tk)
            s = _dot(k_ref[0, pl.ds(start, tk), :], q_columns(c))
            s_ref[sc, slot, :, col(c)] = s
            cmax_ref[sc, slot, :, col(c)] = jnp.max(s, axis=0, keepdims=True)

        def produce_diagonal(slot, kv, c):
            start = pl.multiple_of(kv * tk, tk)
            lo = c * tk
            s_ref[sc, slot, 0:half, lo:lo + tk] = _dot(k_ref[0, pl.ds(start, half), :], q_columns(c))
            s_ref[sc, slot, half:tk, lo + half:lo + tk] = _dot(
                k_ref[0, pl.ds(start + half, half), :], q_columns(c)[:, half:tk])

        def update(s, cmax, kv, cols, nrows):
            m_prev = m_ref[sc, :, cols]
            m_new = jnp.maximum(m_prev, cmax)
            alpha = jnp.exp2(m_prev - m_new)
            p = jnp.exp2(s - m_new).astype(BF16)
            v_ext = jnp.concatenate([vT_ref[0, 0, kv, :, 0:nrows], ones_rows[:, 0:nrows]], axis=0)
            acc_ref[sc, :, cols] = alpha * acc_ref[sc, :, cols] + _dot(v_ext, p)
            m_ref[sc, :, cols] = m_new

        def consume(slot, kv, c, diagonal=False):
            if not diagonal:
                update(s_ref[sc, slot, :, col(c)], cmax_ref[sc, slot, :, col(c)], kv, col(c), tk)
                return
            lo = c * tk
            for cols, nrows in ((slice(lo, lo + half), half), (slice(lo + half, lo + tk), tk)):
                s = s_ref[sc, slot, 0:nrows, cols]
                r = lax.broadcasted_iota(jnp.int32, s.shape, 0)
                qpos = lax.broadcasted_iota(jnp.int32, s.shape, 1) + (cols.start - lo)
                s = jnp.where(r <= qpos, s, -jnp.inf)
                update(s, jnp.max(s, axis=0, keepdims=True), kv, cols, nrows)

        def finalize(sub):
            c1 = sub * sw
            c2 = c1 + tk
            o1 = acc_ref[sc, 0:dv, c1:c1 + tk] / acc_ref[sc, dv:dv + 1, c1:c1 + tk]
            o2 = acc_ref[sc, 0:dv, c2:c2 + tk] / acc_ref[sc, dv:dv + 1, c2:c2 + tk]
            o = (o1 - lam * o2).T
            ms = jnp.mean(o * o, axis=-1, keepdims=True)
            o = o * lax.rsqrt(ms + EPS) * normw_ref[...] * (1.0 - LAMBDA_INIT_L0)
            rows = slice(g * tq + sub * tk, g * tq + (sub + 1) * tk)
            o_ref[0, rows, :] = (o * za_ref[0, rows, :].astype(F32)).astype(BF16)

        m_ref[sc] = jnp.full(m_ref.shape[1:], -jnp.inf, F32)
        acc_ref[sc] = jnp.zeros(acc_ref.shape[1:], F32)

        for c in range(lead):
            produce(0, 0, c)

        def off_diagonal(kv, nblocks):
            for idx in range(nblocks * nchunk):
                ahead = idx + lead
                produce((ahead // nchunk) % 2, kv + ahead // nchunk, ahead % nchunk)
                consume((idx // nchunk) % 2, kv + idx // nchunk, idx % nchunk)

        trips = (nsub * nq // per_trip) * step + (nsub * g) // per_trip
        left = (nsub * g) % per_trip

        def trip(i, carry):
            off_diagonal(i * per_trip, per_trip)
            return carry

        lax.fori_loop(0, trips, trip, 0)
        if left:
            off_diagonal(trips * per_trip, left)

        kv0 = trips * per_trip + left
        units = [(d, c) for d in range(nsub) for c in range(nchunk) if c // 2 >= d]
        assert units[:lead] == [(0, c) for c in range(lead)]
        for i, (d, c) in enumerate(units):
            if i + lead < len(units):
                nd, nc = units[i + lead]
                (produce_diagonal if nc // 2 == nd else produce)(nd % 2, kv0 + nd, nc)
            consume(d % 2, kv0 + d, c, diagonal=(c // 2 == d))
            if c % 2 == 1 and c // 2 == d:
                finalize(d)

    for g in range(nq):
        query_block(g)


def _attention(q_t, k, v_t, za, lq1, lk1, lq2, lk2, attn_norm_w):
    bsz, seq, d_attn = k.shape
    hw = ATTN_V_DIM
    nh = d_attn // hw
    tk = TK_ATTN
    nsub = NSUB_ATTN
    nq = NQ_ATTN
    tq = nsub * tk
    rows = nq * tq
    nblk = seq // tk
    assert nsub % 2 == 0 and seq % rows == 0
    assert q_t.shape == (bsz, nh, nblk, hw, 2 * tk) and v_t.shape == (bsz, nh, nblk, hw, tk)
    grid = (bsz, nh, seq // rows)
    qrow = lambda b, h, i: (b, i, h)
    const2 = lambda b, h, i: (0, 0)
    vec = pl.BlockSpec((1, ATTN_QK_DIM), const2)
    in_specs = [
        pl.BlockSpec((1, 1, nq * nsub, hw, 2 * tk), lambda b, h, i: (b, h, i, 0, 0)),
        pl.BlockSpec((1, seq, hw), lambda b, h, i: (b, 0, h)),
        pl.BlockSpec((1, 1, nblk, hw, tk), lambda b, h, i: (b, h, 0, 0, 0)),
        pl.BlockSpec((1, rows, hw), qrow),
        vec, vec, vec, vec,
        pl.BlockSpec((1, hw), const2),
    ]
    width = 2 * tq
    nscr = ATTN_SCRATCH_SETS
    assert nscr >= min(nq, 2)
    vmem = (2 * 2 * seq * hw * 2 + nscr * 2 * tk * width * 4
            + nscr * (2 * hw + BF16_SUBLANE_PACK) * width * 4 + 4 * tk * width * 4)
    return pl.pallas_call(
        functools.partial(_attn_kernel, tk=tk, nsub=nsub, nq=nq), grid=grid, in_specs=in_specs,
        out_specs=pl.BlockSpec((1, rows, hw), qrow),
        out_shape=jax.ShapeDtypeStruct((bsz, seq, d_attn), BF16),
        scratch_shapes=[pltpu.VMEM((nscr, 2, tk, width), F32),
                        pltpu.VMEM((nscr, 2, 1, width), F32), pltpu.VMEM((nscr, 1, width), F32),
                        pltpu.VMEM((nscr, hw + BF16_SUBLANE_PACK, width), F32)],
        compiler_params=pltpu.CompilerParams(
            dimension_semantics=("arbitrary", "arbitrary", "arbitrary"),
            vmem_limit_bytes=_vmem_limit(vmem)),
        name="diff_attention",
    )(q_t, k, v_t, za, lq1.reshape(1, -1), lk1.reshape(1, -1), lq2.reshape(1, -1),
      lk2.reshape(1, -1), attn_norm_w.reshape(1, -1))


def _outproj_kernel(ys_ref, ya_ref, w_ref, x_ref, gate_ref, g_ref, b_ref, o_ref, *, d_ssd, alpha,
                    tm, rows_per_pass):
    for r0 in range(0, tm, rows_per_pass):
        rows = slice(r0, r0 + rows_per_pass)
        mixed = (_dot(ys_ref[0, rows, :], w_ref[0:d_ssd, :])
                 + _dot(ya_ref[0, rows, :], w_ref[d_ssd:, :]))
        xf = alpha * x_ref[0, rows, :] + gate_ref[0] * mixed
        mu = jnp.mean(xf, axis=-1, keepdims=True)
        xc = xf - mu
        var = jnp.mean(xc * xc, axis=-1, keepdims=True)
        o_ref[0, rows, :] = xc * lax.rsqrt(var + EPS) * g_ref[...] + b_ref[...]


def _out_projection(y_ssd, y_attn, w_out, x, gate, ln_g, ln_b, alpha):
    bsz, seq, d = x.shape
    d_ssd = y_ssd.shape[-1]
    d_attn = y_attn.shape[-1]
    tm = TM_OUT
    row = lambda b, l: (b, l, 0)
    const2 = lambda b, l: (0, 0)
    in_specs = [
        pl.BlockSpec((1, tm, d_ssd), row),
        pl.BlockSpec((1, tm, d_attn), row),
        pl.BlockSpec((d_ssd + d_attn, d), const2, pipeline_mode=pl.Buffered(1)),
        pl.BlockSpec((1, tm, d), row),
        pl.BlockSpec((1, 1, d), lambda b, l: (b, 0, 0)),
        pl.BlockSpec((1, d), const2),
        pl.BlockSpec((1, d), const2),
    ]
    vmem = (d_ssd + d_attn) * d * 2 + 2 * tm * (d_ssd + d_attn) * 2 + 4 * tm * d * 4 + 4 * tm * d * 4
    return pl.pallas_call(
        functools.partial(_outproj_kernel, d_ssd=d_ssd, alpha=alpha, tm=tm,
                          rows_per_pass=TM_OUT_PASS),
        grid=(bsz, seq // tm), in_specs=in_specs,
        out_specs=pl.BlockSpec((1, tm, d), row),
        out_shape=jax.ShapeDtypeStruct((bsz, seq, d), F32),
        compiler_params=pltpu.CompilerParams(
            dimension_semantics=("arbitrary", "arbitrary"), vmem_limit_bytes=_vmem_limit(vmem)),
        name="out_projection",
    )(y_ssd, y_attn, w_out, x, gate, ln_g.reshape(1, -1), ln_b.reshape(1, -1))


def _layer(x, c, w_ada, b_ada, w_in, conv_w, conv_b, dt_bias, a_log, d_skip, ssd_norm_w,
           lambda_q1, lambda_k1, lambda_q2, lambda_k2, attn_norm_w, w_out, ln_g, ln_b, depth):
    bsz, seq, d = x.shape
    nheads = a_log.shape[0]
    d_ssd = nheads * SSD_HEAD_DIM
    d_xbc = conv_w.shape[1]
    d_attn = w_out.shape[0] - d_ssd
    assert d_xbc == d_ssd + 2 * SSD_GROUPS * SSD_STATE
    assert w_in.shape[1] == d_ssd + d_xbc + nheads + 4 * d_attn
    assert seq % TM_INPROJ == 0 and seq % TM_OUT == 0 and seq % CHUNK == 0
    assert TM_INPROJ == TK_ATTN, "the in-projection writes q^T / v^T in attention-sized time blocks"

    mod = _modulation(c, w_ada, b_ada)
    shift = mod[:, None, 0:d]
    scale = mod[:, None, d:2 * d]
    gate = mod[:, None, 2 * d:3 * d]

    off_dt = d_ssd + d_xbc
    off_q = off_dt + nheads
    off_k, off_v, off_za = off_q + d_attn, off_q + 2 * d_attn, off_q + 3 * d_attn
    w_t = w_in.T
    w_main = tuple(w_t[lo:hi].astype(BF16).T
                   for lo, hi in ((0, off_dt), (off_k, off_v), (off_za, off_za + d_attn)))
    w_qv_t = jnp.concatenate([w_t[off_q:off_k], w_t[off_v:off_za]], axis=0).astype(BF16)
    w_dt = w_t[off_dt:off_q]

    zs, xs, bm, cm, q_t, k, v_t, za, dt, dt_t = _in_projection(
        x, scale, shift, w_main, w_qv_t, w_dt, conv_w, conv_b, dt_bias,
        d_ssd=d_ssd, d_xbc=d_xbc, d_attn=d_attn)
    y_ssd = _ssd(xs, bm, cm, zs, dt, dt_t, a_log, d_skip, ssd_norm_w)
    y_attn = _attention(q_t, k, v_t, za, lambda_q1, lambda_k1, lambda_q2, lambda_k2, attn_norm_w)
    alpha = (2.0 * depth) ** 0.25
    return _out_projection(y_ssd, y_attn, w_out.astype(BF16), x, gate, ln_g, ln_b, alpha)


def kernel(x, c, w_ada, b_ada, w_in, conv_w, conv_b, dt_bias, a_log, d_skip, ssd_norm_w, lambda_q1,
           lambda_k1, lambda_q2, lambda_k2, attn_norm_w, w_out, ln_g, ln_b):
    depth = w_in.shape[0]
    assert depth == 1, "lambda_init and the layer loop are specialised to a single layer"
    return _layer(x, c, w_ada[0], b_ada[0], w_in[0], conv_w[0], conv_b[0], dt_bias[0], a_log[0],
                  d_skip[0], ssd_norm_w[0], lambda_q1[0], lambda_k1[0], lambda_q2[0], lambda_k2[0],
                  attn_norm_w[0], w_out[0], ln_g[0], ln_b[0], depth)
```

```python
import functools
import math

import jax
import jax.numpy as jnp
from jax import lax
from jax.experimental import pallas as pl
from jax.experimental.pallas import tpu as pltpu

F32 = jnp.float32
BF16 = jnp.bfloat16

SSD_HEAD_DIM = 64
SSD_GROUPS = 2
SSD_STATE = 128
CONV_WIDTH = 4
CHUNK = 128
ATTN_QK_DIM = 64
ATTN_V_DIM = 2 * ATTN_QK_DIM
EPS = 1e-5
LAMBDA_INIT_L0 = 0.8 - 0.6 * math.exp(-0.3 * 0)
LOG2_E = math.log2(math.e)

V7X_LANES = 128
V7X_SUBLANES = 8
BF16_SUBLANE_PACK = 16
V7X_VMEM_BYTES = 64 * 1024 * 1024
V7X_VMEM_DEFAULT_SCOPED_BYTES = V7X_VMEM_BYTES // 2
V7X_VMEM_REQUEST_CAP_BYTES = V7X_VMEM_BYTES * 7 // 8

TM_INPROJ = 512
TM_INPROJ_PASS = 256
TK_ATTN = 512
NSUB_ATTN = 2
NQ_ATTN = 4
ATTN_SCRATCH_SETS = 2
ATTN_PRODUCE_LEAD = 2
ATTN_BLOCKS_PER_TRIP = 4
TM_OUT = 1024
TM_OUT_PASS = 256
SSD_CHUNKS_PER_STEP = 4


def _vmem_limit(estimate_bytes):
    return int(min(V7X_VMEM_REQUEST_CAP_BYTES, max(V7X_VMEM_DEFAULT_SCOPED_BYTES, estimate_bytes)))


def _silu(x):
    hx = 0.5 * x
    return hx + hx * jnp.tanh(hx)


def _softplus(x):
    return jnp.maximum(x, 0.0) + jnp.log(1.0 + jnp.exp(-jnp.abs(x)))


def _split_bf16(a, pieces):
    out = []
    r = a
    for i in range(pieces):
        p = r.astype(BF16)
        out.append(p)
        if i + 1 < pieces:
            r = r - p.astype(F32)
    return out


def _dot(a, b):
    return jnp.dot(a, b, preferred_element_type=F32)


def _dot_nt(a, b):
    return lax.dot_general(a, b, (((1,), (1,)), ((), ())), preferred_element_type=F32)


def _dot_tn(a, b):
    return lax.dot_general(a, b, (((0,), (0,)), ((), ())), preferred_element_type=F32)


def _mod_kernel(c_ref, w_ref, b_ref, o_ref):
    c_hi, c_lo = _split_bf16(c_ref[...], 2)
    w_hi, w_lo = _split_bf16(w_ref[...], 2)
    o_ref[...] = _dot(c_hi, w_hi) + _dot(c_hi, w_lo) + _dot(c_lo, w_hi) + b_ref[...]


def _modulation(c, w_ada, b_ada):
    bsz, d = c.shape
    n = w_ada.shape[1]
    rows = V7X_SUBLANES
    c_pad = jnp.zeros((rows, d), F32).at[:bsz].set(c)
    bn = d
    out = pl.pallas_call(
        _mod_kernel,
        grid=(n // bn,),
        in_specs=[pl.BlockSpec((rows, d), lambda j: (0, 0)),
                  pl.BlockSpec((d, bn), lambda j: (0, j)),
                  pl.BlockSpec((1, bn), lambda j: (0, j))],
        out_specs=pl.BlockSpec((rows, bn), lambda j: (0, j)),
        out_shape=jax.ShapeDtypeStruct((rows, n), F32),
        name="modulation",
    )(c_pad, w_ada, b_ada.reshape(1, n))
    return out[:bsz]


def _inproj_kernel(x_ref, scale_ref, shift_ref, wzx_ref, wk_ref, wza_ref, wqvT_ref, wdt_ref,
                   convw_ref, convb_ref, dtb_ref,
                   zs_ref, xs_ref, bm_ref, cm_ref, qT_ref, k_ref, vT_ref, za_ref, dt_ref, dtT_ref,
                   carry_ref, *, tm, rows_per_pass, d_ssd, d_xbc, d_attn, n_bc):
    @pl.when(pl.program_id(1) == 0)
    def _():
        carry_ref[...] = jnp.zeros_like(carry_ref)

    pad = V7X_SUBLANES
    nh = dt_ref.shape[-1]
    nh_attn = d_attn // ATTN_V_DIM
    cw = convw_ref[...]
    w_hi, w_lo = _split_bf16(wdt_ref[...], 2)
    wrow = lax.broadcasted_iota(jnp.int32, (V7X_LANES, 1), 0)
    w_hilo = jnp.where(wrow < nh, w_hi, w_lo)

    tail = carry_ref[...]
    for r0 in range(0, tm, rows_per_pass):
        rows = slice(r0, r0 + rows_per_pass)
        h32 = x_ref[0, rows, :] * (1.0 + scale_ref[0]) + shift_ref[0]
        h = h32.astype(BF16)

        u = _dot(h, wzx_ref[:, d_ssd:d_ssd + d_xbc])
        u_ext = jnp.concatenate([tail, u], axis=0)
        tail = u[rows_per_pass - pad:rows_per_pass, :]
        acc = convb_ref[...] + cw[CONV_WIDTH - 1:CONV_WIDTH, :] * u
        for tap in range(CONV_WIDTH - 1):
            shift = CONV_WIDTH - 1 - tap
            acc = acc + (cw[tap:tap + 1, :]
                         * pltpu.roll(u_ext, shift, 0)[pad:pad + rows_per_pass, :])
        xbc = _silu(acc)
        xs_ref[0, rows, :] = xbc[:, :d_ssd].astype(BF16)
        bm_ref[0, rows, :] = xbc[:, d_ssd:d_ssd + n_bc].astype(BF16)
        cm_ref[0, rows, :] = xbc[:, d_ssd + n_bc:d_ssd + 2 * n_bc].astype(BF16)

        zs_ref[0, rows, :] = _silu(_dot(h, wzx_ref[:, 0:d_ssd])).astype(BF16)
        k_ref[0, rows, :] = _dot(h, wk_ref[...]).astype(BF16)
        za_ref[0, rows, :] = _silu(_dot(h, wza_ref[...])).astype(BF16)

        qvT = _dot_nt(wqvT_ref[...], h)
        qT = qvT[:d_attn] * (ATTN_QK_DIM ** -0.5 * LOG2_E)
        qT = qT.astype(BF16).reshape(nh_attn, ATTN_V_DIM, rows_per_pass)
        feat = lax.broadcasted_iota(jnp.int32, (1, ATTN_V_DIM, 1), 1)
        zero = jnp.zeros_like(qT)
        qT_ref[0, :, 0, :, rows] = jnp.where(feat < ATTN_QK_DIM, qT, zero)
        qT_ref[0, :, 0, :, tm + r0:tm + r0 + rows_per_pass] = jnp.where(feat >= ATTN_QK_DIM, qT, zero)
        vT_ref[0, :, 0, :, rows] = qvT[d_attn:].astype(BF16).reshape(nh_attn, ATTN_V_DIM,
                                                                      rows_per_pass)

        r1 = _dot_nt(h, w_hilo)
        dt_raw = r1 + pltpu.roll(r1, V7X_LANES - nh, 1)
        dt = _softplus(dt_raw + dtb_ref[...])
        dt_ref[0, rows, :] = dt[:, :nh]
        dtT_ref[0, :, rows] = dt.T[:nh, :]
    carry_ref[...] = tail


def _in_projection(x, scale, shift, w_main, w_qv_t, w_dt, conv_w, conv_b, dt_bias, *, d_ssd, d_xbc,
                   d_attn):
    bsz, seq, d = x.shape
    tm = TM_INPROJ
    nh = w_dt.shape[0]
    nh_attn = d_attn // ATTN_V_DIM
    n_bc = (d_xbc - d_ssd) // 2
    w_zx, w_k, w_za = w_main
    assert w_zx.shape == (d, d_ssd + d_xbc) and w_k.shape == (d, d_attn) and w_za.shape == (d, d_attn)
    n_main = w_zx.shape[1] + w_k.shape[1] + w_za.shape[1]
    grid = (bsz, seq // tm)

    row = lambda b, l: (b, l, 0)
    head_major = lambda b, l: (b, 0, l, 0, 0)
    const2 = lambda b, l: (0, 0)
    per_b = lambda b, l: (b, 0, 0)
    single = pl.Buffered(1)
    in_specs = [
        pl.BlockSpec((1, tm, d), row),
        pl.BlockSpec((1, 1, d), per_b),
        pl.BlockSpec((1, 1, d), per_b),
        pl.BlockSpec(w_zx.shape, const2, pipeline_mode=single),
        pl.BlockSpec(w_k.shape, const2, pipeline_mode=single),
        pl.BlockSpec(w_za.shape, const2, pipeline_mode=single),
        pl.BlockSpec((2 * d_attn, d), const2, pipeline_mode=single),
        pl.BlockSpec((V7X_LANES, d), const2),
        pl.BlockSpec((CONV_WIDTH, d_xbc), const2),
        pl.BlockSpec((1, d_xbc), const2),
        pl.BlockSpec((1, V7X_LANES), const2),
    ]
    out_specs = [
        pl.BlockSpec((1, tm, d_ssd), row),
        pl.BlockSpec((1, tm, d_ssd), row),
        pl.BlockSpec((1, tm, n_bc), row),
        pl.BlockSpec((1, tm, n_bc), row),
        pl.BlockSpec((1, nh_attn, 1, ATTN_V_DIM, 2 * tm), head_major),
        pl.BlockSpec((1, tm, d_attn), row),
        pl.BlockSpec((1, nh_attn, 1, ATTN_V_DIM, tm), head_major),
        pl.BlockSpec((1, tm, d_attn), row),
        pl.BlockSpec((1, tm, nh), row),
        pl.BlockSpec((1, nh, tm), lambda b, l: (b, 0, l)),
    ]
    out_shape = [
        jax.ShapeDtypeStruct((bsz, seq, d_ssd), BF16),
        jax.ShapeDtypeStruct((bsz, seq, d_ssd), BF16),
        jax.ShapeDtypeStruct((bsz, seq, n_bc), BF16),
        jax.ShapeDtypeStruct((bsz, seq, n_bc), BF16),
        jax.ShapeDtypeStruct((bsz, nh_attn, seq // tm, ATTN_V_DIM, 2 * tm), BF16),
        jax.ShapeDtypeStruct((bsz, seq, d_attn), BF16),
        jax.ShapeDtypeStruct((bsz, nh_attn, seq // tm, ATTN_V_DIM, tm), BF16),
        jax.ShapeDtypeStruct((bsz, seq, d_attn), BF16),
        jax.ShapeDtypeStruct((bsz, seq, nh), F32),
        jax.ShapeDtypeStruct((bsz, nh, seq), F32),
    ]
    pad = V7X_SUBLANES
    vmem = (d * (n_main + 2 * d_attn) * 2 + 2 * tm * d * 4
            + 2 * tm * (2 * d_ssd + 2 * n_bc + 4 * d_attn) * 2
            + (tm + 2 * pad) * d_xbc * 4 + 6 * tm * d_xbc * 4)
    kern = functools.partial(_inproj_kernel, tm=tm, rows_per_pass=TM_INPROJ_PASS, d_ssd=d_ssd,
                             d_xbc=d_xbc, d_attn=d_attn, n_bc=n_bc)
    assert 2 * nh <= V7X_LANES
    w_dt_pad = jnp.concatenate([w_dt, w_dt, jnp.zeros((V7X_LANES - 2 * nh, d), F32)], axis=0)
    dt_bias_pad = jnp.concatenate([dt_bias, jnp.zeros((V7X_LANES - nh,), F32)]).reshape(1, -1)
    return pl.pallas_call(
        kern, grid=grid, in_specs=in_specs, out_specs=out_specs, out_shape=out_shape,
        scratch_shapes=[pltpu.VMEM((pad, d_xbc), F32)],
        compiler_params=pltpu.CompilerParams(
            dimension_semantics=("arbitrary", "arbitrary"), vmem_limit_bytes=_vmem_limit(vmem)),
        name="in_projection",
    )(x, scale, shift, w_zx, w_k, w_za, w_qv_t, w_dt_pad, conv_w, conv_b.reshape(1, -1), dt_bias_pad)


def _ssd_kernel(xs_ref, bm_ref, cm_ref, zs_ref, dt_ref, dtT_ref, alog_ref, alogT_ref, dskip_ref,
                normw_ref, y_ref, state_ref, ybuf_ref, *, nheads, d_ssd, cps):
    t = CHUNK
    n = SSD_STATE
    hp = SSD_HEAD_DIM
    gw = d_ssd // SSD_GROUPS
    heads_per_group = nheads // SSD_GROUPS

    @pl.when(pl.program_id(1) == 0)
    def _():
        state_ref[...] = jnp.zeros_like(state_ref)

    row = lax.broadcasted_iota(jnp.int32, (t, t), 0)
    col = lax.broadcasted_iota(jnp.int32, (t, t), 1)
    causal = col <= row
    tri = jnp.where(causal, 1.0, 0.0).astype(BF16)
    tri_t = jnp.where(row <= col, 1.0, 0.0).astype(BF16)
    eh = lax.broadcasted_iota(jnp.int32, (nheads, d_ssd), 0)
    ej = lax.broadcasted_iota(jnp.int32, (nheads, d_ssd), 1)
    expand_m = jnp.where(ej // hp == eh, 1.0, 0.0).astype(BF16)
    lane = lax.broadcasted_iota(jnp.int32, (1, 2 * hp), 1)
    neg_inf = jnp.float32(-jnp.inf)
    a = -jnp.exp(alog_ref[...])
    a_t = -jnp.exp(alogT_ref[...])

    def expand(v, pieces):
        return sum(_dot(p, expand_m) for p in _split_bf16(v, pieces))

    for ci in range(cps):
        rows = slice(ci * t, (ci + 1) * t)
        dt = dt_ref[0, rows, :]
        dt_t = dtT_ref[0, :, rows]
        acum = sum(_dot(tri, p) for p in _split_bf16(dt * a, 3))
        acum_t = sum(_dot(p, tri_t) for p in _split_bf16(dt_t * a_t, 3))
        last = acum[t - 1:t, :]
        dt_x = expand(dt, 1)
        w_x = expand(jnp.exp(last - acum), 1)
        e_x = expand(jnp.exp(acum), 2)

        xs = xs_ref[0, rows, :].astype(F32)
        xdt = xs * dt_x
        xdt_b = xdt.astype(BF16)
        xw_b = (xdt * w_x).astype(BF16)

        for g in range(SSD_GROUPS):
            b_g = bm_ref[0, rows, g * n:(g + 1) * n]
            c_g = cm_ref[0, rows, g * n:(g + 1) * n]
            gsl = slice(g * gw, (g + 1) * gw)
            cb = _dot_nt(c_g, b_g)
            prev = state_ref[g]
            y_off = _dot(c_g, prev.astype(BF16)) * e_x[:, gsl]
            state_ref[g] = prev * e_x[t - 1:t, gsl] + _dot_tn(b_g, xw_b[:, gsl])
            for pair in range(heads_per_group // 2):
                lo = g * gw + pair * 2 * hp
                x_pair = xdt_b[:, lo:lo + 2 * hp]
                decays = []
                for half in range(2):
                    hd = g * heads_per_group + pair * 2 + half
                    diff = acum[:, hd:hd + 1] - acum_t[hd:hd + 1, :]
                    decays.append((cb * jnp.exp(jnp.where(causal, diff, neg_inf))).astype(BF16))
                zero = jnp.zeros_like(x_pair)
                x_diag = jnp.concatenate([jnp.where(lane < hp, x_pair, zero),
                                          jnp.where(lane >= hp, x_pair, zero)], axis=0)
                y_pair = (y_off[:, pair * 2 * hp:(pair + 1) * 2 * hp]
                          + _dot(jnp.concatenate(decays, axis=1), x_diag))
                ybuf_ref[rows, lo:lo + 2 * hp] = y_pair

    xs = xs_ref[0].astype(F32)
    y = ybuf_ref[...] + dskip_ref[...] * xs
    y = y * zs_ref[0].astype(F32)
    for g in range(SSD_GROUPS):
        gsl = slice(g * gw, (g + 1) * gw)
        y_g = y[:, gsl]
        ms = jnp.mean(y_g * y_g, axis=-1, keepdims=True)
        y_ref[0, :, gsl] = (y_g * lax.rsqrt(ms + EPS) * normw_ref[:, gsl]).astype(BF16)


def _ssd(xs, bm, cm, zs, dt, dt_t, a_log, d_skip, ssd_norm_w):
    bsz, seq, d_ssd = xs.shape
    nheads = dt.shape[-1]
    n_bc = bm.shape[-1]
    cps = SSD_CHUNKS_PER_STEP
    t = cps * CHUNK
    assert seq % t == 0
    grid = (bsz, seq // t)
    row = lambda b, c: (b, c, 0)
    const2 = lambda b, c: (0, 0)
    in_specs = [
        pl.BlockSpec((1, t, d_ssd), row),
        pl.BlockSpec((1, t, n_bc), row),
        pl.BlockSpec((1, t, n_bc), row),
        pl.BlockSpec((1, t, d_ssd), row),
        pl.BlockSpec((1, t, nheads), row),
        pl.BlockSpec((1, nheads, t), lambda b, c: (b, 0, c)),
        pl.BlockSpec((1, nheads), const2),
        pl.BlockSpec((nheads, 1), const2),
        pl.BlockSpec((1, d_ssd), const2),
        pl.BlockSpec((1, d_ssd), const2),
    ]
    kern = functools.partial(_ssd_kernel, nheads=nheads, d_ssd=d_ssd, cps=cps)
    return pl.pallas_call(
        kern, grid=grid, in_specs=in_specs,
        out_specs=pl.BlockSpec((1, t, d_ssd), row),
        out_shape=jax.ShapeDtypeStruct((bsz, seq, d_ssd), BF16),
        scratch_shapes=[pltpu.VMEM((SSD_GROUPS, SSD_STATE, d_ssd // SSD_GROUPS), F32),
                        pltpu.VMEM((t, d_ssd), F32)],
        compiler_params=pltpu.CompilerParams(dimension_semantics=("arbitrary", "arbitrary")),
        name="ssd_scan",
    )(xs, bm, cm, zs, dt, dt_t, a_log.reshape(1, -1), a_log.reshape(-1, 1),
      jnp.repeat(d_skip, SSD_HEAD_DIM).reshape(1, -1), ssd_norm_w.reshape(1, -1))


def _attn_kernel(qT_ref, k_ref, vT_ref, za_ref, lq1_ref, lk1_ref, lq2_ref, lk2_ref, normw_ref,
                 o_ref, s_ref, cmax_ref, m_ref, acc_ref, *, tk, nsub, nq):
    step = pl.program_id(2)
    dk = ATTN_QK_DIM
    dv = 2 * dk
    sw = 2 * tk
    nchunk = 2 * nsub
    tq = nsub * tk
    half = tk // 2
    lead = ATTN_PRODUCE_LEAD
    per_trip = ATTN_BLOCKS_PER_TRIP
    assert lead <= nchunk and per_trip % 2 == 0 and nsub % 2 == 0 and (nsub * nq) % per_trip == 0
    ones_rows = jnp.ones((BF16_SUBLANE_PACK, tk), BF16)
    lam = (jnp.exp(jnp.sum(lq1_ref[...] * lk1_ref[...], axis=-1, keepdims=True))
           - jnp.exp(jnp.sum(lq2_ref[...] * lk2_ref[...], axis=-1, keepdims=True))
           + LAMBDA_INIT_L0)

    def col(c):
        return slice(c * tk, (c + 1) * tk)

    def query_block(g):
        sc = g % ATTN_SCRATCH_SETS

        def q_columns(c):
            return qT_ref[0, 0, g * nsub + c // 2, :, (c % 2) * tk:(c % 2 + 1) * tk]

        def produce(slot, kv, c):
            start = pl.multiple_of(kv * tk, tk)
            s = _dot(k_ref[0, pl.ds(start, tk), :], q_columns(c))
            s_ref[sc, slot, :, col(c)] = s
            cmax_ref[sc, slot, :, col(c)] = jnp.max(s, axis=0, keepdims=True)

        def produce_diagonal(slot, kv, c):
            start = pl.multiple_of(kv * tk, tk)
            lo = c * tk
            s_ref[sc, slot, 0:half, lo:lo + tk] = _dot(k_ref[0, pl.ds(start, half), :], q_columns(c))
            s_ref[sc, slot, half:tk, lo + half:lo + tk] = _dot(
                k_ref[0, pl.ds(start + half, half), :], q_columns(c)[:, half:tk])

        def update(s, cmax, kv, cols, nrows):
            m_prev = m_ref[sc, :, cols]
            m_new = jnp.maximum(m_prev, cmax)
            alpha = jnp.exp2(m_prev - m_new)
            p = jnp.exp2(s - m_new).astype(BF16)
            v_ext = jnp.concatenate([vT_ref[0, 0, kv, :, 0:nrows], ones_rows[:, 0:nrows]], axis=0)
            acc_ref[sc, :, cols] = alpha * acc_ref[sc, :, cols] + _dot(v_ext, p)
            m_ref[sc, :, cols] = m_new

        def consume(slot, kv, c, diagonal=False):
            if not diagonal:
                update(s_ref[sc, slot, :, col(c)], cmax_ref[sc, slot, :, col(c)], kv, col(c), tk)
                return
            lo = c * tk
            for cols, nrows in ((slice(lo, lo + half), half), (slice(lo + half, lo + tk), tk)):
                s = s_ref[sc, slot, 0:nrows, cols]
                r = lax.broadcasted_iota(jnp.int32, s.shape, 0)
                qpos = lax.broadcasted_iota(jnp.int32, s.shape, 1) + (cols.start - lo)
                s = jnp.where(r <= qpos, s, -jnp.inf)
                update(s, jnp.max(s, axis=0, keepdims=True), kv, cols, nrows)

        def finalize(sub):
            c1 = sub * sw
            c2 = c1 + tk
            o1 = acc_ref[sc, 0:dv, c1:c1 + tk] / acc_ref[sc, dv:dv + 1, c1:c1 + tk]
            o2 = acc_ref[sc, 0:dv, c2:c2 + tk] / acc_ref[sc, dv:dv + 1, c2:c2 + tk]
            o = (o1 - lam * o2).T
            ms = jnp.mean(o * o, axis=-1, keepdims=True)
            o = o * lax.rsqrt(ms + EPS) * normw_ref[...] * (1.0 - LAMBDA_INIT_L0)
            rows = slice(g * tq + sub * tk, g * tq + (sub + 1) * tk)
            o_ref[0, rows, :] = (o * za_ref[0, rows, :].astype(F32)).astype(BF16)

        m_ref[sc] = jnp.full(m_ref.shape[1:], -jnp.inf, F32)
        acc_ref[sc] = jnp.zeros(acc_ref.shape[1:], F32)

        for c in range(lead):
            produce(0, 0, c)

        def off_diagonal(kv, nblocks):
            for idx in range(nblocks * nchunk):
                ahead = idx + lead
                produce((ahead // nchunk) % 2, kv + ahead // nchunk, ahead % nchunk)
                consume((idx // nchunk) % 2, kv + idx // nchunk, idx % nchunk)

        trips = (nsub * nq // per_trip) * step + (nsub * g) // per_trip
        left = (nsub * g) % per_trip

        def trip(i, carry):
            off_diagonal(i * per_trip, per_trip)
            return carry

        lax.fori_loop(0, trips, trip, 0)
        if left:
            off_diagonal(trips * per_trip, left)

        kv0 = trips * per_trip + left
        units = [(d, c) for d in range(nsub) for c in range(nchunk) if c // 2 >= d]
        assert units[:lead] == [(0, c) for c in range(lead)]
        for i, (d, c) in enumerate(units):
            if i + lead < len(units):
                nd, nc = units[i + lead]
                (produce_diagonal if nc // 2 == nd else produce)(nd % 2, kv0 + nd, nc)
            consume(d % 2, kv0 + d, c, diagonal=(c // 2 == d))
            if c % 2 == 1 and c // 2 == d:
                finalize(d)

    for g in range(nq):
        query_block(g)


def _attention(q_t, k, v_t, za, lq1, lk1, lq2, lk2, attn_norm_w):
    bsz, seq, d_attn = k.shape
    hw = ATTN_V_DIM
    nh = d_attn // hw
    tk = TK_ATTN
    nsub = NSUB_ATTN
    nq = NQ_ATTN
    tq = nsub * tk
    rows = nq * tq
    nblk = seq // tk
    assert nsub % 2 == 0 and seq % rows == 0
    assert q_t.shape == (bsz, nh, nblk, hw, 2 * tk) and v_t.shape == (bsz, nh, nblk, hw, tk)
    grid = (bsz, nh, seq // rows)
    qrow = lambda b, h, i: (b, i, h)
    const2 = lambda b, h, i: (0, 0)
    vec = pl.BlockSpec((1, ATTN_QK_DIM), const2)
    in_specs = [
        pl.BlockSpec((1, 1, nq * nsub, hw, 2 * tk), lambda b, h, i: (b, h, i, 0, 0)),
        pl.BlockSpec((1, seq, hw), lambda b, h, i: (b, 0, h)),
        pl.BlockSpec((1, 1, nblk, hw, tk), lambda b, h, i: (b, h, 0, 0, 0)),
        pl.BlockSpec((1, rows, hw), qrow),
        vec, vec, vec, vec,
        pl.BlockSpec((1, hw), const2),
    ]
    width = 2 * tq
    nscr = ATTN_SCRATCH_SETS
    assert nscr >= min(nq, 2)
    vmem = (2 * 2 * seq * hw * 2 + nscr * 2 * tk * width * 4
            + nscr * (2 * hw + BF16_SUBLANE_PACK) * width * 4 + 4 * tk * width * 4)
    return pl.pallas_call(
        functools.partial(_attn_kernel, tk=tk, nsub=nsub, nq=nq), grid=grid, in_specs=in_specs,
        out_specs=pl.BlockSpec((1, rows, hw), qrow),
        out_shape=jax.ShapeDtypeStruct((bsz, seq, d_attn), BF16),
        scratch_shapes=[pltpu.VMEM((nscr, 2, tk, width), F32),
                        pltpu.VMEM((nscr, 2, 1, width), F32), pltpu.VMEM((nscr, 1, width), F32),
                        pltpu.VMEM((nscr, hw + BF16_SUBLANE_PACK, width), F32)],
        compiler_params=pltpu.CompilerParams(
            dimension_semantics=("arbitrary", "arbitrary", "arbitrary"),
            vmem_limit_bytes=_vmem_limit(vmem)),
        name="diff_attention",
    )(q_t, k, v_t, za, lq1.reshape(1, -1), lk1.reshape(1, -1), lq2.reshape(1, -1),
      lk2.reshape(1, -1), attn_norm_w.reshape(1, -1))


def _outproj_kernel(ys_ref, ya_ref, w32_ref, x_ref, gate_ref, g_ref, b_ref, o_ref, w_ref, *, d_ssd,
                    alpha, tm, rows_per_pass):
    @pl.when((pl.program_id(0) == 0) & (pl.program_id(1) == 0))
    def _():
        w_ref[...] = w32_ref[...].astype(BF16)

    for r0 in range(0, tm, rows_per_pass):
        rows = slice(r0, r0 + rows_per_pass)
        mixed = (_dot(ys_ref[0, rows, :], w_ref[0:d_ssd, :])
                 + _dot(ya_ref[0, rows, :], w_ref[d_ssd:, :]))
        xf = alpha * x_ref[0, rows, :] + gate_ref[0] * mixed
        mu = jnp.mean(xf, axis=-1, keepdims=True)
        xc = xf - mu
        var = jnp.mean(xc * xc, axis=-1, keepdims=True)
        o_ref[0, rows, :] = xc * lax.rsqrt(var + EPS) * g_ref[...] + b_ref[...]


def _out_projection(y_ssd, y_attn, w_out, x, gate, ln_g, ln_b, alpha):
    bsz, seq, d = x.shape
    d_ssd = y_ssd.shape[-1]
    d_attn = y_attn.shape[-1]
    tm = TM_OUT
    row = lambda b, l: (b, l, 0)
    const2 = lambda b, l: (0, 0)
    in_specs = [
        pl.BlockSpec((1, tm, d_ssd), row),
        pl.BlockSpec((1, tm, d_attn), row),
        pl.BlockSpec((d_ssd + d_attn, d), const2, pipeline_mode=pl.Buffered(1)),
        pl.BlockSpec((1, tm, d), row),
        pl.BlockSpec((1, 1, d), lambda b, l: (b, 0, 0)),
        pl.BlockSpec((1, d), const2),
        pl.BlockSpec((1, d), const2),
    ]
    vmem = ((d_ssd + d_attn) * d * (4 + 2) + 2 * tm * (d_ssd + d_attn) * 2 + 4 * tm * d * 4
            + 4 * tm * d * 4)
    return pl.pallas_call(
        functools.partial(_outproj_kernel, d_ssd=d_ssd, alpha=alpha, tm=tm,
                          rows_per_pass=TM_OUT_PASS),
        grid=(bsz, seq // tm), in_specs=in_specs,
        out_specs=pl.BlockSpec((1, tm, d), row),
        out_shape=jax.ShapeDtypeStruct((bsz, seq, d), F32),
        scratch_shapes=[pltpu.VMEM((d_ssd + d_attn, d), BF16)],
        compiler_params=pltpu.CompilerParams(
            dimension_semantics=("arbitrary", "arbitrary"), vmem_limit_bytes=_vmem_limit(vmem)),
        name="out_projection",
    )(y_ssd, y_attn, w_out, x, gate, ln_g.reshape(1, -1), ln_b.reshape(1, -1))


def _layer(x, c, w_ada, b_ada, w_in, conv_w, conv_b, dt_bias, a_log, d_skip, ssd_norm_w,
           lambda_q1, lambda_k1, lambda_q2, lambda_k2, attn_norm_w, w_out, ln_g, ln_b, depth):
    bsz, seq, d = x.shape
    nheads = a_log.shape[0]
    d_ssd = nheads * SSD_HEAD_DIM
    d_xbc = conv_w.shape[1]
    d_attn = w_out.shape[0] - d_ssd
    assert d_xbc == d_ssd + 2 * SSD_GROUPS * SSD_STATE
    assert w_in.shape[1] == d_ssd + d_xbc + nheads + 4 * d_attn
    assert seq % TM_INPROJ == 0 and seq % TM_OUT == 0 and seq % CHUNK == 0
    assert TM_INPROJ == TK_ATTN, "the in-projection writes q^T / v^T in attention-sized time blocks"

    mod = _modulation(c, w_ada, b_ada)
    shift = mod[:, None, 0:d]
    scale = mod[:, None, d:2 * d]
    gate = mod[:, None, 2 * d:3 * d]

    off_dt = d_ssd + d_xbc
    off_q = off_dt + nheads
    off_k, off_v, off_za = off_q + d_attn, off_q + 2 * d_attn, off_q + 3 * d_attn
    w_t = w_in.T
    w_main = tuple(w_t[lo:hi].astype(BF16).T
                   for lo, hi in ((0, off_dt), (off_k, off_v), (off_za, off_za + d_attn)))
    w_qv_t = jnp.concatenate([w_t[off_q:off_k], w_t[off_v:off_za]], axis=0).astype(BF16)
    w_dt = w_t[off_dt:off_q]

    zs, xs, bm, cm, q_t, k, v_t, za, dt, dt_t = _in_projection(
        x, scale, shift, w_main, w_qv_t, w_dt, conv_w, conv_b, dt_bias,
        d_ssd=d_ssd, d_xbc=d_xbc, d_attn=d_attn)
    y_ssd = _ssd(xs, bm, cm, zs, dt, dt_t, a_log, d_skip, ssd_norm_w)
    y_attn = _attention(q_t, k, v_t, za, lambda_q1, lambda_k1, lambda_q2, lambda_k2, attn_norm_w)
    alpha = (2.0 * depth) ** 0.25
    return _out_projection(y_ssd, y_attn, w_out, x, gate, ln_g, ln_b, alpha)


def kernel(x, c, w_ada, b_ada, w_in, conv_w, conv_b, dt_bias, a_log, d_skip, ssd_norm_w, lambda_q1,
           lambda_k1, lambda_q2, lambda_k2, attn_norm_w, w_out, ln_g, ln_b):
    depth = w_in.shape[0]
    assert depth == 1, "lambda_init and the layer loop are specialised to a single layer"
    return _layer(x, c, w_ada[0], b_ada[0], w_in[0], conv_w[0], conv_b[0], dt_bias[0], a_log[0],
                  d_skip[0], ssd_norm_w[0], lambda_q1[0], lambda_k1[0], lambda_q2[0], lambda_k2[0],
                  attn_norm_w[0], w_out[0], ln_g[0], ln_b[0], depth)
```
